```python
import jax, jax.numpy as jnp
from jax import lax
import numpy as np

D_MODEL = 2048
BATCH = 1
SEQ = 16384
DEPTH = 2
DEC_BATCH = 8
DEC_SEQ = 16
PAST_LEN = 1024

CHUNK = 64
N_MIXERS = 2
N_SG_LAYERS = (DEPTH + 1) // 2
N_HG_LAYERS = DEPTH // 2
D_FF = 5632
SG_CHUNK = 128
D_SG = D_MODEL
SG_GROUPS = 16
SG_GROUP_DIM = D_SG // SG_GROUPS
HG_HEADS = 16
HG_DK = D_MODEL // HG_HEADS
HG_DV = D_MODEL // HG_HEADS
EPS = 1e-6

kernel_name = "hybrid_sgmlp_hgrn2_streaming_step"


def rms_norm(x, g):
    xf = x.astype(jnp.float32)
    y = xf * lax.rsqrt(jnp.mean(xf * xf, axis=-1, keepdims=True) + EPS)
    return (y * g.astype(jnp.float32)).astype(x.dtype)


def layer_norm(x, g, b):
    xf = x.astype(jnp.float32)
    mu = jnp.mean(xf, axis=-1, keepdims=True)
    var = jnp.mean(jnp.square(xf - mu), axis=-1, keepdims=True)
    y = (xf - mu) * lax.rsqrt(var + EPS)
    return (y * g.astype(jnp.float32) + b.astype(jnp.float32)).astype(x.dtype)


def swiglu_ffn(x, w_up, w_down):
    a, b = jnp.split(x @ w_up, 2, axis=-1)
    return (jax.nn.silu(a) * b) @ w_down


def spatial_gating_mixer(h, w_in, ln_g, ln_b, w_s, b_s, w_out):
    B, T, _ = h.shape
    u, v = jnp.split(jax.nn.gelu(h @ w_in, approximate=False), 2, axis=-1)
    v = layer_norm(v, ln_g, ln_b)
    L = min(T, SG_CHUNK)
    n = T // L
    blk = jnp.arange(SG_CHUNK) // CHUNK
    mask = (blk[:, None] >= blk[None, :])[:L, :L]
    w = jnp.where(mask[None], w_s[:, :L, :L], jnp.zeros((), w_s.dtype))
    vg = v.reshape(B, n, L, SG_GROUPS, SG_GROUP_DIM)
    s = jnp.einsum('gij,bnjgd->bnigd', w, vg) + b_s[:, :L].T[None, None, :, :, None]
    y = u * s.reshape(B, T, D_SG)
    return y @ w_out, v


def layer_lower_bound(hg_lower, layer_idx):
    lbs = jnp.cumsum(jax.nn.softmax(hg_lower.astype(jnp.float32), axis=0), axis=0)
    lbs = lbs - lbs[0]
    return lbs[layer_idx]


def hgrn_blocked_scan(S0, q, k, v, logf, block):
    B, T, H, K = q.shape
    n = T // block

    def to_blocks(a):
        return a.reshape(B, n, block, H, a.shape[-1]).swapaxes(0, 1)

    tril = jnp.tril(jnp.ones((block, block), bool))[None, :, :, None, None]

    def step(S, xs):
        qb, kb, vb, lfb = xs
        b = jnp.cumsum(lfb, axis=1)
        inter = jnp.einsum('bthk,bhkv->bthv', qb * jnp.exp(b), S)
        decay = jnp.exp(jnp.where(tril, b[:, :, None] - b[:, None], -jnp.inf))
        scores = jnp.einsum('bthk,btshk,bshk->btsh', qb, decay, kb)
        intra = jnp.einsum('btsh,bshv->bthv', scores, vb)
        bl = b[:, -1]
        S_new = jnp.exp(bl)[..., None] * S + jnp.einsum('bshk,bshv->bhkv', kb * jnp.exp(bl[:, None] - b), vb)
        return S_new, inter + intra

    S, o = lax.scan(step, S0, (to_blocks(q), to_blocks(k), to_blocks(v), to_blocks(logf)))
    return S, o.swapaxes(0, 1).reshape(B, T, H, v.shape[-1])


def hgrn2_mixer(h, w_in, lb, norm_g, w_out, S0, block):
    B, T, _ = h.shape
    q, fz, i_, gz = jnp.split(h @ w_in, 4, axis=-1)
    q = jax.nn.silu(q.astype(jnp.float32)).reshape(B, T, HG_HEADS, HG_DK)
    f = lb + (1.0 - lb) * jax.nn.sigmoid(fz.astype(jnp.float32))
    logf = jnp.log(f).reshape(B, T, HG_HEADS, HG_DK)
    k = (1.0 - f).reshape(B, T, HG_HEADS, HG_DK)
    v = i_.astype(jnp.float32).reshape(B, T, HG_HEADS, HG_DV)
    S, o = hgrn_blocked_scan(S0, q, k, v, logf, block)
    gate = jax.nn.silu(gz.astype(jnp.float32)).reshape(B, T, HG_HEADS, HG_DV)
    o = rms_norm(o, norm_g) * gate
    return o.reshape(B, T, D_MODEL).astype(h.dtype) @ w_out, S


def setup_inputs(seed: int = 0) -> dict:
    key = jax.random.key(seed)
    ks = jax.random.split(key, 16)

    def nrm(k, shape, scale):
        return jax.random.normal(k, shape, jnp.float32) * scale

    return {
        "x_prompt": nrm(ks[0], (BATCH, SEQ, D_MODEL), 1.0),
        "x_sample": nrm(ks[1], (DEC_BATCH, DEC_SEQ, D_MODEL), 1.0),
        "state_hgrn": nrm(ks[2], (N_HG_LAYERS, DEC_BATCH, HG_HEADS, HG_DK, HG_DV), 0.5),
        "norm_g": 1.0 + nrm(ks[3], (DEPTH, 6, D_MODEL), 0.05),
        "ffn_w_up": nrm(ks[4], (DEPTH, 2, D_MODEL, 2 * D_FF), D_MODEL ** -0.5),
        "ffn_w_down": nrm(ks[5], (DEPTH, 2, D_FF, D_MODEL), D_FF ** -0.5),
        "sg_w_in": nrm(ks[6], (N_SG_LAYERS, D_MODEL, 2 * D_SG), D_MODEL ** -0.5),
        "sg_ln_g": 1.0 + nrm(ks[7], (N_SG_LAYERS, D_SG), 0.05),
        "sg_ln_b": nrm(ks[8], (N_SG_LAYERS, D_SG), 0.02),
        "sg_w_s": nrm(ks[9], (N_SG_LAYERS, SG_GROUPS, SG_CHUNK, SG_CHUNK), SG_CHUNK ** -0.5),
        "sg_b_s": 1.0 + nrm(ks[10], (N_SG_LAYERS, SG_GROUPS, SG_CHUNK), 0.02),
        "sg_w_out": nrm(ks[11], (N_SG_LAYERS, D_SG, D_MODEL), D_SG ** -0.5),
        "hg_w_in": nrm(ks[12], (N_HG_LAYERS, D_MODEL, 4 * D_MODEL), D_MODEL ** -0.5),
        "hg_lower": nrm(ks[13], (DEPTH, D_MODEL), 0.1),
        "hg_norm_g": 1.0 + nrm(ks[14], (N_HG_LAYERS, HG_DV), 0.05),
        "hg_w_out": nrm(ks[15], (N_HG_LAYERS, D_MODEL, D_MODEL), D_MODEL ** -0.5),
    }


def reference(x_prompt, x_sample, state_hgrn, norm_g, ffn_w_up, ffn_w_down,
              sg_w_in, sg_ln_g, sg_ln_b, sg_w_s, sg_b_s, sg_w_out,
              hg_w_in, hg_lower, hg_norm_g, hg_w_out):

    def trunk(x, hg_init, block):
        hg_states, sg_vs = [], []
        for i in range(DEPTH):
            g = norm_g[i]
            j = i // N_MIXERS
            x = x + 0.5 * rms_norm(swiglu_ffn(rms_norm(x, g[0]), ffn_w_up[i, 0], ffn_w_down[i, 0]), g[1])
            h = rms_norm(x, g[2])
            if i % N_MIXERS == 0:
                m, v_rows = spatial_gating_mixer(h, sg_w_in[j], sg_ln_g[j], sg_ln_b[j],
                                                 sg_w_s[j], sg_b_s[j], sg_w_out[j])
                sg_vs.append(v_rows)
            else:
                lb = layer_lower_bound(hg_lower, i)
                m, S = hgrn2_mixer(h, hg_w_in[j], lb, hg_norm_g[j], hg_w_out[j], hg_init[j], block)
                hg_states.append(S)
            x = x + rms_norm(m, g[3])
            x = x + 0.5 * rms_norm(swiglu_ffn(rms_norm(x, g[4]), ffn_w_up[i, 1], ffn_w_down[i, 1]), g[5])
        return x, hg_states, sg_vs

    prompt_init = [jnp.zeros((x_prompt.shape[0], HG_HEADS, HG_DK, HG_DV), jnp.float32)
                   for _ in range(N_HG_LAYERS)]
    y_prompt, hg_p, _ = trunk(x_prompt, prompt_init, CHUNK)

    sample_init = [state_hgrn[j].astype(jnp.float32) for j in range(N_HG_LAYERS)]
    y_sample, hg_s, sg_v_s = trunk(x_sample, sample_init, x_sample.shape[1])

    state_hgrn_prompt = jnp.stack(hg_p, axis=0)
    state_hgrn_sample = jnp.stack(hg_s, axis=0)
    state_sg_v_sample = jnp.stack(sg_v_s, axis=0)
    return (y_prompt, y_sample, state_hgrn_prompt, state_hgrn_sample, state_sg_v_sample)
```

```python
import functools

import numpy as np
import jax
import jax.numpy as jnp
from jax import lax
from jax.experimental import pallas as pl
from jax.experimental.pallas import tpu as pltpu

EPS = 1e-6
F32 = jnp.float32
BF16 = jnp.bfloat16

LANES = 128
HEAD_DIM = 128
SG_CHUNK = 128
STREAM_CHUNK = 64
V7X_VMEM_BYTES = 64 * 1024 * 1024
VMEM_LIMIT_BYTES = V7X_VMEM_BYTES - 8 * 1024 * 1024


def _dot(a, b):
    return jnp.dot(a, b, preferred_element_type=F32)


def _dot_nt(a, b):
    return lax.dot_general(a, b, (((1,), (1,)), ((), ())), preferred_element_type=F32)


def _dot_tn(a, b):
    return lax.dot_general(a, b, (((0,), (0,)), ((), ())), preferred_element_type=F32)


def _rms(x, g):
    return x * lax.rsqrt(jnp.mean(x * x, axis=-1, keepdims=True) + EPS) * g


def _silu(x):
    return x * jax.nn.sigmoid(x)


def _gelu(x):
    return 0.5 * x * (1.0 + lax.erf(x * np.float32(np.sqrt(0.5))))


def _params():
    return pltpu.CompilerParams(dimension_semantics=("arbitrary", "arbitrary"),
                                vmem_limit_bytes=VMEM_LIMIT_BYTES)


def _ffn_body(x_ref, gpre_ref, gpost_ref, wa_ref, wb_ref, wd_ref, o_ref, h_scr, acc_scr):
    j = pl.program_id(1)

    @pl.when(j == 0)
    def _():
        h_scr[...] = _rms(x_ref[...], gpre_ref[...]).astype(BF16)

    h = h_scr[...]
    a = _dot(h, wa_ref[...])
    b = _dot(h, wb_ref[...])
    contrib = _dot((_silu(a) * b).astype(BF16), wd_ref[...])

    @pl.when(j == 0)
    def _():
        acc_scr[...] = contrib

    @pl.when(j > 0)
    def _():
        acc_scr[...] += contrib

    @pl.when(j == pl.num_programs(1) - 1)
    def _():
        o_ref[...] = x_ref[...] + 0.5 * _rms(acc_scr[...], gpost_ref[...])


def _ffn(x, g_pre, g_post, w_up, w_down, *, tm, tf):
    t, d = x.shape
    f = w_down.shape[0]
    nf = f // tf
    assert t % tm == 0 and f % tf == 0
    return pl.pallas_call(
        _ffn_body,
        out_shape=jax.ShapeDtypeStruct((t, d), F32),
        grid=(t // tm, nf),
        in_specs=[
            pl.BlockSpec((tm, d), lambda i, j: (i, 0)),
            pl.BlockSpec((1, d), lambda i, j: (0, 0)),
            pl.BlockSpec((1, d), lambda i, j: (0, 0)),
            pl.BlockSpec((d, tf), lambda i, j: (0, j)),
            pl.BlockSpec((d, tf), lambda i, j: (0, j + nf)),
            pl.BlockSpec((tf, d), lambda i, j: (j, 0)),
        ],
        out_specs=pl.BlockSpec((tm, d), lambda i, j: (i, 0)),
        scratch_shapes=[pltpu.VMEM((tm, d), BF16), pltpu.VMEM((tm, d), F32)],
        compiler_params=_params(),
    )(x, g_pre, g_post, w_up, w_up, w_down)


def _sg_mask(seg_mode):
    row = lax.broadcasted_iota(jnp.int32, (SG_CHUNK, SG_CHUNK), 0)
    col = lax.broadcasted_iota(jnp.int32, (SG_CHUNK, SG_CHUNK), 1)
    if seg_mode is None:
        return (row // STREAM_CHUNK) >= (col // STREAM_CHUNK)
    return (row // seg_mode) == (col // seg_mode)


def _sg_body(x_ref, g2_ref, g3_ref, wu_ref, wv_ref, lng_ref, lnb_ref, ws_ref, bias_ref, wo_ref,
             *rest, ng, gpt, nchunk, seg_mode, emit_v):
    if emit_v:
        o_ref, vout_ref, h_scr, u_scr, v_scr, y_scr, acc_scr = rest
    else:
        o_ref, h_scr, u_scr, v_scr, y_scr, acc_scr = rest
        vout_ref = None
    j = pl.program_id(1)
    tn = gpt * LANES
    d = ng * tn

    @pl.when(j == 0)
    def _():
        h_scr[...] = _rms(x_ref[...], g2_ref[...]).astype(BF16)

    @pl.when(j < ng)
    def _():
        h = h_scr[...]
        u_scr[j] = _gelu(_dot(h, wu_ref[...]))
        v_scr[j] = _gelu(_dot(h, wv_ref[...]))

    @pl.when(j == ng)
    def _():
        tot = jnp.sum(v_scr[0], axis=-1, keepdims=True)
        for jj in range(1, ng):
            tot += jnp.sum(v_scr[jj], axis=-1, keepdims=True)
        mu = tot * (1.0 / d)
        sq = jnp.sum(jnp.square(v_scr[0] - mu), axis=-1, keepdims=True)
        for jj in range(1, ng):
            sq += jnp.sum(jnp.square(v_scr[jj] - mu), axis=-1, keepdims=True)
        rstd = lax.rsqrt(sq * (1.0 / d) + EPS)
        for jj in range(ng):
            cols = slice(jj * tn, (jj + 1) * tn)
            vn = (v_scr[jj] - mu) * rstd * lng_ref[:, cols] + lnb_ref[:, cols]
            v_scr[jj] = vn
            if emit_v:
                vout_ref[:, cols] = vn

    @pl.when(j >= ng)
    def _():
        jj = j - ng
        mask = _sg_mask(seg_mode)
        for g in range(gpt):
            cols = slice(g * LANES, (g + 1) * LANES)
            w = jnp.where(mask, ws_ref[g], 0.0).astype(BF16)
            for n in range(nchunk):
                rows = slice(n * SG_CHUNK, (n + 1) * SG_CHUNK)
                s = _dot(w, v_scr[jj, rows, cols].astype(BF16)) + bias_ref[:, cols]
                y_scr[rows, cols] = (u_scr[jj, rows, cols] * s).astype(BF16)
        contrib = _dot(y_scr[...], wo_ref[...])

        @pl.when(j == ng)
        def _():
            acc_scr[...] = contrib

        @pl.when(j > ng)
        def _():
            acc_scr[...] += contrib

    @pl.when(j == 2 * ng - 1)
    def _():
        o_ref[...] = x_ref[...] + _rms(acc_scr[...], g3_ref[...])


def _sg_mixer(x, g2, g3, w_in, ln_g, ln_b, w_s, bias_rows, w_out, *, tm, gpt, seg_mode, emit_v):
    t, d = x.shape
    tn = gpt * LANES
    ng = d // tn
    assert t % tm == 0 and tm % SG_CHUNK == 0 and d % tn == 0
    last = ng - 1
    body = functools.partial(_sg_body, ng=ng, gpt=gpt, nchunk=tm // SG_CHUNK,
                             seg_mode=seg_mode, emit_v=emit_v)
    row_spec = pl.BlockSpec((tm, d), lambda i, j: (i, 0))
    vec_spec = pl.BlockSpec((1, d), lambda i, j: (0, 0))
    out_shape = jax.ShapeDtypeStruct((t, d), F32)
    return pl.pallas_call(
        body,
        out_shape=(out_shape, out_shape) if emit_v else out_shape,
        grid=(t // tm, 2 * ng),
        in_specs=[
            row_spec, vec_spec, vec_spec,
            pl.BlockSpec((d, tn), lambda i, j: (0, jnp.minimum(j, last))),
            pl.BlockSpec((d, tn), lambda i, j: (0, ng + jnp.minimum(j, last))),
            vec_spec, vec_spec,
            pl.BlockSpec((gpt, SG_CHUNK, SG_CHUNK), lambda i, j: (jnp.maximum(j - ng, 0), 0, 0)),
            pl.BlockSpec((SG_CHUNK, tn), lambda i, j: (0, jnp.maximum(j - ng, 0))),
            pl.BlockSpec((tn, d), lambda i, j: (jnp.maximum(j - ng, 0), 0)),
        ],
        out_specs=(row_spec, row_spec) if emit_v else row_spec,
        scratch_shapes=[
            pltpu.VMEM((tm, d), BF16),
            pltpu.VMEM((ng, tm, tn), F32),
            pltpu.VMEM((ng, tm, tn), F32),
            pltpu.VMEM((tm, tn), BF16),
            pltpu.VMEM((tm, d), F32),
        ],
        compiler_params=_params(),
    )(x, g2, g3, w_in, w_in, ln_g, ln_b, w_s, bias_rows, w_out)


def _scan_tables(chunk, seg):
    levels = []
    m = seg // 2
    while m >= 1:
        levels.append(m)
        m //= 2
    nb = 2 + len(levels)
    t = np.arange(chunk)[:, None]
    j = np.arange(chunk)[None, :]
    same_seg = (t // seg) == (j // seg)
    sums = np.zeros((nb, chunk, chunk), np.float32)
    sums[0] = same_seg & (j <= t)
    sums[1] = same_seg & (j > t)
    level_id = np.where(t == j, 0, -1).astype(np.int32)
    for li, m in enumerate(levels):
        g = 2 * m
        ref = (t // g) * g + m - 1
        odd = (t % g) >= m
        sums[2 + li] = np.where(odd, (j > ref) & (j <= t), (j > t) & (j <= ref))
        pair = ((t // g) == (j // g)) & odd & ((j % g) < m)
        level_id = np.where(pair, li + 1, level_id).astype(np.int32)
    sums = sums.reshape(nb * chunk, chunk)
    return np.concatenate([sums, sums, sums], axis=1), level_id, nb


def _split3(x):
    hi = x.astype(BF16)
    r1 = x - hi.astype(F32)
    mid = r1.astype(BF16)
    lo = (r1 - mid.astype(F32)).astype(BF16)
    return jnp.concatenate([hi, mid, lo], axis=0)


def _hgrn_body(x_ref, g2_ref, g3_ref, wq_ref, wf_ref, wi_ref, wg_ref, hl_ref, ng_ref, wo_ref,
               sums_ref, lid_ref, s0_ref, o_ref, sout_ref,
               h_scr, q_scr, lf_scr, k_scr, v_scr, gate_scr, e_scr, st_scr, on_scr, acc_scr,
               *, layer, chunk, seg, nh, nb, nchunk):
    i = pl.program_id(0)
    j = pl.program_id(1)
    nseg = chunk // seg
    nlev = nb - 2

    @pl.when(j == 0)
    def _():
        h_scr[...] = _rms(x_ref[...], g2_ref[...]).astype(BF16)

    @pl.when(i == 0)
    def _():
        for s in range(nseg):
            for hh in range(nh):
                st_scr[s, j * nh + hh] = s0_ref[s, j * nh + hh].T

    hl = hl_ref[...]
    e = jnp.exp(hl - jnp.max(hl, axis=0, keepdims=True))
    p = e / jnp.sum(e, axis=0, keepdims=True)
    lb = jnp.zeros_like(p[0:1])
    for r in range(1, layer + 1):
        lb = lb + p[r:r + 1]

    h = h_scr[...]
    q_scr[...] = _silu(_dot(h, wq_ref[...]))
    f = lb + (1.0 - lb) * jax.nn.sigmoid(_dot(h, wf_ref[...]))
    lf_scr[...] = jnp.log(f)
    k_scr[...] = 1.0 - f
    v_scr[...] = _dot(h, wi_ref[...]).astype(BF16)
    gate_scr[...] = _silu(_dot(h, wg_ref[...]))

    lid = lid_ref[...]
    for c in range(nchunk):
        rows = slice(c * chunk, (c + 1) * chunk)
        e_scr[...] = jnp.exp(_dot(sums_ref[...], _split3(lf_scr[rows, :])))
        for hh in range(nh):
            cols = slice(hh * HEAD_DIM, (hh + 1) * HEAD_DIM)
            head = j * nh + hh
            qh = q_scr[rows, cols]
            kh = k_scr[rows, cols]
            vh = v_scr[rows, cols]
            e_pre = e_scr[0:chunk, cols]
            k_end = (kh * e_scr[chunk:2 * chunk, cols]).astype(BF16)
            q_pre = (qh * e_pre).astype(BF16)
            sc = jnp.where(lid == 0, _dot_nt(qh.astype(BF16), kh.astype(BF16)), 0.0)
            for l in range(nlev):
                el = e_scr[(2 + l) * chunk:(3 + l) * chunk, cols]
                sc_l = _dot_nt((qh * el).astype(BF16), (kh * el).astype(BF16))
                sc = jnp.where(lid == l + 1, sc_l, sc)
            o = _dot(sc.astype(BF16), vh)
            inter = []
            for s in range(nseg):
                srows = slice(s * seg, (s + 1) * seg)
                st = st_scr[s, head]
                inter.append(_dot_nt(q_pre[srows], st.astype(BF16)))
                e_end = e_pre[(s + 1) * seg - 1:(s + 1) * seg, :]
                st_scr[s, head] = st * e_end + _dot_tn(vh[srows], k_end[srows])
            o = o + (inter[0] if nseg == 1 else jnp.concatenate(inter, axis=0))
            on = _rms(o, ng_ref[...]) * gate_scr[rows, cols]
            on_scr[rows, cols] = on.astype(BF16)

    @pl.when(i == pl.num_programs(0) - 1)
    def _():
        for s in range(nseg):
            for hh in range(nh):
                sout_ref[s, j * nh + hh] = st_scr[s, j * nh + hh].T

    contrib = _dot(on_scr[...], wo_ref[...])

    @pl.when(j == 0)
    def _():
        acc_scr[...] = contrib

    @pl.when(j > 0)
    def _():
        acc_scr[...] += contrib

    @pl.when(j == pl.num_programs(1) - 1)
    def _():
        o_ref[...] = x_ref[...] + _rms(acc_scr[...], g3_ref[...])


def _hgrn_mixer(x, g2, g3, w_in, hg_lower, norm_g, w_out, s0, *, layer, tm, chunk, seg, nh):
    t, d = x.shape
    heads = d // HEAD_DIM
    nseg = chunk // seg
    tn = nh * HEAD_DIM
    nhg = heads // nh
    assert t % tm == 0 and tm % chunk == 0 and heads % nh == 0
    assert s0.shape == (nseg, heads, HEAD_DIM, HEAD_DIM)
    assert nseg == 1 or t == chunk
    sums, level_id, nb = _scan_tables(chunk, seg)
    body = functools.partial(_hgrn_body, layer=layer, chunk=chunk, seg=seg, nh=nh, nb=nb,
                             nchunk=tm // chunk)
    row_spec = pl.BlockSpec((tm, d), lambda i, j: (i, 0))
    vec_spec = pl.BlockSpec((1, d), lambda i, j: (0, 0))
    nsec = d // tn
    state_spec = pl.BlockSpec(s0.shape, lambda i, j: (0, 0, 0, 0))
    return pl.pallas_call(
        body,
        out_shape=(jax.ShapeDtypeStruct((t, d), F32), jax.ShapeDtypeStruct(s0.shape, F32)),
        grid=(t // tm, nhg),
        in_specs=[
            row_spec, vec_spec, vec_spec,
            pl.BlockSpec((d, tn), lambda i, j: (0, j)),
            pl.BlockSpec((d, tn), lambda i, j: (0, nsec + j)),
            pl.BlockSpec((d, tn), lambda i, j: (0, 2 * nsec + j)),
            pl.BlockSpec((d, tn), lambda i, j: (0, 3 * nsec + j)),
            pl.BlockSpec((hg_lower.shape[0], tn), lambda i, j: (0, j)),
            pl.BlockSpec((1, HEAD_DIM), lambda i, j: (0, 0)),
            pl.BlockSpec((tn, d), lambda i, j: (j, 0)),
            pl.BlockSpec(sums.shape, lambda i, j: (0, 0)),
            pl.BlockSpec(level_id.shape, lambda i, j: (0, 0)),
            state_spec,
        ],
        out_specs=(row_spec, state_spec),
        scratch_shapes=[
            pltpu.VMEM((tm, d), BF16),
            pltpu.VMEM((tm, tn), F32),
            pltpu.VMEM((tm, tn), F32),
            pltpu.VMEM((tm, tn), F32),
            pltpu.VMEM((tm, tn), BF16),
            pltpu.VMEM((tm, tn), F32),
            pltpu.VMEM((nb * chunk, tn), F32),
            pltpu.VMEM((nseg, heads, HEAD_DIM, HEAD_DIM), F32),
            pltpu.VMEM((tm, tn), BF16),
            pltpu.VMEM((tm, d), F32),
        ],
        compiler_params=_params(),
    )(x, g2, g3, w_in, w_in, w_in, w_in, hg_lower, norm_g, w_out,
      jnp.asarray(sums, BF16), jnp.asarray(level_id), s0)


def _trunk(x, hg_init, p, *, tm, tf, sample):
    depth = p["norm_g"].shape[0]
    n_hg = p["hg_w_in"].shape[0]
    n_mix = 2
    hg_states, sg_vs = [], []
    for i in range(depth):
        g = p["norm_g"][i]
        row = lambda r: g[r:r + 1]
        jm = i // n_mix
        x = _ffn(x, row(0), row(1), p["ffn_w_up"][i, 0], p["ffn_w_down"][i, 0], tm=tm, tf=tf)
        if i % n_mix == 0:
            w_s, b_s = p["sg_w_s"][jm], p["sg_b_s"][jm]
            if sample is not None:
                n_streams, seq = sample
                w_s = jnp.tile(w_s[:, :seq, :seq], (1, n_streams, n_streams))
                b_s = jnp.tile(b_s[:, :seq], (1, n_streams))
            bias_rows = jnp.repeat(b_s.T, LANES, axis=1)
            res = _sg_mixer(x, row(2), row(3), p["sg_w_in"][jm], p["sg_ln_g"][jm:jm + 1],
                            p["sg_ln_b"][jm:jm + 1], w_s, bias_rows, p["sg_w_out"][jm],
                            tm=tm, gpt=4, seg_mode=None if sample is None else sample[1],
                            emit_v=sample is not None)
            if sample is not None:
                x, v_rows = res
                sg_vs.append(v_rows)
            else:
                x = res
        else:
            x, st = _hgrn_mixer(x, row(2), row(3), p["hg_w_in"][jm], p["hg_lower"],
                                p["hg_norm_g"][jm:jm + 1], p["hg_w_out"][jm], hg_init[jm],
                                layer=i, tm=tm, chunk=SG_CHUNK if sample is None else x.shape[0],
                                seg=SG_CHUNK if sample is None else sample[1], nh=2)
            hg_states.append(st)
        x = _ffn(x, row(4), row(5), p["ffn_w_up"][i, 1], p["ffn_w_down"][i, 1], tm=tm, tf=tf)
    return x, hg_states, sg_vs


def kernel(x_prompt, x_sample, state_hgrn, norm_g, ffn_w_up, ffn_w_down, sg_w_in, sg_ln_g, sg_ln_b,
           sg_w_s, sg_b_s, sg_w_out, hg_w_in, hg_lower, hg_norm_g, hg_w_out):
    batch, seq, d = x_prompt.shape
    dec_batch, dec_seq, _ = x_sample.shape
    n_hg = hg_w_in.shape[0]
    heads = d // HEAD_DIM
    p = dict(norm_g=norm_g, ffn_w_up=ffn_w_up.astype(BF16), ffn_w_down=ffn_w_down.astype(BF16),
             sg_w_in=sg_w_in.astype(BF16), sg_ln_g=sg_ln_g, sg_ln_b=sg_ln_b, sg_w_s=sg_w_s,
             sg_b_s=sg_b_s, sg_w_out=sg_w_out.astype(BF16), hg_w_in=hg_w_in.astype(BF16),
             hg_lower=hg_lower, hg_norm_g=hg_norm_g, hg_w_out=hg_w_out.astype(BF16))

    ys, sts = [], []
    for b in range(batch):
        init = [jnp.zeros((1, heads, HEAD_DIM, HEAD_DIM), F32) for _ in range(n_hg)]
        y, hg_p, _ = _trunk(x_prompt[b], init, p, tm=512, tf=512, sample=None)
        ys.append(y)
        sts.append(jnp.stack([s[0] for s in hg_p], axis=0))
    y_prompt = jnp.stack(ys, axis=0)
    state_hgrn_prompt = jnp.stack(sts, axis=1)

    init = [state_hgrn[jm].astype(F32) for jm in range(n_hg)]
    y, hg_s, sg_v = _trunk(x_sample.reshape(dec_batch * dec_seq, d), init, p,
                           tm=dec_batch * dec_seq, tf=512, sample=(dec_batch, dec_seq))
    y_sample = y.reshape(dec_batch, dec_seq, d)
    state_hgrn_sample = jnp.stack(hg_s, axis=0)
    state_sg_v_sample = jnp.stack([v.reshape(dec_batch, dec_seq, d) for v in sg_v], axis=0)
    return (y_prompt, y_sample, state_hgrn_prompt, state_hgrn_sample, state_sg_v_sample)
```

```python
import functools

import numpy as np
import jax
import jax.numpy as jnp
from jax import lax
from jax.experimental import pallas as pl
from jax.experimental.pallas import tpu as pltpu

EPS = 1e-6
F32 = jnp.float32
BF16 = jnp.bfloat16

LANES = 128
HEAD_DIM = 128
SG_CHUNK = 128
STREAM_CHUNK = 64
V7X_VMEM_BYTES = 64 * 1024 * 1024
VMEM_LIMIT_BYTES = V7X_VMEM_BYTES - 8 * 1024 * 1024
MAX_SPLIT_DECAY = 140.0


def _dot(a, b):
    return jnp.dot(a, b, preferred_element_type=F32)


def _dot_nt(a, b):
    return lax.dot_general(a, b, (((1,), (1,)), ((), ())), preferred_element_type=F32)


def _dot_tn(a, b):
    return lax.dot_general(a, b, (((0,), (0,)), ((), ())), preferred_element_type=F32)


def _rms(x, g):
    return x * lax.rsqrt(jnp.mean(x * x, axis=-1, keepdims=True) + EPS) * g


def _silu(x):
    return x * jax.nn.sigmoid(x)


def _gelu(x):
    return 0.5 * x * (1.0 + lax.erf(x * np.float32(np.sqrt(0.5))))


def _params():
    return pltpu.CompilerParams(dimension_semantics=("arbitrary", "arbitrary"),
                                vmem_limit_bytes=VMEM_LIMIT_BYTES)


def _ffn_body(x_ref, gpre_ref, gpost_ref, wa_ref, wb_ref, wd_ref, o_ref, h_scr, acc_scr):
    j = pl.program_id(1)

    @pl.when(j == 0)
    def _():
        h_scr[...] = _rms(x_ref[...], gpre_ref[...]).astype(BF16)
        acc_scr[...] = jnp.zeros_like(acc_scr)

    h = h_scr[...]
    a = _dot(h, wa_ref[...])
    b = _dot(h, wb_ref[...])
    acc_scr[...] += _dot((_silu(a) * b).astype(BF16), wd_ref[...])

    @pl.when(j == pl.num_programs(1) - 1)
    def _():
        o_ref[...] = x_ref[...] + 0.5 * _rms(acc_scr[...], gpost_ref[...])


def _ffn(x, g_pre, g_post, w_up, w_down, *, tm, tf):
    t, d = x.shape
    f = w_down.shape[0]
    nf = f // tf
    assert t % tm == 0 and f % tf == 0
    return pl.pallas_call(
        _ffn_body,
        out_shape=jax.ShapeDtypeStruct((t, d), F32),
        grid=(t // tm, nf),
        in_specs=[
            pl.BlockSpec((tm, d), lambda i, j: (i, 0)),
            pl.BlockSpec((1, d), lambda i, j: (0, 0)),
            pl.BlockSpec((1, d), lambda i, j: (0, 0)),
            pl.BlockSpec((d, tf), lambda i, j: (0, j)),
            pl.BlockSpec((d, tf), lambda i, j: (0, j + nf)),
            pl.BlockSpec((tf, d), lambda i, j: (j, 0)),
        ],
        out_specs=pl.BlockSpec((tm, d), lambda i, j: (i, 0)),
        scratch_shapes=[pltpu.VMEM((tm, d), BF16), pltpu.VMEM((tm, d), F32)],
        compiler_params=_params(),
    )(x, g_pre, g_post, w_up, w_up, w_down)


def _sg_mask(seg_mode):
    row = lax.broadcasted_iota(jnp.int32, (SG_CHUNK, SG_CHUNK), 0)
    col = lax.broadcasted_iota(jnp.int32, (SG_CHUNK, SG_CHUNK), 1)
    if seg_mode is None:
        return (row // STREAM_CHUNK) >= (col // STREAM_CHUNK)
    return (row // seg_mode) == (col // seg_mode)


def _sg_body(x_ref, g2_ref, g3_ref, wu_ref, wv_ref, lng_ref, lnb_ref, ws_ref, bias_ref, wo_ref,
             *rest, ng, gpt, nchunk, seg_mode, emit_v):
    if emit_v:
        o_ref, vout_ref, h_scr, u_scr, v_scr, y_scr, acc_scr = rest
    else:
        o_ref, h_scr, u_scr, v_scr, y_scr, acc_scr = rest
        vout_ref = None
    j = pl.program_id(1)
    tn = gpt * LANES
    d = ng * tn

    @pl.when(j == 0)
    def _():
        h_scr[...] = _rms(x_ref[...], g2_ref[...]).astype(BF16)
        acc_scr[...] = jnp.zeros_like(acc_scr)

    @pl.when(j < ng)
    def _():
        h = h_scr[...]
        u_scr[j] = _gelu(_dot(h, wu_ref[...]))
        v_scr[j] = _gelu(_dot(h, wv_ref[...]))

    @pl.when(j == ng)
    def _():
        tot = jnp.sum(v_scr[0], axis=-1, keepdims=True)
        for jj in range(1, ng):
            tot += jnp.sum(v_scr[jj], axis=-1, keepdims=True)
        mu = tot * (1.0 / d)
        sq = jnp.sum(jnp.square(v_scr[0] - mu), axis=-1, keepdims=True)
        for jj in range(1, ng):
            sq += jnp.sum(jnp.square(v_scr[jj] - mu), axis=-1, keepdims=True)
        rstd = lax.rsqrt(sq * (1.0 / d) + EPS)
        for jj in range(ng):
            cols = slice(jj * tn, (jj + 1) * tn)
            vn = (v_scr[jj] - mu) * rstd * lng_ref[:, cols] + lnb_ref[:, cols]
            v_scr[jj] = vn
            if emit_v:
                vout_ref[:, cols] = vn

    @pl.when(j >= ng)
    def _():
        jj = j - ng
        mask = _sg_mask(seg_mode)
        for g in range(gpt):
            cols = slice(g * LANES, (g + 1) * LANES)
            w = jnp.where(mask, ws_ref[g], 0.0).astype(BF16)
            for n in range(nchunk):
                rows = slice(n * SG_CHUNK, (n + 1) * SG_CHUNK)
                s = _dot(w, v_scr[jj, rows, cols].astype(BF16)) + bias_ref[:, cols]
                y_scr[rows, cols] = (u_scr[jj, rows, cols] * s).astype(BF16)
        acc_scr[...] += _dot(y_scr[...], wo_ref[...])

    @pl.when(j == 2 * ng - 1)
    def _():
        o_ref[...] = x_ref[...] + _rms(acc_scr[...], g3_ref[...])


def _sg_mixer(x, g2, g3, w_in, ln_g, ln_b, w_s, bias_rows, w_out, *, tm, gpt, seg_mode, emit_v):
    t, d = x.shape
    tn = gpt * LANES
    ng = d // tn
    assert t % tm == 0 and tm % SG_CHUNK == 0 and d % tn == 0
    last = ng - 1
    body = functools.partial(_sg_body, ng=ng, gpt=gpt, nchunk=tm // SG_CHUNK,
                             seg_mode=seg_mode, emit_v=emit_v)
    row_spec = pl.BlockSpec((tm, d), lambda i, j: (i, 0))
    vec_spec = pl.BlockSpec((1, d), lambda i, j: (0, 0))
    out_shape = jax.ShapeDtypeStruct((t, d), F32)
    return pl.pallas_call(
        body,
        out_shape=(out_shape, out_shape) if emit_v else out_shape,
        grid=(t // tm, 2 * ng),
        in_specs=[
            row_spec, vec_spec, vec_spec,
            pl.BlockSpec((d, tn), lambda i, j: (0, jnp.minimum(j, last))),
            pl.BlockSpec((d, tn), lambda i, j: (0, ng + jnp.minimum(j, last))),
            vec_spec, vec_spec,
            pl.BlockSpec((gpt, SG_CHUNK, SG_CHUNK), lambda i, j: (jnp.maximum(j - ng, 0), 0, 0)),
            pl.BlockSpec((SG_CHUNK, tn), lambda i, j: (0, jnp.maximum(j - ng, 0))),
            pl.BlockSpec((tn, d), lambda i, j: (jnp.maximum(j - ng, 0), 0)),
        ],
        out_specs=(row_spec, row_spec) if emit_v else row_spec,
        scratch_shapes=[
            pltpu.VMEM((tm, d), BF16),
            pltpu.VMEM((ng, tm, tn), F32),
            pltpu.VMEM((ng, tm, tn), F32),
            pltpu.VMEM((tm, tn), BF16),
            pltpu.VMEM((tm, d), F32),
        ],
        compiler_params=_params(),
    )(x, g2, g3, w_in, w_in, ln_g, ln_b, w_s, bias_rows, w_out)


def _scan_tables(chunk, seg):
    levels = []
    m = seg // 2
    while m >= 1:
        levels.append(m)
        m //= 2
    nb = 2 + len(levels)
    t = np.arange(chunk)[:, None]
    j = np.arange(chunk)[None, :]
    same_seg = (t // seg) == (j // seg)
    sums = np.zeros((nb, chunk, chunk), np.float32)
    sums[0] = same_seg & (j <= t)
    sums[1] = same_seg & (j > t)
    level_id = np.where(t == j, 0, -1).astype(np.int32)
    for li, m in enumerate(levels):
        g = 2 * m
        ref = (t // g) * g + m - 1
        odd = (t % g) >= m
        sums[2 + li] = np.where(odd, (j > ref) & (j <= t), (j > t) & (j <= ref))
        pair = ((t // g) == (j // g)) & odd & ((j % g) < m)
        level_id = np.where(pair, li + 1, level_id).astype(np.int32)
    sums = sums.reshape(nb * chunk, chunk)
    return np.concatenate([sums, sums, sums], axis=1), level_id, nb


def _split3(x):
    hi = x.astype(BF16)
    r1 = x - hi.astype(F32)
    mid = r1.astype(BF16)
    lo = (r1 - mid.astype(F32)).astype(BF16)
    return jnp.concatenate([hi, mid, lo], axis=0)


def _hgrn_body(x_ref, g2_ref, g3_ref, wq_ref, wf_ref, wi_ref, wg_ref, hl_ref, ng_ref, wo_ref,
               sums_ref, lid_ref, s0_ref, o_ref, sout_ref,
               h_scr, q_scr, lf_scr, k_scr, v_scr, gate_scr, b_scr, e_scr, st_scr, on_scr, acc_scr,
               *, layer, chunk, seg, nh, nb, nchunk):
    i = pl.program_id(0)
    j = pl.program_id(1)
    nseg = chunk // seg
    nlev = nb - 2

    @pl.when(j == 0)
    def _():
        h_scr[...] = _rms(x_ref[...], g2_ref[...]).astype(BF16)
        acc_scr[...] = jnp.zeros_like(acc_scr)

    @pl.when(i == 0)
    def _():
        for s in range(nseg):
            for hh in range(nh):
                st_scr[s, j * nh + hh] = s0_ref[s, j * nh + hh].T

    hl = hl_ref[...]
    e = jnp.exp(hl - jnp.max(hl, axis=0, keepdims=True))
    p = e / jnp.sum(e, axis=0, keepdims=True)
    lb = jnp.zeros_like(p[0:1])
    for r in range(1, layer + 1):
        lb = lb + p[r:r + 1]

    h = h_scr[...]
    q_scr[...] = _silu(_dot(h, wq_ref[...]))
    f = lb + (1.0 - lb) * jax.nn.sigmoid(_dot(h, wf_ref[...]))
    lf_scr[...] = jnp.log(f)
    k_scr[...] = 1.0 - f
    v_scr[...] = _dot(h, wi_ref[...]).astype(BF16)
    gate_scr[...] = _silu(_dot(h, wg_ref[...]))

    lid = lid_ref[...]

    def carry_state(rows, cols, head, o_intra, q_pre, k_end, vh, e_end):
        inter = []
        for s in range(nseg):
            srows = slice(s * seg, (s + 1) * seg)
            st = st_scr[s, head]
            inter.append(_dot_nt(q_pre[srows], st.astype(BF16)))
            st_scr[s, head] = st * e_end[s] + _dot_tn(vh[srows], k_end[srows])
        o = o_intra + (inter[0] if nseg == 1 else jnp.concatenate(inter, axis=0))
        on_scr[rows, cols] = (_rms(o, ng_ref[...]) * gate_scr[rows, cols]).astype(BF16)

    def split_scan():
        for c in range(nchunk):
            rows = slice(c * chunk, (c + 1) * chunk)
            b = b_scr[rows, :]
            half = 0.5 * b[chunk - 1:chunk, :]
            x = b - half
            e_q = jnp.exp(x)
            e_k = jnp.exp(-x)
            e_half = jnp.exp(half)
            for hh in range(nh):
                cols = slice(hh * HEAD_DIM, (hh + 1) * HEAD_DIM)
                qt = q_scr[rows, cols] * e_q[:, cols]
                kt = k_scr[rows, cols] * e_k[:, cols]
                vh = v_scr[rows, cols]
                eh = e_half[:, cols]
                sc = jnp.where(lid >= 0, _dot_nt(qt.astype(BF16), kt.astype(BF16)), 0.0)
                carry_state(rows, cols, j * nh + hh, _dot(sc.astype(BF16), vh),
                            (qt * eh).astype(BF16), (kt * eh).astype(BF16), vh, [eh * eh])

    def level_scan():
        for c in range(nchunk):
            rows = slice(c * chunk, (c + 1) * chunk)
            e_scr[...] = jnp.exp(_dot(sums_ref[...], _split3(lf_scr[rows, :])))
            for hh in range(nh):
                cols = slice(hh * HEAD_DIM, (hh + 1) * HEAD_DIM)
                qh = q_scr[rows, cols]
                kh = k_scr[rows, cols]
                vh = v_scr[rows, cols]
                e_pre = e_scr[0:chunk, cols]
                k_end = (kh * e_scr[chunk:2 * chunk, cols]).astype(BF16)
                sc = jnp.where(lid == 0, _dot_nt(qh.astype(BF16), kh.astype(BF16)), 0.0)
                for l in range(nlev):
                    el = e_scr[(2 + l) * chunk:(3 + l) * chunk, cols]
                    sc_l = _dot_nt((qh * el).astype(BF16), (kh * el).astype(BF16))
                    sc = jnp.where(lid == l + 1, sc_l, sc)
                e_end = [e_pre[(s + 1) * seg - 1:(s + 1) * seg, :] for s in range(nseg)]
                carry_state(rows, cols, j * nh + hh, _dot(sc.astype(BF16), vh),
                            (qh * e_pre).astype(BF16), k_end, vh, e_end)

    if nseg == 1:
        worst = jnp.float32(0.0)
        for c in range(nchunk):
            rows = slice(c * chunk, (c + 1) * chunk)
            b = _dot(sums_ref[0:chunk, :], _split3(lf_scr[rows, :]))
            b_scr[rows, :] = b
            worst = jnp.maximum(worst, jnp.max(-b[chunk - 1:chunk, :]))
        pl.when(worst <= MAX_SPLIT_DECAY)(split_scan)
        pl.when(jnp.logical_not(worst <= MAX_SPLIT_DECAY))(level_scan)
    else:
        level_scan()

    @pl.when(i == pl.num_programs(0) - 1)
    def _():
        for s in range(nseg):
            for hh in range(nh):
                sout_ref[s, j * nh + hh] = st_scr[s, j * nh + hh].T

    acc_scr[...] += _dot(on_scr[...], wo_ref[...])

    @pl.when(j == pl.num_programs(1) - 1)
    def _():
        o_ref[...] = x_ref[...] + _rms(acc_scr[...], g3_ref[...])


def _hgrn_mixer(x, g2, g3, w_in, hg_lower, norm_g, w_out, s0, *, layer, tm, chunk, seg, nh):
    t, d = x.shape
    heads = d // HEAD_DIM
    nseg = chunk // seg
    tn = nh * HEAD_DIM
    nhg = heads // nh
    assert t % tm == 0 and tm % chunk == 0 and heads % nh == 0
    assert s0.shape == (nseg, heads, HEAD_DIM, HEAD_DIM)
    assert nseg == 1 or t == chunk
    sums, level_id, nb = _scan_tables(chunk, seg)
    body = functools.partial(_hgrn_body, layer=layer, chunk=chunk, seg=seg, nh=nh, nb=nb,
                             nchunk=tm // chunk)
    row_spec = pl.BlockSpec((tm, d), lambda i, j: (i, 0))
    vec_spec = pl.BlockSpec((1, d), lambda i, j: (0, 0))
    nsec = d // tn
    state_spec = pl.BlockSpec(s0.shape, lambda i, j: (0, 0, 0, 0))
    return pl.pallas_call(
        body,
        out_shape=(jax.ShapeDtypeStruct((t, d), F32), jax.ShapeDtypeStruct(s0.shape, F32)),
        grid=(t // tm, nhg),
        in_specs=[
            row_spec, vec_spec, vec_spec,
            pl.BlockSpec((d, tn), lambda i, j: (0, j)),
            pl.BlockSpec((d, tn), lambda i, j: (0, nsec + j)),
            pl.BlockSpec((d, tn), lambda i, j: (0, 2 * nsec + j)),
            pl.BlockSpec((d, tn), lambda i, j: (0, 3 * nsec + j)),
            pl.BlockSpec((hg_lower.shape[0], tn), lambda i, j: (0, j)),
            pl.BlockSpec((1, HEAD_DIM), lambda i, j: (0, 0)),
            pl.BlockSpec((tn, d), lambda i, j: (j, 0)),
            pl.BlockSpec(sums.shape, lambda i, j: (0, 0)),
            pl.BlockSpec(level_id.shape, lambda i, j: (0, 0)),
            state_spec,
        ],
        out_specs=(row_spec, state_spec),
        scratch_shapes=[
            pltpu.VMEM((tm, d), BF16),
            pltpu.VMEM((tm, tn), F32),
            pltpu.VMEM((tm, tn), F32),
            pltpu.VMEM((tm, tn), F32),
            pltpu.VMEM((tm, tn), BF16),
            pltpu.VMEM((tm, tn), F32),
            pltpu.VMEM((tm, tn), F32),
            pltpu.VMEM((nb * chunk, tn), F32),
            pltpu.VMEM((nseg, heads, HEAD_DIM, HEAD_DIM), F32),
            pltpu.VMEM((tm, tn), BF16),
            pltpu.VMEM((tm, d), F32),
        ],
        compiler_params=_params(),
    )(x, g2, g3, w_in, w_in, w_in, w_in, hg_lower, norm_g, w_out,
      jnp.asarray(sums, BF16), jnp.asarray(level_id), s0)


def _trunk(x, hg_init, p, *, tm, tf, sample):
    depth = p["norm_g"].shape[0]
    n_hg = p["hg_w_in"].shape[0]
    n_mix = 2
    hg_states, sg_vs = [], []
    for i in range(depth):
        g = p["norm_g"][i]
        row = lambda r: g[r:r + 1]
        jm = i // n_mix
        x = _ffn(x, row(0), row(1), p["ffn_w_up"][i, 0], p["ffn_w_down"][i, 0], tm=tm, tf=tf)
        if i % n_mix == 0:
            w_s, b_s = p["sg_w_s"][jm], p["sg_b_s"][jm]
            if sample is not None:
                n_streams, seq = sample
                w_s = jnp.tile(w_s[:, :seq, :seq], (1, n_streams, n_streams))
                b_s = jnp.tile(b_s[:, :seq], (1, n_streams))
            bias_rows = jnp.repeat(b_s.T, LANES, axis=1)
            res = _sg_mixer(x, row(2), row(3), p["sg_w_in"][jm], p["sg_ln_g"][jm:jm + 1],
                            p["sg_ln_b"][jm:jm + 1], w_s, bias_rows, p["sg_w_out"][jm],
                            tm=tm, gpt=4, seg_mode=None if sample is None else sample[1],
                            emit_v=sample is not None)
            if sample is not None:
                x, v_rows = res
                sg_vs.append(v_rows)
            else:
                x = res
        else:
            x, st = _hgrn_mixer(x, row(2), row(3), p["hg_w_in"][jm], p["hg_lower"],
                                p["hg_norm_g"][jm:jm + 1], p["hg_w_out"][jm], hg_init[jm],
                                layer=i, tm=tm, chunk=SG_CHUNK if sample is None else x.shape[0],
                                seg=SG_CHUNK if sample is None else sample[1], nh=2)
            hg_states.append(st)
        x = _ffn(x, row(4), row(5), p["ffn_w_up"][i, 1], p["ffn_w_down"][i, 1], tm=tm, tf=tf)
    return x, hg_states, sg_vs


def kernel(x_prompt, x_sample, state_hgrn, norm_g, ffn_w_up, ffn_w_down, sg_w_in, sg_ln_g, sg_ln_b,
           sg_w_s, sg_b_s, sg_w_out, hg_w_in, hg_lower, hg_norm_g, hg_w_out):
    batch, seq, d = x_prompt.shape
    dec_batch, dec_seq, _ = x_sample.shape
    n_hg = hg_w_in.shape[0]
    heads = d // HEAD_DIM
    p = dict(norm_g=norm_g, ffn_w_up=ffn_w_up.astype(BF16), ffn_w_down=ffn_w_down.astype(BF16),
             sg_w_in=sg_w_in.astype(BF16), sg_ln_g=sg_ln_g, sg_ln_b=sg_ln_b, sg_w_s=sg_w_s,
             sg_b_s=sg_b_s, sg_w_out=sg_w_out.astype(BF16), hg_w_in=hg_w_in.astype(BF16),
             hg_lower=hg_lower, hg_norm_g=hg_norm_g, hg_w_out=hg_w_out.astype(BF16))

    ys, sts = [], []
    for b in range(batch):
        init = [jnp.zeros((1, heads, HEAD_DIM, HEAD_DIM), F32) for _ in range(n_hg)]
        y, hg_p, _ = _trunk(x_prompt[b], init, p, tm=512, tf=512, sample=None)
        ys.append(y)
        sts.append(jnp.stack([s[0] for s in hg_p], axis=0))
    y_prompt = jnp.stack(ys, axis=0)
    state_hgrn_prompt = jnp.stack(sts, axis=1)

    init = [state_hgrn[jm].astype(F32) for jm in range(n_hg)]
    y, hg_s, sg_v = _trunk(x_sample.reshape(dec_batch * dec_seq, d), init, p,
                           tm=dec_batch * dec_seq, tf=512, sample=(dec_batch, dec_seq))
    y_sample = y.reshape(dec_batch, dec_seq, d)
    state_hgrn_sample = jnp.stack(hg_s, axis=0)
    state_sg_v_sample = jnp.stack([v.reshape(dec_batch, dec_seq, d) for v in sg_v], axis=0)
    return (y_prompt, y_sample, state_hgrn_prompt, state_hgrn_sample, state_sg_v_sample)
```

```python
import functools

import numpy as np
import jax
import jax.numpy as jnp
from jax import lax
from jax.experimental import pallas as pl
from jax.experimental.pallas import tpu as pltpu

EPS = 1e-6
F32 = jnp.float32
BF16 = jnp.bfloat16

LANES = 128
ROW_CHUNK = 16
HEAD_DIM = 128
SG_CHUNK = 128
STREAM_CHUNK = 64
V7X_VMEM_BYTES = 64 * 1024 * 1024
VMEM_LIMIT_BYTES = V7X_VMEM_BYTES - 8 * 1024 * 1024
MAX_SPLIT_DECAY = 140.0


def _dot(a, b):
    return jnp.dot(a, b, preferred_element_type=F32)


def _dot_nt(a, b):
    return lax.dot_general(a, b, (((1,), (1,)), ((), ())), preferred_element_type=F32)


def _dot_tn(a, b):
    return lax.dot_general(a, b, (((0,), (0,)), ((), ())), preferred_element_type=F32)


def _rms(x, g):
    return x * lax.rsqrt(jnp.mean(x * x, axis=-1, keepdims=True) + EPS) * g


def _silu(x):
    return x * jax.nn.sigmoid(x)


def _gelu(x):
    return 0.5 * x * (1.0 + lax.erf(x * np.float32(np.sqrt(0.5))))


def _for_tiles(n_rows, n_cols, fn):
    def step(c, carry):
        rows = pl.ds(pl.multiple_of(c * ROW_CHUNK, ROW_CHUNK), ROW_CHUNK)
        for ct in range(n_cols // LANES):
            fn(rows, slice(ct * LANES, (ct + 1) * LANES))
        return carry
    lax.fori_loop(0, n_rows // ROW_CHUNK, step, 0, unroll=max(2, 32 * LANES // n_cols))


def _row_rsqrt(x):
    r = lax.rsqrt(jnp.mean(x * x, axis=-1, keepdims=True) + EPS)
    return jnp.broadcast_to(r, (x.shape[0], LANES))


def _pre_norm(x_ref, g_ref, h_scr, acc_scr, rs_scr):
    rs_scr[...] = _row_rsqrt(x_ref[...])

    def tile(rows, cols):
        h_scr[rows, cols] = (x_ref[rows, cols] * rs_scr[rows, :] * g_ref[:, cols]).astype(BF16)
        acc_scr[rows, cols] = jnp.zeros((ROW_CHUNK, LANES), F32)
    _for_tiles(x_ref.shape[0], x_ref.shape[1], tile)


def _post_norm(x_ref, g_ref, acc_scr, rs_scr, o_ref, scale):
    rs_scr[...] = _row_rsqrt(acc_scr[...])

    def tile(rows, cols):
        o_ref[rows, cols] = x_ref[rows, cols] + (acc_scr[rows, cols] * rs_scr[rows, :]) * (scale * g_ref[:, cols])
    _for_tiles(x_ref.shape[0], x_ref.shape[1], tile)


def _params():
    return pltpu.CompilerParams(dimension_semantics=("arbitrary", "arbitrary"),
                                vmem_limit_bytes=VMEM_LIMIT_BYTES)


def _ffn_body(xn_ref, xp_ref, gpre_ref, gpost_ref, wa_ref, wb_ref, wd_ref, o_ref, h0, h1, acc0, acc1,
              *, nt, nslice):
    r = pl.program_id(0)
    j = pl.program_id(1)
    rs = xn_ref.shape[0]
    base = jnp.minimum(j, nslice - 1) * rs

    def chunk_rows(c):
        return pl.ds(pl.multiple_of(base + c * ROW_CHUNK, ROW_CHUNK), ROW_CHUNK)

    def stages(h_pre, h_main, acc_main, acc_post):
        def pre():
            for c in range(rs // ROW_CHUNK):
                xc = xn_ref[c * ROW_CHUNK:(c + 1) * ROW_CHUNK, :]
                h_pre[chunk_rows(c), :] = _rms(xc, gpre_ref[...]).astype(BF16)

        def main():
            h = h_main[...]
            a = _dot(h, wa_ref[...])
            b = _dot(h, wb_ref[...])
            contrib = _dot((_silu(a) * b).astype(BF16), wd_ref[...])
            acc_main[...] = jnp.where(j == 0, contrib, acc_main[...] + contrib)

        def post():
            for c in range(rs // ROW_CHUNK):
                rows = slice(c * ROW_CHUNK, (c + 1) * ROW_CHUNK)
                o_ref[rows, :] = xp_ref[rows, :] + 0.5 * _rms(acc_post[chunk_rows(c), :], gpost_ref[...])

        steady = jnp.logical_and(r >= 2, r < nt)

        @pl.when(steady)
        def _():
            pre()
            main()
            post()

        @pl.when(jnp.logical_not(steady))
        def _():
            @pl.when(jnp.logical_and(r == 0, j == 0))
            def _():
                acc0[...] = jnp.zeros_like(acc0)
                acc1[...] = jnp.zeros_like(acc1)
            pl.when(r < nt)(pre)
            pl.when(jnp.logical_and(r >= 1, r <= nt))(main)
            pl.when(r >= 2)(post)

    pl.when(r % 2 == 0)(lambda: stages(h0, h1, acc1, acc0))
    pl.when(r % 2 == 1)(lambda: stages(h1, h0, acc0, acc1))


def _ffn(x, g_pre, g_post, w_up, w_down, li, k, *, tm, tf):
    t, d = x.shape
    f = w_down.shape[2]
    nf = f // tf
    nt = t // tm
    nslice = min(nf, tm // ROW_CHUNK)
    while tm % (nslice * ROW_CHUNK):
        nslice -= 1
    rs = tm // nslice
    assert t % tm == 0 and f % tf == 0

    def w_step(r, j):
        return jnp.where(r == 0, 0, jnp.where(r > nt, nf - 1, j))

    def next_slice(r, j):
        return jnp.where(r < nt, r * nslice + jnp.minimum(j, nslice - 1), nt * nslice - 1)

    def prev_slice(r, j):
        return jnp.where(r >= 2, (r - 2) * nslice + jnp.minimum(j, nslice - 1), 0)

    return pl.pallas_call(
        functools.partial(_ffn_body, nt=nt, nslice=nslice),
        out_shape=jax.ShapeDtypeStruct((t, d), F32),
        grid=(nt + 2, nf),
        in_specs=[
            pl.BlockSpec((rs, d), lambda r, j: (next_slice(r, j), 0)),
            pl.BlockSpec((rs, d), lambda r, j: (prev_slice(r, j), 0)),
            pl.BlockSpec((1, d), lambda r, j: (0, 0)),
            pl.BlockSpec((1, d), lambda r, j: (0, 0)),
            pl.BlockSpec((None, None, d, tf), lambda r, j: (li, k, 0, w_step(r, j))),
            pl.BlockSpec((None, None, d, tf), lambda r, j: (li, k, 0, w_step(r, j) + nf)),
            pl.BlockSpec((None, None, tf, d), lambda r, j: (li, k, w_step(r, j), 0)),
        ],
        out_specs=pl.BlockSpec((rs, d), lambda r, j: (prev_slice(r, j), 0)),
        scratch_shapes=[pltpu.VMEM((tm, d), BF16), pltpu.VMEM((tm, d), BF16),
                        pltpu.VMEM((tm, d), F32), pltpu.VMEM((tm, d), F32)],
        compiler_params=_params(),
    )(x, x, g_pre, g_post, w_up, w_up, w_down)


def _sg_mask(seg_mode):
    row = lax.broadcasted_iota(jnp.int32, (SG_CHUNK, SG_CHUNK), 0)
    col = lax.broadcasted_iota(jnp.int32, (SG_CHUNK, SG_CHUNK), 1)
    if seg_mode is None:
        return (row // STREAM_CHUNK) >= (col // STREAM_CHUNK)
    return (row // seg_mode) == (col // seg_mode)


def _sg_body(x_ref, g2_ref, g3_ref, wu_ref, wv_ref, lng_ref, lnb_ref, ws_ref, bias_ref, wo_ref,
             *rest, ng, gpt, nchunk, seg_mode, emit_v):
    if emit_v:
        o_ref, vout_ref, h_scr, u_scr, v_scr, y_scr, acc_scr, rs_scr, mu_scr = rest
    else:
        o_ref, h_scr, u_scr, v_scr, y_scr, acc_scr, rs_scr, mu_scr = rest
        vout_ref = None
    j = pl.program_id(1)
    tn = gpt * LANES
    d = ng * tn
    tm = x_ref.shape[0]

    @pl.when(j == 0)
    def _():
        _pre_norm(x_ref, g2_ref, h_scr, acc_scr, rs_scr)

    @pl.when(j < ng)
    def _():
        h = h_scr[...]
        u_scr[j] = _gelu(_dot(h, wu_ref[...]))
        v_scr[j] = _gelu(_dot(h, wv_ref[...]))

    @pl.when(j == ng)
    def _():
        mu = sum(jnp.sum(v_scr[jj], axis=-1, keepdims=True) for jj in range(ng)) * (1.0 / d)
        var = sum(jnp.sum(jnp.square(v_scr[jj] - mu), axis=-1, keepdims=True)
                  for jj in range(ng)) * (1.0 / d)
        mu_scr[...] = jnp.broadcast_to(mu, (tm, LANES))
        rs_scr[...] = jnp.broadcast_to(lax.rsqrt(var + EPS), (tm, LANES))
        for jj in range(ng):
            def tile(rows, cols, jj=jj):
                gcols = slice(jj * tn + cols.start, jj * tn + cols.stop)
                vn = ((v_scr[jj, rows, cols] - mu_scr[rows, :]) * rs_scr[rows, :] * lng_ref[:, gcols]
                      + lnb_ref[:, gcols])
                v_scr[jj, rows, cols] = vn
                if emit_v:
                    vout_ref[rows, gcols] = vn
            _for_tiles(tm, tn, tile)

    @pl.when(j >= ng)
    def _():
        jj = j - ng
        mask = _sg_mask(seg_mode)
        for g in range(gpt):
            cols = slice(g * LANES, (g + 1) * LANES)
            w = jnp.where(mask, ws_ref[g], 0.0).astype(BF16)
            for n in range(nchunk):
                rows = slice(n * SG_CHUNK, (n + 1) * SG_CHUNK)
                s = _dot(w, v_scr[jj, rows, cols].astype(BF16)) + bias_ref[:, cols]
                y_scr[rows, cols] = (u_scr[jj, rows, cols] * s).astype(BF16)
        acc_scr[...] += _dot(y_scr[...], wo_ref[...])

    @pl.when(j == 2 * ng - 1)
    def _():
        _post_norm(x_ref, g3_ref, acc_scr, rs_scr, o_ref, 1.0)


def _sg_mixer(x, g2, g3, w_in, ln_g, ln_b, w_s, bias_rows, w_out, jm, *, tm, gpt, seg_mode, emit_v):
    t, d = x.shape
    tn = gpt * LANES
    ng = d // tn
    assert t % tm == 0 and tm % SG_CHUNK == 0 and d % tn == 0 and tm % ROW_CHUNK == 0
    last = ng - 1
    body = functools.partial(_sg_body, ng=ng, gpt=gpt, nchunk=tm // SG_CHUNK,
                             seg_mode=seg_mode, emit_v=emit_v)
    row_spec = pl.BlockSpec((tm, d), lambda i, j: (i, 0))
    vec_spec = pl.BlockSpec((1, d), lambda i, j: (0, 0))
    out_shape = jax.ShapeDtypeStruct((t, d), F32)
    return pl.pallas_call(
        body,
        out_shape=(out_shape, out_shape) if emit_v else out_shape,
        grid=(t // tm, 2 * ng),
        in_specs=[
            row_spec, vec_spec, vec_spec,
            pl.BlockSpec((None, d, tn), lambda i, j: (jm, 0, jnp.minimum(j, last))),
            pl.BlockSpec((None, d, tn), lambda i, j: (jm, 0, ng + jnp.minimum(j, last))),
            vec_spec, vec_spec,
            pl.BlockSpec((gpt, SG_CHUNK, SG_CHUNK), lambda i, j: (jnp.maximum(j - ng, 0), 0, 0)),
            pl.BlockSpec((SG_CHUNK, tn), lambda i, j: (0, jnp.maximum(j - ng, 0))),
            pl.BlockSpec((None, tn, d), lambda i, j: (jm, jnp.maximum(j - ng, 0), 0)),
        ],
        out_specs=(row_spec, row_spec) if emit_v else row_spec,
        scratch_shapes=[
            pltpu.VMEM((tm, d), BF16),
            pltpu.VMEM((ng, tm, tn), F32),
            pltpu.VMEM((ng, tm, tn), F32),
            pltpu.VMEM((tm, tn), BF16),
            pltpu.VMEM((tm, d), F32),
            pltpu.VMEM((tm, LANES), F32),
            pltpu.VMEM((tm, LANES), F32),
        ],
        compiler_params=_params(),
    )(x, g2, g3, w_in, w_in, ln_g, ln_b, w_s, bias_rows, w_out)


def _scan_tables(chunk, seg):
    levels = []
    m = seg // 2
    while m >= 1:
        levels.append(m)
        m //= 2
    nb = 2 + len(levels)
    t = np.arange(chunk)[:, None]
    j = np.arange(chunk)[None, :]
    same_seg = (t // seg) == (j // seg)
    sums = np.zeros((nb, chunk, chunk), np.float32)
    sums[0] = same_seg & (j <= t)
    sums[1] = same_seg & (j > t)
    level_id = np.where(t == j, 0, -1).astype(np.int32)
    for li, m in enumerate(levels):
        g = 2 * m
        ref = (t // g) * g + m - 1
        odd = (t % g) >= m
        sums[2 + li] = np.where(odd, (j > ref) & (j <= t), (j > t) & (j <= ref))
        pair = ((t // g) == (j // g)) & odd & ((j % g) < m)
        level_id = np.where(pair, li + 1, level_id).astype(np.int32)
    sums = sums.reshape(nb * chunk, chunk)
    return np.concatenate([sums, sums, sums], axis=1), level_id, nb


def _split3(x):
    hi = x.astype(BF16)
    r1 = x - hi.astype(F32)
    mid = r1.astype(BF16)
    lo = (r1 - mid.astype(F32)).astype(BF16)
    return jnp.concatenate([hi, mid, lo], axis=0)


def _hgrn_body(x_ref, g2_ref, g3_ref, wq_ref, wf_ref, wi_ref, wg_ref, hl_ref, ng_ref, wo_ref,
               sums_ref, lid_ref, s0_ref, o_ref, sout_ref,
               h_scr, q_scr, lf_scr, k_scr, v_scr, gate_scr, b_scr, e_scr, st_scr, on_scr, acc_scr, rs_scr,
               *, layer, chunk, seg, nh, nb, nchunk):
    i = pl.program_id(0)
    j = pl.program_id(1)
    nseg = chunk // seg
    nlev = nb - 2

    @pl.when(j == 0)
    def _():
        _pre_norm(x_ref, g2_ref, h_scr, acc_scr, rs_scr)

    @pl.when(i == 0)
    def _():
        for s in range(nseg):
            for hh in range(nh):
                st_scr[s, j * nh + hh] = s0_ref[s, j * nh + hh].T

    hl = hl_ref[...]
    e = jnp.exp(hl - jnp.max(hl, axis=0, keepdims=True))
    p = e / jnp.sum(e, axis=0, keepdims=True)
    lb = jnp.zeros_like(p[0:1])
    for r in range(1, layer + 1):
        lb = lb + p[r:r + 1]

    h = h_scr[...]
    q_scr[...] = _silu(_dot(h, wq_ref[...]))
    f = lb + (1.0 - lb) * jax.nn.sigmoid(_dot(h, wf_ref[...]))
    lf_scr[...] = jnp.log(f)
    k_scr[...] = 1.0 - f
    v_scr[...] = _dot(h, wi_ref[...]).astype(BF16)
    gate_scr[...] = _silu(_dot(h, wg_ref[...]))

    lid = lid_ref[...]

    def carry_state(rows, cols, head, o_intra, q_pre, k_end, vh, e_end):
        inter = []
        for s in range(nseg):
            srows = slice(s * seg, (s + 1) * seg)
            st = st_scr[s, head]
            inter.append(_dot_nt(q_pre[srows], st.astype(BF16)))
            st_scr[s, head] = st * e_end[s] + _dot_tn(vh[srows], k_end[srows])
        o = o_intra + (inter[0] if nseg == 1 else jnp.concatenate(inter, axis=0))
        on_scr[rows, cols] = (_rms(o, ng_ref[...]) * gate_scr[rows, cols]).astype(BF16)

    def split_scan():
        for c in range(nchunk):
            rows = slice(c * chunk, (c + 1) * chunk)
            b = b_scr[rows, :]
            half = 0.5 * b[chunk - 1:chunk, :]
            x = b - half
            e_q = jnp.exp(x)
            e_k = jnp.exp(-x)
            e_half = jnp.exp(half)
            for hh in range(nh):
                cols = slice(hh * HEAD_DIM, (hh + 1) * HEAD_DIM)
                qt = q_scr[rows, cols] * e_q[:, cols]
                kt = k_scr[rows, cols] * e_k[:, cols]
                vh = v_scr[rows, cols]
                eh = e_half[:, cols]
                sc = jnp.where(lid >= 0, _dot_nt(qt.astype(BF16), kt.astype(BF16)), 0.0)
                carry_state(rows, cols, j * nh + hh, _dot(sc.astype(BF16), vh),
                            (qt * eh).astype(BF16), (kt * eh).astype(BF16), vh, [eh * eh])

    def level_scan():
        for c in range(nchunk):
            rows = slice(c * chunk, (c + 1) * chunk)
            e_scr[...] = jnp.exp(_dot(sums_ref[...], _split3(lf_scr[rows, :])))
            for hh in range(nh):
                cols = slice(hh * HEAD_DIM, (hh + 1) * HEAD_DIM)
                qh = q_scr[rows, cols]
                kh = k_scr[rows, cols]
                vh = v_scr[rows, cols]
                e_pre = e_scr[0:chunk, cols]
                k_end = (kh * e_scr[chunk:2 * chunk, cols]).astype(BF16)
                sc = jnp.where(lid == 0, _dot_nt(qh.astype(BF16), kh.astype(BF16)), 0.0)
                for l in range(nlev):
                    el = e_scr[(2 + l) * chunk:(3 + l) * chunk, cols]
                    sc_l = _dot_nt((qh * el).astype(BF16), (kh * el).astype(BF16))
                    sc = jnp.where(lid == l + 1, sc_l, sc)
                e_end = [e_pre[(s + 1) * seg - 1:(s + 1) * seg, :] for s in range(nseg)]
                carry_state(rows, cols, j * nh + hh, _dot(sc.astype(BF16), vh),
                            (qh * e_pre).astype(BF16), k_end, vh, e_end)

    if nseg == 1:
        worst = jnp.float32(0.0)
        for c in range(nchunk):
            rows = slice(c * chunk, (c + 1) * chunk)
            b = _dot(sums_ref[0:chunk, :], _split3(lf_scr[rows, :]))
            b_scr[rows, :] = b
            worst = jnp.maximum(worst, jnp.max(-b[chunk - 1:chunk, :]))
        pl.when(worst <= MAX_SPLIT_DECAY)(split_scan)
        pl.when(jnp.logical_not(worst <= MAX_SPLIT_DECAY))(level_scan)
    else:
        level_scan()

    @pl.when(i == pl.num_programs(0) - 1)
    def _():
        for s in range(nseg):
            for hh in range(nh):
                sout_ref[s, j * nh + hh] = st_scr[s, j * nh + hh].T

    acc_scr[...] += _dot(on_scr[...], wo_ref[...])

    @pl.when(j == pl.num_programs(1) - 1)
    def _():
        _post_norm(x_ref, g3_ref, acc_scr, rs_scr, o_ref, 1.0)


def _hgrn_mixer(x, g2, g3, w_in, hg_lower, norm_g, w_out, s0, jm, *, layer, tm, chunk, seg, nh):
    t, d = x.shape
    heads = d // HEAD_DIM
    nseg = chunk // seg
    tn = nh * HEAD_DIM
    nhg = heads // nh
    assert t % tm == 0 and tm % chunk == 0 and heads % nh == 0 and tm % ROW_CHUNK == 0
    assert s0.shape == (nseg, heads, HEAD_DIM, HEAD_DIM)
    assert nseg == 1 or t == chunk
    sums, level_id, nb = _scan_tables(chunk, seg)
    body = functools.partial(_hgrn_body, layer=layer, chunk=chunk, seg=seg, nh=nh, nb=nb,
                             nchunk=tm // chunk)
    row_spec = pl.BlockSpec((tm, d), lambda i, j: (i, 0))
    vec_spec = pl.BlockSpec((1, d), lambda i, j: (0, 0))
    nsec = d // tn
    state_spec = pl.BlockSpec(s0.shape, lambda i, j: (0, 0, 0, 0))
    return pl.pallas_call(
        body,
        out_shape=(jax.ShapeDtypeStruct((t, d), F32), jax.ShapeDtypeStruct(s0.shape, F32)),
        grid=(t // tm, nhg),
        in_specs=[
            row_spec, vec_spec, vec_spec,
            pl.BlockSpec((None, d, tn), lambda i, j: (jm, 0, j)),
            pl.BlockSpec((None, d, tn), lambda i, j: (jm, 0, nsec + j)),
            pl.BlockSpec((None, d, tn), lambda i, j: (jm, 0, 2 * nsec + j)),
            pl.BlockSpec((None, d, tn), lambda i, j: (jm, 0, 3 * nsec + j)),
            pl.BlockSpec((hg_lower.shape[0], tn), lambda i, j: (0, j)),
            pl.BlockSpec((1, HEAD_DIM), lambda i, j: (0, 0)),
            pl.BlockSpec((None, tn, d), lambda i, j: (jm, j, 0)),
            pl.BlockSpec(sums.shape, lambda i, j: (0, 0)),
            pl.BlockSpec(level_id.shape, lambda i, j: (0, 0)),
            state_spec,
        ],
        out_specs=(row_spec, state_spec),
        scratch_shapes=[
            pltpu.VMEM((tm, d), BF16),
            pltpu.VMEM((tm, tn), F32),
            pltpu.VMEM((tm, tn), F32),
            pltpu.VMEM((tm, tn), F32),
            pltpu.VMEM((tm, tn), BF16),
            pltpu.VMEM((tm, tn), F32),
            pltpu.VMEM((tm, tn), F32),
            pltpu.VMEM((nb * chunk, tn), F32),
            pltpu.VMEM((nseg, heads, HEAD_DIM, HEAD_DIM), F32),
            pltpu.VMEM((tm, tn), BF16),
            pltpu.VMEM((tm, d), F32),
            pltpu.VMEM((tm, LANES), F32),
        ],
        compiler_params=_params(),
    )(x, g2, g3, w_in, w_in, w_in, w_in, hg_lower, norm_g, w_out,
      jnp.asarray(sums, BF16), jnp.asarray(level_id), s0)


def _trunk(x, hg_init, p, *, tm, tf, sample):
    depth = p["norm_g"].shape[0]
    n_mix = 2
    hg_states, sg_vs = [], []
    for i in range(depth):
        g = p["norm_g"][i]
        row = lambda r: g[r:r + 1]
        jm = i // n_mix
        x = _ffn(x, row(0), row(1), p["ffn_w_up"], p["ffn_w_down"], i, 0, tm=tm, tf=tf)
        if i % n_mix == 0:
            w_s, b_s = p["sg_w_s"][jm], p["sg_b_s"][jm]
            if sample is not None:
                n_streams, seq = sample
                w_s = jnp.tile(w_s[:, :seq, :seq], (1, n_streams, n_streams))
                b_s = jnp.tile(b_s[:, :seq], (1, n_streams))
            bias_rows = jnp.repeat(b_s.T, LANES, axis=1)
            res = _sg_mixer(x, row(2), row(3), p["sg_w_in"], p["sg_ln_g"][jm:jm + 1],
                            p["sg_ln_b"][jm:jm + 1], w_s, bias_rows, p["sg_w_out"], jm,
                            tm=tm, gpt=4, seg_mode=None if sample is None else sample[1],
                            emit_v=sample is not None)
            if sample is not None:
                x, v_rows = res
                sg_vs.append(v_rows)
            else:
                x = res
        else:
            x, st = _hgrn_mixer(x, row(2), row(3), p["hg_w_in"], p["hg_lower"],
                                p["hg_norm_g"][jm:jm + 1], p["hg_w_out"], hg_init[jm], jm,
                                layer=i, tm=tm, chunk=SG_CHUNK if sample is None else x.shape[0],
                                seg=SG_CHUNK if sample is None else sample[1], nh=2)
            hg_states.append(st)
        x = _ffn(x, row(4), row(5), p["ffn_w_up"], p["ffn_w_down"], i, 1, tm=tm, tf=tf)
    return x, hg_states, sg_vs


def kernel(x_prompt, x_sample, state_hgrn, norm_g, ffn_w_up, ffn_w_down, sg_w_in, sg_ln_g, sg_ln_b,
           sg_w_s, sg_b_s, sg_w_out, hg_w_in, hg_lower, hg_norm_g, hg_w_out):
    batch, seq, d = x_prompt.shape
    dec_batch, dec_seq, _ = x_sample.shape
    n_hg = hg_w_in.shape[0]
    heads = d // HEAD_DIM
    p = dict(norm_g=norm_g, ffn_w_up=ffn_w_up.astype(BF16), ffn_w_down=ffn_w_down.astype(BF16),
             sg_w_in=sg_w_in.astype(BF16), sg_ln_g=sg_ln_g, sg_ln_b=sg_ln_b, sg_w_s=sg_w_s,
             sg_b_s=sg_b_s, sg_w_out=sg_w_out.astype(BF16), hg_w_in=hg_w_in.astype(BF16),
             hg_lower=hg_lower, hg_norm_g=hg_norm_g, hg_w_out=hg_w_out.astype(BF16))

    ys, sts = [], []
    for b in range(batch):
        init = [jnp.zeros((1, heads, HEAD_DIM, HEAD_DIM), F32) for _ in range(n_hg)]
        y, hg_p, _ = _trunk(x_prompt[b], init, p, tm=512, tf=512, sample=None)
        ys.append(y)
        sts.append(jnp.stack([s[0] for s in hg_p], axis=0))
    y_prompt = jnp.stack(ys, axis=0)
    state_hgrn_prompt = jnp.stack(sts, axis=1)

    init = [state_hgrn[jm].astype(F32) for jm in range(n_hg)]
    y, hg_s, sg_v = _trunk(x_sample.reshape(dec_batch * dec_seq, d), init, p,
                           tm=dec_batch * dec_seq, tf=512, sample=(dec_batch, dec_seq))
    y_sample = y.reshape(dec_batch, dec_seq, d)
    state_hgrn_sample = jnp.stack(hg_s, axis=0)
    state_sg_v_sample = jnp.stack([v.reshape(dec_batch, dec_seq, d) for v in sg_v], axis=0)
    return (y_prompt, y_sample, state_hgrn_prompt, state_hgrn_sample, state_sg_v_sample)
```

```python
import functools

import numpy as np
import jax
import jax.numpy as jnp
from jax import lax
from jax.experimental import pallas as pl
from jax.experimental.pallas import tpu as pltpu

EPS = 1e-6
F32 = jnp.float32
BF16 = jnp.bfloat16

LANES = 128
ROW_CHUNK = 16
HEAD_DIM = 128
SG_CHUNK = 128
FFN_TILE = 512
SG_TILE = 512
HG_TILE = 256
STREAM_CHUNK = 64
V7X_VMEM_BYTES = 64 * 1024 * 1024
VMEM_LIMIT_BYTES = V7X_VMEM_BYTES - 8 * 1024 * 1024
MAX_SPLIT_DECAY = 140.0


def _dot(a, b):
    return jnp.dot(a, b, preferred_element_type=F32)


def _dot_nt(a, b):
    return lax.dot_general(a, b, (((1,), (1,)), ((), ())), preferred_element_type=F32)


def _dot_tn(a, b):
    return lax.dot_general(a, b, (((0,), (0,)), ((), ())), preferred_element_type=F32)


def _rms(x, g):
    return x * lax.rsqrt(jnp.mean(x * x, axis=-1, keepdims=True) + EPS) * g


def _silu(x):
    return x * jax.nn.sigmoid(x)


def _gelu(x):
    return 0.5 * x * (1.0 + lax.erf(x * np.float32(np.sqrt(0.5))))


def _for_tiles(n_rows, n_cols, fn):
    def step(c, carry):
        rows = pl.ds(pl.multiple_of(c * ROW_CHUNK, ROW_CHUNK), ROW_CHUNK)
        for ct in range(n_cols // LANES):
            fn(rows, slice(ct * LANES, (ct + 1) * LANES))
        return carry
    lax.fori_loop(0, n_rows // ROW_CHUNK, step, 0, unroll=max(2, 32 * LANES // n_cols))


def _row_rsqrt(x):
    r = lax.rsqrt(jnp.mean(x * x, axis=-1, keepdims=True) + EPS)
    return jnp.broadcast_to(r, (x.shape[0], LANES))


def _pre_norm(x_ref, g_ref, h_scr, acc_scr, rs_scr):
    rs_scr[...] = _row_rsqrt(x_ref[...])

    def tile(rows, cols):
        h_scr[rows, cols] = (x_ref[rows, cols] * rs_scr[rows, :] * g_ref[:, cols]).astype(BF16)
        acc_scr[rows, cols] = jnp.zeros((ROW_CHUNK, LANES), F32)
    _for_tiles(x_ref.shape[0], x_ref.shape[1], tile)


def _post_norm(x_ref, g_ref, acc_scr, rs_scr, o_ref, scale):
    rs_scr[...] = _row_rsqrt(acc_scr[...])

    def tile(rows, cols):
        o_ref[rows, cols] = x_ref[rows, cols] + (acc_scr[rows, cols] * rs_scr[rows, :]) * (scale * g_ref[:, cols])
    _for_tiles(x_ref.shape[0], x_ref.shape[1], tile)


def _params():
    return pltpu.CompilerParams(dimension_semantics=("arbitrary", "arbitrary"),
                                vmem_limit_bytes=VMEM_LIMIT_BYTES)


def _ffn_body(xn_ref, xp_ref, gpre_ref, gpost_ref, wa_ref, wb_ref, wd_ref, o_ref, h0, h1, acc0, acc1,
              *, nt, nslice):
    r = pl.program_id(0)
    j = pl.program_id(1)
    rs = xn_ref.shape[0]
    base = jnp.minimum(j, nslice - 1) * rs

    def chunk_rows(c):
        return pl.ds(pl.multiple_of(base + c * ROW_CHUNK, ROW_CHUNK), ROW_CHUNK)

    def stages(h_pre, h_main, acc_main, acc_post):
        def pre():
            for c in range(rs // ROW_CHUNK):
                xc = xn_ref[c * ROW_CHUNK:(c + 1) * ROW_CHUNK, :]
                h_pre[chunk_rows(c), :] = _rms(xc, gpre_ref[...]).astype(BF16)

        def main():
            h = h_main[...]
            a = _dot(h, wa_ref[...])
            b = _dot(h, wb_ref[...])
            contrib = _dot((_silu(a) * b).astype(BF16), wd_ref[...])
            acc_main[...] = jnp.where(j == 0, contrib, acc_main[...] + contrib)

        def post():
            for c in range(rs // ROW_CHUNK):
                rows = slice(c * ROW_CHUNK, (c + 1) * ROW_CHUNK)
                o_ref[rows, :] = xp_ref[rows, :] + 0.5 * _rms(acc_post[chunk_rows(c), :], gpost_ref[...])

        steady = jnp.logical_and(r >= 2, r < nt)

        @pl.when(steady)
        def _():
            pre()
            main()
            post()

        @pl.when(jnp.logical_not(steady))
        def _():
            @pl.when(jnp.logical_and(r == 0, j == 0))
            def _():
                acc0[...] = jnp.zeros_like(acc0)
                acc1[...] = jnp.zeros_like(acc1)
            pl.when(r < nt)(pre)
            pl.when(jnp.logical_and(r >= 1, r <= nt))(main)
            pl.when(r >= 2)(post)

    pl.when(r % 2 == 0)(lambda: stages(h0, h1, acc1, acc0))
    pl.when(r % 2 == 1)(lambda: stages(h1, h0, acc0, acc1))


def _ffn(x, g_pre, g_post, w_up, w_down, li, k, *, tm):
    t, d = x.shape
    f = w_down.shape[2]
    tf = w_up.shape[4]
    nf = f // tf
    nt = t // tm
    nslice = min(nf, tm // ROW_CHUNK)
    while tm % (nslice * ROW_CHUNK):
        nslice -= 1
    rs = tm // nslice
    assert t % tm == 0 and f % tf == 0

    def w_step(r, j):
        return jnp.where(r == 0, 0, jnp.where(r > nt, nf - 1, j))

    def next_slice(r, j):
        return jnp.where(r < nt, r * nslice + jnp.minimum(j, nslice - 1), nt * nslice - 1)

    def prev_slice(r, j):
        return jnp.where(r >= 2, (r - 2) * nslice + jnp.minimum(j, nslice - 1), 0)

    return pl.pallas_call(
        functools.partial(_ffn_body, nt=nt, nslice=nslice),
        out_shape=jax.ShapeDtypeStruct((t, d), F32),
        grid=(nt + 2, nf),
        in_specs=[
            pl.BlockSpec((rs, d), lambda r, j: (next_slice(r, j), 0)),
            pl.BlockSpec((rs, d), lambda r, j: (prev_slice(r, j), 0)),
            pl.BlockSpec((1, d), lambda r, j: (0, 0)),
            pl.BlockSpec((1, d), lambda r, j: (0, 0)),
            pl.BlockSpec((None, None, None, d, tf), lambda r, j: (li, k, w_step(r, j), 0, 0)),
            pl.BlockSpec((None, None, None, d, tf), lambda r, j: (li, k, w_step(r, j) + nf, 0, 0)),
            pl.BlockSpec((None, None, tf, d), lambda r, j: (li, k, w_step(r, j), 0)),
        ],
        out_specs=pl.BlockSpec((rs, d), lambda r, j: (prev_slice(r, j), 0)),
        scratch_shapes=[pltpu.VMEM((tm, d), BF16), pltpu.VMEM((tm, d), BF16),
                        pltpu.VMEM((tm, d), F32), pltpu.VMEM((tm, d), F32)],
        compiler_params=_params(),
    )(x, x, g_pre, g_post, w_up, w_up, w_down)


def _sg_mask(seg_mode):
    row = lax.broadcasted_iota(jnp.int32, (SG_CHUNK, SG_CHUNK), 0)
    col = lax.broadcasted_iota(jnp.int32, (SG_CHUNK, SG_CHUNK), 1)
    if seg_mode is None:
        return (row // STREAM_CHUNK) >= (col // STREAM_CHUNK)
    return (row // seg_mode) == (col // seg_mode)


def _sg_body(x_ref, g2_ref, g3_ref, wu_ref, wv_ref, lng_ref, lnb_ref, ws_ref, bias_ref, wo_ref,
             *rest, ng, gpt, nchunk, seg_mode, emit_v):
    if emit_v:
        o_ref, vout_ref, h_scr, u_scr, v_scr, y_scr, acc_scr, rs_scr, mu_scr = rest
    else:
        o_ref, h_scr, u_scr, v_scr, y_scr, acc_scr, rs_scr, mu_scr = rest
        vout_ref = None
    j = pl.program_id(1)
    tn = gpt * LANES
    d = ng * tn
    tm = x_ref.shape[0]

    @pl.when(j == 0)
    def _():
        _pre_norm(x_ref, g2_ref, h_scr, acc_scr, rs_scr)

    @pl.when(j < ng)
    def _():
        h = h_scr[...]
        u_scr[j] = _gelu(_dot(h, wu_ref[...]))
        v_scr[j] = _gelu(_dot(h, wv_ref[...]))

    @pl.when(j == ng)
    def _():
        mu = sum(jnp.sum(v_scr[jj], axis=-1, keepdims=True) for jj in range(ng)) * (1.0 / d)
        var = sum(jnp.sum(jnp.square(v_scr[jj] - mu), axis=-1, keepdims=True)
                  for jj in range(ng)) * (1.0 / d)
        mu_scr[...] = jnp.broadcast_to(mu, (tm, LANES))
        rs_scr[...] = jnp.broadcast_to(lax.rsqrt(var + EPS), (tm, LANES))
        for jj in range(ng):
            def tile(rows, cols, jj=jj):
                gcols = slice(jj * tn + cols.start, jj * tn + cols.stop)
                vn = ((v_scr[jj, rows, cols] - mu_scr[rows, :]) * rs_scr[rows, :] * lng_ref[:, gcols]
                      + lnb_ref[:, gcols])
                v_scr[jj, rows, cols] = vn
                if emit_v:
                    vout_ref[rows, gcols] = vn
            _for_tiles(tm, tn, tile)

    @pl.when(j >= ng)
    def _():
        jj = j - ng
        mask = _sg_mask(seg_mode)
        for g in range(gpt):
            cols = slice(g * LANES, (g + 1) * LANES)
            w = jnp.where(mask, ws_ref[g], 0.0).astype(BF16)
            for n in range(nchunk):
                rows = slice(n * SG_CHUNK, (n + 1) * SG_CHUNK)
                s = _dot(w, v_scr[jj, rows, cols].astype(BF16)) + bias_ref[:, cols]
                y_scr[rows, cols] = (u_scr[jj, rows, cols] * s).astype(BF16)
        acc_scr[...] += _dot(y_scr[...], wo_ref[...])

    @pl.when(j == 2 * ng - 1)
    def _():
        _post_norm(x_ref, g3_ref, acc_scr, rs_scr, o_ref, 1.0)


def _sg_mixer(x, g2, g3, w_in, ln_g, ln_b, w_s, bias_rows, w_out, jm, *, tm, seg_mode, emit_v):
    t, d = x.shape
    tn = w_in.shape[3]
    gpt = tn // LANES
    ng = d // tn
    assert t % tm == 0 and tm % SG_CHUNK == 0 and d % tn == 0 and tm % ROW_CHUNK == 0
    last = ng - 1
    body = functools.partial(_sg_body, ng=ng, gpt=gpt, nchunk=tm // SG_CHUNK,
                             seg_mode=seg_mode, emit_v=emit_v)
    row_spec = pl.BlockSpec((tm, d), lambda i, j: (i, 0))
    vec_spec = pl.BlockSpec((1, d), lambda i, j: (0, 0))
    out_shape = jax.ShapeDtypeStruct((t, d), F32)
    return pl.pallas_call(
        body,
        out_shape=(out_shape, out_shape) if emit_v else out_shape,
        grid=(t // tm, 2 * ng),
        in_specs=[
            row_spec, vec_spec, vec_spec,
            pl.BlockSpec((None, None, d, tn), lambda i, j: (jm, jnp.minimum(j, last), 0, 0)),
            pl.BlockSpec((None, None, d, tn), lambda i, j: (jm, ng + jnp.minimum(j, last), 0, 0)),
            vec_spec, vec_spec,
            pl.BlockSpec((gpt, SG_CHUNK, SG_CHUNK), lambda i, j: (jnp.maximum(j - ng, 0), 0, 0)),
            pl.BlockSpec((SG_CHUNK, tn), lambda i, j: (0, jnp.maximum(j - ng, 0))),
            pl.BlockSpec((None, tn, d), lambda i, j: (jm, jnp.maximum(j - ng, 0), 0)),
        ],
        out_specs=(row_spec, row_spec) if emit_v else row_spec,
        scratch_shapes=[
            pltpu.VMEM((tm, d), BF16),
            pltpu.VMEM((ng, tm, tn), F32),
            pltpu.VMEM((ng, tm, tn), F32),
            pltpu.VMEM((tm, tn), BF16),
            pltpu.VMEM((tm, d), F32),
            pltpu.VMEM((tm, LANES), F32),
            pltpu.VMEM((tm, LANES), F32),
        ],
        compiler_params=_params(),
    )(x, g2, g3, w_in, w_in, ln_g, ln_b, w_s, bias_rows, w_out)


def _scan_tables(chunk, seg):
    levels = []
    m = seg // 2
    while m >= 1:
        levels.append(m)
        m //= 2
    nb = 2 + len(levels)
    t = np.arange(chunk)[:, None]
    j = np.arange(chunk)[None, :]
    same_seg = (t // seg) == (j // seg)
    sums = np.zeros((nb, chunk, chunk), np.float32)
    sums[0] = same_seg & (j <= t)
    sums[1] = same_seg & (j > t)
    level_id = np.where(t == j, 0, -1).astype(np.int32)
    for li, m in enumerate(levels):
        g = 2 * m
        ref = (t // g) * g + m - 1
        odd = (t % g) >= m
        sums[2 + li] = np.where(odd, (j > ref) & (j <= t), (j > t) & (j <= ref))
        pair = ((t // g) == (j // g)) & odd & ((j % g) < m)
        level_id = np.where(pair, li + 1, level_id).astype(np.int32)
    sums = sums.reshape(nb * chunk, chunk)
    return np.concatenate([sums, sums, sums], axis=1), level_id, nb


def _split3(x):
    hi = x.astype(BF16)
    r1 = x - hi.astype(F32)
    mid = r1.astype(BF16)
    lo = (r1 - mid.astype(F32)).astype(BF16)
    return jnp.concatenate([hi, mid, lo], axis=0)


def _hgrn_body(x_ref, g2_ref, g3_ref, wq_ref, wf_ref, wi_ref, wg_ref, hl_ref, ng_ref, wo_ref,
               sums_ref, lid_ref, s0_ref, o_ref, sout_ref,
               h_scr, q_scr, lf_scr, k_scr, v_scr, gate_scr, b_scr, e_scr, st_scr, on_scr, acc_scr, rs_scr,
               *, layer, chunk, seg, nh, nb, nchunk):
    i = pl.program_id(0)
    j = pl.program_id(1)
    nseg = chunk // seg
    nlev = nb - 2

    @pl.when(j == 0)
    def _():
        _pre_norm(x_ref, g2_ref, h_scr, acc_scr, rs_scr)

    @pl.when(i == 0)
    def _():
        for s in range(nseg):
            for hh in range(nh):
                st_scr[s, j * nh + hh] = s0_ref[s, j * nh + hh].T

    hl = hl_ref[...]
    e = jnp.exp(hl - jnp.max(hl, axis=0, keepdims=True))
    p = e / jnp.sum(e, axis=0, keepdims=True)
    lb = jnp.zeros_like(p[0:1])
    for r in range(1, layer + 1):
        lb = lb + p[r:r + 1]

    h = h_scr[...]
    q_scr[...] = _silu(_dot(h, wq_ref[...]))
    f = lb + (1.0 - lb) * jax.nn.sigmoid(_dot(h, wf_ref[...]))
    lf_scr[...] = jnp.log(f)
    k_scr[...] = 1.0 - f
    v_scr[...] = _dot(h, wi_ref[...]).astype(BF16)
    gate_scr[...] = _silu(_dot(h, wg_ref[...]))

    lid = lid_ref[...]

    def carry_state(rows, cols, head, o_intra, q_pre, k_end, vh, e_end):
        inter = []
        for s in range(nseg):
            srows = slice(s * seg, (s + 1) * seg)
            st = st_scr[s, head]
            inter.append(_dot_nt(q_pre[srows], st.astype(BF16)))
            st_scr[s, head] = st * e_end[s] + _dot_tn(vh[srows], k_end[srows])
        o = o_intra + (inter[0] if nseg == 1 else jnp.concatenate(inter, axis=0))
        on_scr[rows, cols] = (_rms(o, ng_ref[...]) * gate_scr[rows, cols]).astype(BF16)

    def split_scan():
        for c in range(nchunk):
            rows = slice(c * chunk, (c + 1) * chunk)
            b = b_scr[rows, :]
            half = 0.5 * b[chunk - 1:chunk, :]
            x = b - half
            e_q = jnp.exp(x)
            e_k = jnp.exp(-x)
            e_half = jnp.exp(half)
            for hh in range(nh):
                cols = slice(hh * HEAD_DIM, (hh + 1) * HEAD_DIM)
                qt = q_scr[rows, cols] * e_q[:, cols]
                kt = k_scr[rows, cols] * e_k[:, cols]
                vh = v_scr[rows, cols]
                eh = e_half[:, cols]
                sc = jnp.where(lid >= 0, _dot_nt(qt.astype(BF16), kt.astype(BF16)), 0.0)
                carry_state(rows, cols, j * nh + hh, _dot(sc.astype(BF16), vh),
                            (qt * eh).astype(BF16), (kt * eh).astype(BF16), vh, [eh * eh])

    def level_scan():
        for c in range(nchunk):
            rows = slice(c * chunk, (c + 1) * chunk)
            e_scr[...] = jnp.exp(_dot(sums_ref[...], _split3(lf_scr[rows, :])))
            for hh in range(nh):
                cols = slice(hh * HEAD_DIM, (hh + 1) * HEAD_DIM)
                qh = q_scr[rows, cols]
                kh = k_scr[rows, cols]
                vh = v_scr[rows, cols]
                e_pre = e_scr[0:chunk, cols]
                k_end = (kh * e_scr[chunk:2 * chunk, cols]).astype(BF16)
                sc = jnp.where(lid == 0, _dot_nt(qh.astype(BF16), kh.astype(BF16)), 0.0)
                for l in range(nlev):
                    el = e_scr[(2 + l) * chunk:(3 + l) * chunk, cols]
                    sc_l = _dot_nt((qh * el).astype(BF16), (kh * el).astype(BF16))
                    sc = jnp.where(lid == l + 1, sc_l, sc)
                e_end = [e_pre[(s + 1) * seg - 1:(s + 1) * seg, :] for s in range(nseg)]
                carry_state(rows, cols, j * nh + hh, _dot(sc.astype(BF16), vh),
                            (qh * e_pre).astype(BF16), k_end, vh, e_end)

    if nseg == 1:
        worst = jnp.float32(0.0)
        for c in range(nchunk):
            rows = slice(c * chunk, (c + 1) * chunk)
            b = _dot(sums_ref[0:chunk, :], _split3(lf_scr[rows, :]))
            b_scr[rows, :] = b
            worst = jnp.maximum(worst, jnp.max(-b[chunk - 1:chunk, :]))
        pl.when(worst <= MAX_SPLIT_DECAY)(split_scan)
        pl.when(jnp.logical_not(worst <= MAX_SPLIT_DECAY))(level_scan)
    else:
        level_scan()

    @pl.when(i == pl.num_programs(0) - 1)
    def _():
        for s in range(nseg):
            for hh in range(nh):
                sout_ref[s, j * nh + hh] = st_scr[s, j * nh + hh].T

    acc_scr[...] += _dot(on_scr[...], wo_ref[...])

    @pl.when(j == pl.num_programs(1) - 1)
    def _():
        _post_norm(x_ref, g3_ref, acc_scr, rs_scr, o_ref, 1.0)


def _hgrn_mixer(x, g2, g3, w_in, hg_lower, norm_g, w_out, s0, jm, *, layer, tm, chunk, seg):
    t, d = x.shape
    heads = d // HEAD_DIM
    nseg = chunk // seg
    tn = w_in.shape[3]
    nh = tn // HEAD_DIM
    nhg = heads // nh
    assert t % tm == 0 and tm % chunk == 0 and heads % nh == 0 and tm % ROW_CHUNK == 0
    assert s0.shape == (nseg, heads, HEAD_DIM, HEAD_DIM)
    assert nseg == 1 or t == chunk
    sums, level_id, nb = _scan_tables(chunk, seg)
    body = functools.partial(_hgrn_body, layer=layer, chunk=chunk, seg=seg, nh=nh, nb=nb,
                             nchunk=tm // chunk)
    row_spec = pl.BlockSpec((tm, d), lambda i, j: (i, 0))
    vec_spec = pl.BlockSpec((1, d), lambda i, j: (0, 0))
    nsec = d // tn
    state_spec = pl.BlockSpec(s0.shape, lambda i, j: (0, 0, 0, 0))
    return pl.pallas_call(
        body,
        out_shape=(jax.ShapeDtypeStruct((t, d), F32), jax.ShapeDtypeStruct(s0.shape, F32)),
        grid=(t // tm, nhg),
        in_specs=[
            row_spec, vec_spec, vec_spec,
            pl.BlockSpec((None, None, d, tn), lambda i, j: (jm, j, 0, 0)),
            pl.BlockSpec((None, None, d, tn), lambda i, j: (jm, nsec + j, 0, 0)),
            pl.BlockSpec((None, None, d, tn), lambda i, j: (jm, 2 * nsec + j, 0, 0)),
            pl.BlockSpec((None, None, d, tn), lambda i, j: (jm, 3 * nsec + j, 0, 0)),
            pl.BlockSpec((hg_lower.shape[0], tn), lambda i, j: (0, j)),
            pl.BlockSpec((1, HEAD_DIM), lambda i, j: (0, 0)),
            pl.BlockSpec((None, tn, d), lambda i, j: (jm, j, 0)),
            pl.BlockSpec(sums.shape, lambda i, j: (0, 0)),
            pl.BlockSpec(level_id.shape, lambda i, j: (0, 0)),
            state_spec,
        ],
        out_specs=(row_spec, state_spec),
        scratch_shapes=[
            pltpu.VMEM((tm, d), BF16),
            pltpu.VMEM((tm, tn), F32),
            pltpu.VMEM((tm, tn), F32),
            pltpu.VMEM((tm, tn), F32),
            pltpu.VMEM((tm, tn), BF16),
            pltpu.VMEM((tm, tn), F32),
            pltpu.VMEM((tm, tn), F32),
            pltpu.VMEM((nb * chunk, tn), F32),
            pltpu.VMEM((nseg, heads, HEAD_DIM, HEAD_DIM), F32),
            pltpu.VMEM((tm, tn), BF16),
            pltpu.VMEM((tm, d), F32),
            pltpu.VMEM((tm, LANES), F32),
        ],
        compiler_params=_params(),
    )(x, g2, g3, w_in, w_in, w_in, w_in, hg_lower, norm_g, w_out,
      jnp.asarray(sums, BF16), jnp.asarray(level_id), s0)


def _column_tiles(w, tn):
    *lead, d, n = w.shape
    w = w.astype(BF16).reshape(*lead, d, n // tn, tn)
    return jnp.swapaxes(w, -3, -2)


def _trunk(x, hg_init, p, *, tm, sample):
    depth = p["norm_g"].shape[0]
    n_mix = 2
    hg_states, sg_vs = [], []
    for i in range(depth):
        g = p["norm_g"][i]
        row = lambda r: g[r:r + 1]
        jm = i // n_mix
        x = _ffn(x, row(0), row(1), p["ffn_w_up"], p["ffn_w_down"], i, 0, tm=tm)
        if i % n_mix == 0:
            w_s, b_s = p["sg_w_s"][jm], p["sg_b_s"][jm]
            if sample is not None:
                n_streams, seq = sample
                w_s = jnp.tile(w_s[:, :seq, :seq], (1, n_streams, n_streams))
                b_s = jnp.tile(b_s[:, :seq], (1, n_streams))
            bias_rows = jnp.repeat(b_s.T, LANES, axis=1)
            res = _sg_mixer(x, row(2), row(3), p["sg_w_in"], p["sg_ln_g"][jm:jm + 1],
                            p["sg_ln_b"][jm:jm + 1], w_s, bias_rows, p["sg_w_out"], jm,
                            tm=tm, seg_mode=None if sample is None else sample[1],
                            emit_v=sample is not None)
            if sample is not None:
                x, v_rows = res
                sg_vs.append(v_rows)
            else:
                x = res
        else:
            x, st = _hgrn_mixer(x, row(2), row(3), p["hg_w_in"], p["hg_lower"],
                                p["hg_norm_g"][jm:jm + 1], p["hg_w_out"], hg_init[jm], jm,
                                layer=i, tm=tm, chunk=SG_CHUNK if sample is None else x.shape[0],
                                seg=SG_CHUNK if sample is None else sample[1])
            hg_states.append(st)
        x = _ffn(x, row(4), row(5), p["ffn_w_up"], p["ffn_w_down"], i, 1, tm=tm)
    return x, hg_states, sg_vs


def kernel(x_prompt, x_sample, state_hgrn, norm_g, ffn_w_up, ffn_w_down, sg_w_in, sg_ln_g, sg_ln_b,
           sg_w_s, sg_b_s, sg_w_out, hg_w_in, hg_lower, hg_norm_g, hg_w_out):
    batch, seq, d = x_prompt.shape
    dec_batch, dec_seq, _ = x_sample.shape
    n_hg = hg_w_in.shape[0]
    heads = d // HEAD_DIM
    p = dict(norm_g=norm_g, ffn_w_up=_column_tiles(ffn_w_up, FFN_TILE), ffn_w_down=ffn_w_down.astype(BF16),
             sg_w_in=_column_tiles(sg_w_in, SG_TILE), sg_ln_g=sg_ln_g, sg_ln_b=sg_ln_b, sg_w_s=sg_w_s,
             sg_b_s=sg_b_s, sg_w_out=sg_w_out.astype(BF16), hg_w_in=_column_tiles(hg_w_in, HG_TILE),
             hg_lower=hg_lower, hg_norm_g=hg_norm_g, hg_w_out=hg_w_out.astype(BF16))

    ys, sts = [], []
    for b in range(batch):
        init = [jnp.zeros((1, heads, HEAD_DIM, HEAD_DIM), F32) for _ in range(n_hg)]
        y, hg_p, _ = _trunk(x_prompt[b], init, p, tm=512, sample=None)
        ys.append(y)
        sts.append(jnp.stack([s[0] for s in hg_p], axis=0))
    y_prompt = jnp.stack(ys, axis=0)
    state_hgrn_prompt = jnp.stack(sts, axis=1)

    init = [state_hgrn[jm].astype(F32) for jm in range(n_hg)]
    y, hg_s, sg_v = _trunk(x_sample.reshape(dec_batch * dec_seq, d), init, p,
                           tm=dec_batch * dec_seq, sample=(dec_batch, dec_seq))
    y_sample = y.reshape(dec_batch, dec_seq, d)
    state_hgrn_sample = jnp.stack(hg_s, axis=0)
    state_sg_v_sample = jnp.stack([v.reshape(dec_batch, dec_seq, d) for v in sg_v], axis=0)
    return (y_prompt, y_sample, state_hgrn_prompt, state_hgrn_sample, state_sg_v_sample)
```

```python
import functools

import numpy as np
import jax
import jax.numpy as jnp
from jax import lax
from jax.experimental import pallas as pl
from jax.experimental.pallas import tpu as pltpu

EPS = 1e-6
F32 = jnp.float32
BF16 = jnp.bfloat16

LANES = 128
ROW_CHUNK = 16
HEAD_DIM = 128
SG_CHUNK = 128
FFN_TILE = 512
SG_TILE = 512
HG_TILE = 256
STREAM_CHUNK = 64
V7X_VMEM_BYTES = 64 * 1024 * 1024
VMEM_LIMIT_BYTES = V7X_VMEM_BYTES - 8 * 1024 * 1024
MAX_SPLIT_DECAY = 140.0


def _dot(a, b):
    return jnp.dot(a, b, preferred_element_type=F32)


def _dot_nt(a, b):
    return lax.dot_general(a, b, (((1,), (1,)), ((), ())), preferred_element_type=F32)


def _dot_tn(a, b):
    return lax.dot_general(a, b, (((0,), (0,)), ((), ())), preferred_element_type=F32)


def _rms(x, g):
    return x * lax.rsqrt(jnp.mean(x * x, axis=-1, keepdims=True) + EPS) * g


def _silu(x):
    return x * jax.nn.sigmoid(x)


def _gelu(x):
    return 0.5 * x * (1.0 + lax.erf(x * np.float32(np.sqrt(0.5))))


def _for_tiles(n_rows, n_cols, fn):
    def step(c, carry):
        rows = pl.ds(pl.multiple_of(c * ROW_CHUNK, ROW_CHUNK), ROW_CHUNK)
        for ct in range(n_cols // LANES):
            fn(rows, slice(ct * LANES, (ct + 1) * LANES))
        return carry
    lax.fori_loop(0, n_rows // ROW_CHUNK, step, 0, unroll=max(2, 32 * LANES // n_cols))


def _row_rsqrt(x):
    r = lax.rsqrt(jnp.mean(x * x, axis=-1, keepdims=True) + EPS)
    return jnp.broadcast_to(r, (x.shape[0], LANES))


def _pre_norm(x_ref, g_ref, h_scr, acc_scr, rs_scr):
    rs_scr[...] = _row_rsqrt(x_ref[...])

    def tile(rows, cols):
        h_scr[rows, cols] = (x_ref[rows, cols] * rs_scr[rows, :] * g_ref[:, cols]).astype(BF16)
        acc_scr[rows, cols] = jnp.zeros((ROW_CHUNK, LANES), F32)
    _for_tiles(x_ref.shape[0], x_ref.shape[1], tile)


def _post_norm(x_ref, g_ref, acc_scr, rs_scr, o_ref, scale):
    rs_scr[...] = _row_rsqrt(acc_scr[...])

    def tile(rows, cols):
        o_ref[rows, cols] = x_ref[rows, cols] + (acc_scr[rows, cols] * rs_scr[rows, :]) * (scale * g_ref[:, cols])
    _for_tiles(x_ref.shape[0], x_ref.shape[1], tile)


def _params():
    return pltpu.CompilerParams(dimension_semantics=("arbitrary", "arbitrary"),
                                vmem_limit_bytes=VMEM_LIMIT_BYTES)


def _ffn_body(xn_ref, xp_ref, gpre_ref, gpost_ref, wa_ref, wb_ref, wd_ref, o_ref, h0, h1, acc0, acc1,
              *, nt, nslice):
    r = pl.program_id(0)
    j = pl.program_id(1)
    rs = xn_ref.shape[0]
    base = jnp.minimum(j, nslice - 1) * rs

    def chunk_rows(c):
        return pl.ds(pl.multiple_of(base + c * ROW_CHUNK, ROW_CHUNK), ROW_CHUNK)

    def stages(h_pre, h_main, acc_main, acc_post):
        def pre():
            for c in range(rs // ROW_CHUNK):
                xc = xn_ref[c * ROW_CHUNK:(c + 1) * ROW_CHUNK, :]
                h_pre[chunk_rows(c), :] = _rms(xc, gpre_ref[...]).astype(BF16)

        def main():
            h = h_main[...]
            a = _dot(h, wa_ref[...])
            b = _dot(h, wb_ref[...])
            contrib = _dot((_silu(a) * b).astype(BF16), wd_ref[...])
            acc_main[...] = jnp.where(j == 0, contrib, acc_main[...] + contrib)

        def post():
            for c in range(rs // ROW_CHUNK):
                rows = slice(c * ROW_CHUNK, (c + 1) * ROW_CHUNK)
                o_ref[rows, :] = xp_ref[rows, :] + 0.5 * _rms(acc_post[chunk_rows(c), :], gpost_ref[...])

        steady = jnp.logical_and(r >= 2, r < nt)

        @pl.when(steady)
        def _():
            pre()
            main()
            post()

        @pl.when(jnp.logical_not(steady))
        def _():
            @pl.when(jnp.logical_and(r == 0, j == 0))
            def _():
                acc0[...] = jnp.zeros_like(acc0)
                acc1[...] = jnp.zeros_like(acc1)
            pl.when(r < nt)(pre)
            pl.when(jnp.logical_and(r >= 1, r <= nt))(main)
            pl.when(r >= 2)(post)

    pl.when(r % 2 == 0)(lambda: stages(h0, h1, acc1, acc0))
    pl.when(r % 2 == 1)(lambda: stages(h1, h0, acc0, acc1))


def _ffn(x, g_pre, g_post, w_up, w_down, li, k, *, tm):
    t, d = x.shape
    f = w_down.shape[2]
    tf = FFN_TILE
    nf = f // tf
    nt = t // tm
    nslice = min(nf, tm // ROW_CHUNK)
    while tm % (nslice * ROW_CHUNK):
        nslice -= 1
    rs = tm // nslice
    assert t % tm == 0 and f % tf == 0

    def w_step(r, j):
        return jnp.where(r == 0, 0, jnp.where(r > nt, nf - 1, j))

    def next_slice(r, j):
        return jnp.where(r < nt, r * nslice + jnp.minimum(j, nslice - 1), nt * nslice - 1)

    def prev_slice(r, j):
        return jnp.where(r >= 2, (r - 2) * nslice + jnp.minimum(j, nslice - 1), 0)

    return pl.pallas_call(
        functools.partial(_ffn_body, nt=nt, nslice=nslice),
        out_shape=jax.ShapeDtypeStruct((t, d), F32),
        grid=(nt + 2, nf),
        in_specs=[
            pl.BlockSpec((rs, d), lambda r, j: (next_slice(r, j), 0)),
            pl.BlockSpec((rs, d), lambda r, j: (prev_slice(r, j), 0)),
            pl.BlockSpec((1, d), lambda r, j: (0, 0)),
            pl.BlockSpec((1, d), lambda r, j: (0, 0)),
            pl.BlockSpec((None, None, d, tf), lambda r, j: (li, k, 0, w_step(r, j))),
            pl.BlockSpec((None, None, d, tf), lambda r, j: (li, k, 0, w_step(r, j) + nf)),
            pl.BlockSpec((None, None, tf, d), lambda r, j: (li, k, w_step(r, j), 0)),
        ],
        out_specs=pl.BlockSpec((rs, d), lambda r, j: (prev_slice(r, j), 0)),
        scratch_shapes=[pltpu.VMEM((tm, d), BF16), pltpu.VMEM((tm, d), BF16),
                        pltpu.VMEM((tm, d), F32), pltpu.VMEM((tm, d), F32)],
        compiler_params=_params(),
    )(x, x, g_pre, g_post, w_up, w_up, w_down)


def _sg_mask(seg_mode):
    row = lax.broadcasted_iota(jnp.int32, (SG_CHUNK, SG_CHUNK), 0)
    col = lax.broadcasted_iota(jnp.int32, (SG_CHUNK, SG_CHUNK), 1)
    if seg_mode is None:
        return (row // STREAM_CHUNK) >= (col // STREAM_CHUNK)
    return (row // seg_mode) == (col // seg_mode)


def _sg_body(x_ref, g2_ref, g3_ref, wu_ref, wv_ref, lng_ref, lnb_ref, ws_ref, bias_ref, wo_ref,
             *rest, ng, gpt, nchunk, seg_mode, emit_v):
    if emit_v:
        o_ref, vout_ref, h_scr, u_scr, v_scr, y_scr, acc_scr, rs_scr, mu_scr = rest
    else:
        o_ref, h_scr, u_scr, v_scr, y_scr, acc_scr, rs_scr, mu_scr = rest
        vout_ref = None
    j = pl.program_id(1)
    tn = gpt * LANES
    d = ng * tn
    tm = x_ref.shape[0]

    @pl.when(j == 0)
    def _():
        _pre_norm(x_ref, g2_ref, h_scr, acc_scr, rs_scr)

    @pl.when(j < ng)
    def _():
        h = h_scr[...]
        u_scr[j] = _gelu(_dot(h, wu_ref[...]))
        v_scr[j] = _gelu(_dot(h, wv_ref[...]))

    @pl.when(j == ng)
    def _():
        mu = sum(jnp.sum(v_scr[jj], axis=-1, keepdims=True) for jj in range(ng)) * (1.0 / d)
        var = sum(jnp.sum(jnp.square(v_scr[jj] - mu), axis=-1, keepdims=True)
                  for jj in range(ng)) * (1.0 / d)
        mu_scr[...] = jnp.broadcast_to(mu, (tm, LANES))
        rs_scr[...] = jnp.broadcast_to(lax.rsqrt(var + EPS), (tm, LANES))
        for jj in range(ng):
            def tile(rows, cols, jj=jj):
                gcols = slice(jj * tn + cols.start, jj * tn + cols.stop)
                vn = ((v_scr[jj, rows, cols] - mu_scr[rows, :]) * rs_scr[rows, :] * lng_ref[:, gcols]
                      + lnb_ref[:, gcols])
                v_scr[jj, rows, cols] = vn
                if emit_v:
                    vout_ref[rows, gcols] = vn
            _for_tiles(tm, tn, tile)

    @pl.when(j >= ng)
    def _():
        jj = j - ng
        mask = _sg_mask(seg_mode)
        for g in range(gpt):
            cols = slice(g * LANES, (g + 1) * LANES)
            w = jnp.where(mask, ws_ref[g], 0.0).astype(BF16)
            for n in range(nchunk):
                rows = slice(n * SG_CHUNK, (n + 1) * SG_CHUNK)
                s = _dot(w, v_scr[jj, rows, cols].astype(BF16)) + bias_ref[:, cols]
                y_scr[rows, cols] = (u_scr[jj, rows, cols] * s).astype(BF16)
        acc_scr[...] += _dot(y_scr[...], wo_ref[...])

    @pl.when(j == 2 * ng - 1)
    def _():
        _post_norm(x_ref, g3_ref, acc_scr, rs_scr, o_ref, 1.0)


def _sg_mixer(x, g2, g3, w_in, ln_g, ln_b, w_s, bias_rows, w_out, jm, *, tm, seg_mode, emit_v):
    t, d = x.shape
    tn = SG_TILE
    gpt = tn // LANES
    ng = d // tn
    assert t % tm == 0 and tm % SG_CHUNK == 0 and d % tn == 0 and tm % ROW_CHUNK == 0
    last = ng - 1
    body = functools.partial(_sg_body, ng=ng, gpt=gpt, nchunk=tm // SG_CHUNK,
                             seg_mode=seg_mode, emit_v=emit_v)
    row_spec = pl.BlockSpec((tm, d), lambda i, j: (i, 0))
    vec_spec = pl.BlockSpec((1, d), lambda i, j: (0, 0))
    out_shape = jax.ShapeDtypeStruct((t, d), F32)
    return pl.pallas_call(
        body,
        out_shape=(out_shape, out_shape) if emit_v else out_shape,
        grid=(t // tm, 2 * ng),
        in_specs=[
            row_spec, vec_spec, vec_spec,
            pl.BlockSpec((None, d, tn), lambda i, j: (jm, 0, jnp.minimum(j, last))),
            pl.BlockSpec((None, d, tn), lambda i, j: (jm, 0, ng + jnp.minimum(j, last))),
            vec_spec, vec_spec,
            pl.BlockSpec((gpt, SG_CHUNK, SG_CHUNK), lambda i, j: (jnp.maximum(j - ng, 0), 0, 0)),
            pl.BlockSpec((SG_CHUNK, tn), lambda i, j: (0, jnp.maximum(j - ng, 0))),
            pl.BlockSpec((None, tn, d), lambda i, j: (jm, jnp.maximum(j - ng, 0), 0)),
        ],
        out_specs=(row_spec, row_spec) if emit_v else row_spec,
        scratch_shapes=[
            pltpu.VMEM((tm, d), BF16),
            pltpu.VMEM((ng, tm, tn), F32),
            pltpu.VMEM((ng, tm, tn), F32),
            pltpu.VMEM((tm, tn), BF16),
            pltpu.VMEM((tm, d), F32),
            pltpu.VMEM((tm, LANES), F32),
            pltpu.VMEM((tm, LANES), F32),
        ],
        compiler_params=_params(),
    )(x, g2, g3, w_in, w_in, ln_g, ln_b, w_s, bias_rows, w_out)


def _scan_tables(chunk, seg):
    levels = []
    m = seg // 2
    while m >= 1:
        levels.append(m)
        m //= 2
    nb = 2 + len(levels)
    t = np.arange(chunk)[:, None]
    j = np.arange(chunk)[None, :]
    same_seg = (t // seg) == (j // seg)
    sums = np.zeros((nb, chunk, chunk), np.float32)
    sums[0] = same_seg & (j <= t)
    sums[1] = same_seg & (j > t)
    level_id = np.where(t == j, 0, -1).astype(np.int32)
    for li, m in enumerate(levels):
        g = 2 * m
        ref = (t // g) * g + m - 1
        odd = (t % g) >= m
        sums[2 + li] = np.where(odd, (j > ref) & (j <= t), (j > t) & (j <= ref))
        pair = ((t // g) == (j // g)) & odd & ((j % g) < m)
        level_id = np.where(pair, li + 1, level_id).astype(np.int32)
    sums = sums.reshape(nb * chunk, chunk)
    return np.concatenate([sums, sums, sums], axis=1), level_id, nb


def _split3(x):
    hi = x.astype(BF16)
    r1 = x - hi.astype(F32)
    mid = r1.astype(BF16)
    lo = (r1 - mid.astype(F32)).astype(BF16)
    return jnp.concatenate([hi, mid, lo], axis=0)


def _hgrn_body(x_ref, g2_ref, g3_ref, wq_ref, wf_ref, wi_ref, wg_ref, hl_ref, ng_ref, wo_ref,
               sums_ref, lid_ref, s0_ref, o_ref, sout_ref,
               h_scr, q_scr, lf_scr, k_scr, v_scr, gate_scr, b_scr, e_scr, st_scr, on_scr, acc_scr, rs_scr,
               *, layer, chunk, seg, nh, nb, nchunk):
    i = pl.program_id(0)
    j = pl.program_id(1)
    nseg = chunk // seg
    nlev = nb - 2

    @pl.when(j == 0)
    def _():
        _pre_norm(x_ref, g2_ref, h_scr, acc_scr, rs_scr)

    @pl.when(i == 0)
    def _():
        for s in range(nseg):
            for hh in range(nh):
                st_scr[s, j * nh + hh] = s0_ref[s, j * nh + hh].T

    hl = hl_ref[...]
    e = jnp.exp(hl - jnp.max(hl, axis=0, keepdims=True))
    p = e / jnp.sum(e, axis=0, keepdims=True)
    lb = jnp.zeros_like(p[0:1])
    for r in range(1, layer + 1):
        lb = lb + p[r:r + 1]

    h = h_scr[...]
    q_scr[...] = _silu(_dot(h, wq_ref[...]))
    f = lb + (1.0 - lb) * jax.nn.sigmoid(_dot(h, wf_ref[...]))
    lf_scr[...] = jnp.log(f)
    k_scr[...] = 1.0 - f
    v_scr[...] = _dot(h, wi_ref[...]).astype(BF16)
    gate_scr[...] = _silu(_dot(h, wg_ref[...]))

    lid = lid_ref[...]

    def carry_state(rows, cols, head, o_intra, q_pre, k_end, vh, e_end):
        inter = []
        for s in range(nseg):
            srows = slice(s * seg, (s + 1) * seg)
            st = st_scr[s, head]
            inter.append(_dot_nt(q_pre[srows], st.astype(BF16)))
            st_scr[s, head] = st * e_end[s] + _dot_tn(vh[srows], k_end[srows])
        o = o_intra + (inter[0] if nseg == 1 else jnp.concatenate(inter, axis=0))
        on_scr[rows, cols] = (_rms(o, ng_ref[...]) * gate_scr[rows, cols]).astype(BF16)

    def split_scan():
        for c in range(nchunk):
            rows = slice(c * chunk, (c + 1) * chunk)
            b = b_scr[rows, :]
            half = 0.5 * b[chunk - 1:chunk, :]
            x = b - half
            e_q = jnp.exp(x)
            e_k = jnp.exp(-x)
            e_half = jnp.exp(half)
            for hh in range(nh):
                cols = slice(hh * HEAD_DIM, (hh + 1) * HEAD_DIM)
                qt = q_scr[rows, cols] * e_q[:, cols]
                kt = k_scr[rows, cols] * e_k[:, cols]
                vh = v_scr[rows, cols]
                eh = e_half[:, cols]
                sc = jnp.where(lid >= 0, _dot_nt(qt.astype(BF16), kt.astype(BF16)), 0.0)
                carry_state(rows, cols, j * nh + hh, _dot(sc.astype(BF16), vh),
                            (qt * eh).astype(BF16), (kt * eh).astype(BF16), vh, [eh * eh])

    def level_scan():
        for c in range(nchunk):
            rows = slice(c * chunk, (c + 1) * chunk)
            e_scr[...] = jnp.exp(_dot(sums_ref[...], _split3(lf_scr[rows, :])))
            for hh in range(nh):
                cols = slice(hh * HEAD_DIM, (hh + 1) * HEAD_DIM)
                qh = q_scr[rows, cols]
                kh = k_scr[rows, cols]
                vh = v_scr[rows, cols]
                e_pre = e_scr[0:chunk, cols]
                k_end = (kh * e_scr[chunk:2 * chunk, cols]).astype(BF16)
                sc = jnp.where(lid == 0, _dot_nt(qh.astype(BF16), kh.astype(BF16)), 0.0)
                for l in range(nlev):
                    el = e_scr[(2 + l) * chunk:(3 + l) * chunk, cols]
                    sc_l = _dot_nt((qh * el).astype(BF16), (kh * el).astype(BF16))
                    sc = jnp.where(lid == l + 1, sc_l, sc)
                e_end = [e_pre[(s + 1) * seg - 1:(s + 1) * seg, :] for s in range(nseg)]
                carry_state(rows, cols, j * nh + hh, _dot(sc.astype(BF16), vh),
                            (qh * e_pre).astype(BF16), k_end, vh, e_end)

    if nseg == 1:
        worst = jnp.float32(0.0)
        for c in range(nchunk):
            rows = slice(c * chunk, (c + 1) * chunk)
            b = _dot(sums_ref[0:chunk, :], _split3(lf_scr[rows, :]))
            b_scr[rows, :] = b
            worst = jnp.maximum(worst, jnp.max(-b[chunk - 1:chunk, :]))
        pl.when(worst <= MAX_SPLIT_DECAY)(split_scan)
        pl.when(jnp.logical_not(worst <= MAX_SPLIT_DECAY))(level_scan)
    else:
        level_scan()

    @pl.when(i == pl.num_programs(0) - 1)
    def _():
        for s in range(nseg):
            for hh in range(nh):
                sout_ref[s, j * nh + hh] = st_scr[s, j * nh + hh].T

    acc_scr[...] += _dot(on_scr[...], wo_ref[...])

    @pl.when(j == pl.num_programs(1) - 1)
    def _():
        _post_norm(x_ref, g3_ref, acc_scr, rs_scr, o_ref, 1.0)


def _hgrn_mixer(x, g2, g3, w_in, hg_lower, norm_g, w_out, s0, jm, *, layer, tm, chunk, seg):
    t, d = x.shape
    heads = d // HEAD_DIM
    nseg = chunk // seg
    tn = HG_TILE
    nh = tn // HEAD_DIM
    nhg = heads // nh
    assert t % tm == 0 and tm % chunk == 0 and heads % nh == 0 and tm % ROW_CHUNK == 0
    assert s0.shape == (nseg, heads, HEAD_DIM, HEAD_DIM)
    assert nseg == 1 or t == chunk
    sums, level_id, nb = _scan_tables(chunk, seg)
    body = functools.partial(_hgrn_body, layer=layer, chunk=chunk, seg=seg, nh=nh, nb=nb,
                             nchunk=tm // chunk)
    row_spec = pl.BlockSpec((tm, d), lambda i, j: (i, 0))
    vec_spec = pl.BlockSpec((1, d), lambda i, j: (0, 0))
    nsec = d // tn
    state_spec = pl.BlockSpec(s0.shape, lambda i, j: (0, 0, 0, 0))
    return pl.pallas_call(
        body,
        out_shape=(jax.ShapeDtypeStruct((t, d), F32), jax.ShapeDtypeStruct(s0.shape, F32)),
        grid=(t // tm, nhg),
        in_specs=[
            row_spec, vec_spec, vec_spec,
            pl.BlockSpec((None, d, tn), lambda i, j: (jm, 0, j)),
            pl.BlockSpec((None, d, tn), lambda i, j: (jm, 0, nsec + j)),
            pl.BlockSpec((None, d, tn), lambda i, j: (jm, 0, 2 * nsec + j)),
            pl.BlockSpec((None, d, tn), lambda i, j: (jm, 0, 3 * nsec + j)),
            pl.BlockSpec((hg_lower.shape[0], tn), lambda i, j: (0, j)),
            pl.BlockSpec((1, HEAD_DIM), lambda i, j: (0, 0)),
            pl.BlockSpec((None, tn, d), lambda i, j: (jm, j, 0)),
            pl.BlockSpec(sums.shape, lambda i, j: (0, 0)),
            pl.BlockSpec(level_id.shape, lambda i, j: (0, 0)),
            state_spec,
        ],
        out_specs=(row_spec, state_spec),
        scratch_shapes=[
            pltpu.VMEM((tm, d), BF16),
            pltpu.VMEM((tm, tn), F32),
            pltpu.VMEM((tm, tn), F32),
            pltpu.VMEM((tm, tn), F32),
            pltpu.VMEM((tm, tn), BF16),
            pltpu.VMEM((tm, tn), F32),
            pltpu.VMEM((tm, tn), F32),
            pltpu.VMEM((nb * chunk, tn), F32),
            pltpu.VMEM((nseg, heads, HEAD_DIM, HEAD_DIM), F32),
            pltpu.VMEM((tm, tn), BF16),
            pltpu.VMEM((tm, d), F32),
            pltpu.VMEM((tm, LANES), F32),
        ],
        compiler_params=_params(),
    )(x, g2, g3, w_in, w_in, w_in, w_in, hg_lower, norm_g, w_out,
      jnp.asarray(sums, BF16), jnp.asarray(level_id), s0)


def _trunk(x, hg_init, p, *, tm, ffn_tm, sample):
    depth = p["norm_g"].shape[0]
    n_mix = 2
    hg_states, sg_vs = [], []
    for i in range(depth):
        g = p["norm_g"][i]
        row = lambda r: g[r:r + 1]
        jm = i // n_mix
        x = _ffn(x, row(0), row(1), p["ffn_w_up"], p["ffn_w_down"], i, 0, tm=ffn_tm)
        if i % n_mix == 0:
            w_s, b_s = p["sg_w_s"][jm], p["sg_b_s"][jm]
            if sample is not None:
                n_streams, seq = sample
                w_s = jnp.tile(w_s[:, :seq, :seq], (1, n_streams, n_streams))
                b_s = jnp.tile(b_s[:, :seq], (1, n_streams))
            bias_rows = jnp.repeat(b_s.T, LANES, axis=1)
            res = _sg_mixer(x, row(2), row(3), p["sg_w_in"], p["sg_ln_g"][jm:jm + 1],
                            p["sg_ln_b"][jm:jm + 1], w_s, bias_rows, p["sg_w_out"], jm,
                            tm=tm, seg_mode=None if sample is None else sample[1],
                            emit_v=sample is not None)
            if sample is not None:
                x, v_rows = res
                sg_vs.append(v_rows)
            else:
                x = res
        else:
            x, st = _hgrn_mixer(x, row(2), row(3), p["hg_w_in"], p["hg_lower"],
                                p["hg_norm_g"][jm:jm + 1], p["hg_w_out"], hg_init[jm], jm,
                                layer=i, tm=tm, chunk=SG_CHUNK if sample is None else x.shape[0],
                                seg=SG_CHUNK if sample is None else sample[1])
            hg_states.append(st)
        x = _ffn(x, row(4), row(5), p["ffn_w_up"], p["ffn_w_down"], i, 1, tm=ffn_tm)
    return x, hg_states, sg_vs


def kernel(x_prompt, x_sample, state_hgrn, norm_g, ffn_w_up, ffn_w_down, sg_w_in, sg_ln_g, sg_ln_b,
           sg_w_s, sg_b_s, sg_w_out, hg_w_in, hg_lower, hg_norm_g, hg_w_out):
    batch, seq, d = x_prompt.shape
    dec_batch, dec_seq, _ = x_sample.shape
    n_hg = hg_w_in.shape[0]
    heads = d // HEAD_DIM
    p = dict(norm_g=norm_g, ffn_w_up=ffn_w_up.astype(BF16), ffn_w_down=ffn_w_down.astype(BF16),
             sg_w_in=sg_w_in.astype(BF16), sg_ln_g=sg_ln_g, sg_ln_b=sg_ln_b, sg_w_s=sg_w_s,
             sg_b_s=sg_b_s, sg_w_out=sg_w_out.astype(BF16), hg_w_in=hg_w_in.astype(BF16),
             hg_lower=hg_lower, hg_norm_g=hg_norm_g, hg_w_out=hg_w_out.astype(BF16))

    ys, sts = [], []
    for b in range(batch):
        init = [jnp.zeros((1, heads, HEAD_DIM, HEAD_DIM), F32) for _ in range(n_hg)]
        y, hg_p, _ = _trunk(x_prompt[b], init, p, tm=512, ffn_tm=1024, sample=None)
        ys.append(y)
        sts.append(jnp.stack([s[0] for s in hg_p], axis=0))
    y_prompt = jnp.stack(ys, axis=0)
    state_hgrn_prompt = jnp.stack(sts, axis=1)

    init = [state_hgrn[jm].astype(F32) for jm in range(n_hg)]
    y, hg_s, sg_v = _trunk(x_sample.reshape(dec_batch * dec_seq, d), init, p,
                           tm=dec_batch * dec_seq, ffn_tm=dec_batch * dec_seq, sample=(dec_batch, dec_seq))
    y_sample = y.reshape(dec_batch, dec_seq, d)
    state_hgrn_sample = jnp.stack(hg_s, axis=0)
    state_sg_v_sample = jnp.stack([v.reshape(dec_batch, dec_seq, d) for v in sg_v], axis=0)
    return (y_prompt, y_sample, state_hgrn_prompt, state_hgrn_sample, state_sg_v_sample)
```

```python
import functools

import numpy as np
import jax
import jax.numpy as jnp
from jax import lax
from jax.experimental import pallas as pl
from jax.experimental.pallas import tpu as pltpu

EPS = 1e-6
F32 = jnp.float32
BF16 = jnp.bfloat16

LANES = 128
ROW_CHUNK = 16
HEAD_DIM = 128
SG_CHUNK = 128
FFN_TILE = 512
SG_TILE = 512
HG_TILE = 256
STREAM_CHUNK = 64
V7X_VMEM_BYTES = 64 * 1024 * 1024
VMEM_LIMIT_BYTES = V7X_VMEM_BYTES - 8 * 1024 * 1024
MAX_SPLIT_DECAY = 140.0


def _dot(a, b):
    return jnp.dot(a, b, preferred_element_type=F32)


def _dot_nt(a, b):
    return lax.dot_general(a, b, (((1,), (1,)), ((), ())), preferred_element_type=F32)


def _dot_tn(a, b):
    return lax.dot_general(a, b, (((0,), (0,)), ((), ())), preferred_element_type=F32)


def _rms(x, g):
    return x * lax.rsqrt(jnp.mean(x * x, axis=-1, keepdims=True) + EPS) * g


def _silu(x):
    return x * jax.nn.sigmoid(x)


def _gelu(x):
    return 0.5 * x * (1.0 + lax.erf(x * np.float32(np.sqrt(0.5))))


def _for_tiles(n_rows, n_cols, fn):
    def step(c, carry):
        rows = pl.ds(pl.multiple_of(c * ROW_CHUNK, ROW_CHUNK), ROW_CHUNK)
        for ct in range(n_cols // LANES):
            fn(rows, slice(ct * LANES, (ct + 1) * LANES))
        return carry
    lax.fori_loop(0, n_rows // ROW_CHUNK, step, 0, unroll=max(2, 32 * LANES // n_cols))


def _row_rsqrt(x):
    r = lax.rsqrt(jnp.mean(x * x, axis=-1, keepdims=True) + EPS)
    return jnp.broadcast_to(r, (x.shape[0], LANES))


def _pre_norm(x_ref, g_ref, h_scr, acc_scr, rs_scr):
    rs_scr[...] = _row_rsqrt(x_ref[...])

    def tile(rows, cols):
        h_scr[rows, cols] = (x_ref[rows, cols] * rs_scr[rows, :] * g_ref[:, cols]).astype(BF16)
        acc_scr[rows, cols] = jnp.zeros((ROW_CHUNK, LANES), F32)
    _for_tiles(x_ref.shape[0], x_ref.shape[1], tile)


def _post_norm(x_ref, g_ref, acc_scr, rs_scr, o_ref, scale):
    rs_scr[...] = _row_rsqrt(acc_scr[...])

    def tile(rows, cols):
        o_ref[rows, cols] = x_ref[rows, cols] + (acc_scr[rows, cols] * rs_scr[rows, :]) * (scale * g_ref[:, cols])
    _for_tiles(x_ref.shape[0], x_ref.shape[1], tile)


def _params():
    return pltpu.CompilerParams(dimension_semantics=("arbitrary", "arbitrary"),
                                vmem_limit_bytes=VMEM_LIMIT_BYTES)


def _ffn_body(xn_ref, xp_ref, gpre_ref, gpost_ref, wa_ref, wb_ref, wd_ref, o_ref, h0, h1, acc0, acc1,
              *, nt, nslice):
    r = pl.program_id(0)
    j = pl.program_id(1)
    rs = xn_ref.shape[0]
    base = jnp.minimum(j, nslice - 1) * rs

    def chunk_rows(c):
        return pl.ds(pl.multiple_of(base + c * ROW_CHUNK, ROW_CHUNK), ROW_CHUNK)

    def stages(h_pre, h_main, acc_main, acc_post):
        def pre():
            for c in range(rs // ROW_CHUNK):
                xc = xn_ref[c * ROW_CHUNK:(c + 1) * ROW_CHUNK, :]
                h_pre[chunk_rows(c), :] = _rms(xc, gpre_ref[...]).astype(BF16)

        def main():
            h = h_main[...]
            a = _dot(h, wa_ref[...])
            b = _dot(h, wb_ref[...])
            contrib = _dot((_silu(a) * b).astype(BF16), wd_ref[...])
            acc_main[...] = jnp.where(j == 0, contrib, acc_main[...] + contrib)

        def post():
            for c in range(rs // ROW_CHUNK):
                rows = slice(c * ROW_CHUNK, (c + 1) * ROW_CHUNK)
                o_ref[rows, :] = xp_ref[rows, :] + 0.5 * _rms(acc_post[chunk_rows(c), :], gpost_ref[...])

        steady = jnp.logical_and(r >= 2, r < nt)

        @pl.when(steady)
        def _():
            pre()
            main()
            post()

        @pl.when(jnp.logical_not(steady))
        def _():
            @pl.when(jnp.logical_and(r == 0, j == 0))
            def _():
                acc0[...] = jnp.zeros_like(acc0)
                acc1[...] = jnp.zeros_like(acc1)
            pl.when(r < nt)(pre)
            pl.when(jnp.logical_and(r >= 1, r <= nt))(main)
            pl.when(r >= 2)(post)

    pl.when(r % 2 == 0)(lambda: stages(h0, h1, acc1, acc0))
    pl.when(r % 2 == 1)(lambda: stages(h1, h0, acc0, acc1))


def _ffn(x, g_pre, g_post, w_up, w_down, li, k, *, tm):
    t, d = x.shape
    f = w_down.shape[2]
    tf = FFN_TILE
    nf = f // tf
    nt = t // tm
    nslice = min(nf, tm // ROW_CHUNK)
    while tm % (nslice * ROW_CHUNK):
        nslice -= 1
    rs = tm // nslice
    assert t % tm == 0 and f % tf == 0

    def w_step(r, j):
        return jnp.where(r == 0, 0, jnp.where(r > nt, nf - 1, j))

    def next_slice(r, j):
        return jnp.where(r < nt, r * nslice + jnp.minimum(j, nslice - 1), nt * nslice - 1)

    def prev_slice(r, j):
        return jnp.where(r >= 2, (r - 2) * nslice + jnp.minimum(j, nslice - 1), 0)

    return pl.pallas_call(
        functools.partial(_ffn_body, nt=nt, nslice=nslice),
        out_shape=jax.ShapeDtypeStruct((t, d), F32),
        grid=(nt + 2, nf),
        in_specs=[
            pl.BlockSpec((rs, d), lambda r, j: (next_slice(r, j), 0)),
            pl.BlockSpec((rs, d), lambda r, j: (prev_slice(r, j), 0)),
            pl.BlockSpec((1, d), lambda r, j: (0, 0)),
            pl.BlockSpec((1, d), lambda r, j: (0, 0)),
            pl.BlockSpec((None, None, d, tf), lambda r, j: (li, k, 0, w_step(r, j))),
            pl.BlockSpec((None, None, d, tf), lambda r, j: (li, k, 0, w_step(r, j) + nf)),
            pl.BlockSpec((None, None, tf, d), lambda r, j: (li, k, w_step(r, j), 0)),
        ],
        out_specs=pl.BlockSpec((rs, d), lambda r, j: (prev_slice(r, j), 0)),
        scratch_shapes=[pltpu.VMEM((tm, d), BF16), pltpu.VMEM((tm, d), BF16),
                        pltpu.VMEM((tm, d), F32), pltpu.VMEM((tm, d), F32)],
        compiler_params=_params(),
    )(x, x, g_pre, g_post, w_up, w_up, w_down)


def _sg_mask(seg_mode):
    row = lax.broadcasted_iota(jnp.int32, (SG_CHUNK, SG_CHUNK), 0)
    col = lax.broadcasted_iota(jnp.int32, (SG_CHUNK, SG_CHUNK), 1)
    if seg_mode is None:
        return (row // STREAM_CHUNK) >= (col // STREAM_CHUNK)
    return (row // seg_mode) == (col // seg_mode)


def _sg_body(x_ref, g2_ref, g3_ref, wu_ref, wv_ref, lng_ref, lnb_ref, ws_ref, bias_ref, wo_ref,
             *rest, ng, gpt, nchunk, seg_mode, emit_v):
    if emit_v:
        o_ref, vout_ref, h_scr, u_scr, v_scr, y_scr, acc_scr, rs_scr, mu_scr = rest
    else:
        o_ref, h_scr, u_scr, v_scr, y_scr, acc_scr, rs_scr, mu_scr = rest
        vout_ref = None
    j = pl.program_id(1)
    tn = gpt * LANES
    d = ng * tn
    tm = x_ref.shape[0]

    @pl.when(j == 0)
    def _():
        _pre_norm(x_ref, g2_ref, h_scr, acc_scr, rs_scr)

    @pl.when(j < ng)
    def _():
        h = h_scr[...]
        u_scr[j] = _gelu(_dot(h, wu_ref[...]))
        v_scr[j] = _gelu(_dot(h, wv_ref[...]))

    @pl.when(j == ng)
    def _():
        mu = sum(jnp.sum(v_scr[jj], axis=-1, keepdims=True) for jj in range(ng)) * (1.0 / d)
        var = sum(jnp.sum(jnp.square(v_scr[jj] - mu), axis=-1, keepdims=True)
                  for jj in range(ng)) * (1.0 / d)
        mu_scr[...] = jnp.broadcast_to(mu, (tm, LANES))
        rs_scr[...] = jnp.broadcast_to(lax.rsqrt(var + EPS), (tm, LANES))
        for jj in range(ng):
            def tile(rows, cols, jj=jj):
                gcols = slice(jj * tn + cols.start, jj * tn + cols.stop)
                vn = ((v_scr[jj, rows, cols] - mu_scr[rows, :]) * rs_scr[rows, :] * lng_ref[:, gcols]
                      + lnb_ref[:, gcols])
                v_scr[jj, rows, cols] = vn
                if emit_v:
                    vout_ref[rows, gcols] = vn
            _for_tiles(tm, tn, tile)

    @pl.when(j >= ng)
    def _():
        jj = j - ng
        mask = _sg_mask(seg_mode)
        for g in range(gpt):
            cols = slice(g * LANES, (g + 1) * LANES)
            w = jnp.where(mask, ws_ref[g], 0.0).astype(BF16)
            for n in range(nchunk):
                rows = slice(n * SG_CHUNK, (n + 1) * SG_CHUNK)
                s = _dot(w, v_scr[jj, rows, cols].astype(BF16)) + bias_ref[:, cols]
                y_scr[rows, cols] = (u_scr[jj, rows, cols] * s).astype(BF16)
        acc_scr[...] += _dot(y_scr[...], wo_ref[...])

    @pl.when(j == 2 * ng - 1)
    def _():
        _post_norm(x_ref, g3_ref, acc_scr, rs_scr, o_ref, 1.0)


def _sg_mixer(x, g2, g3, w_in, ln_g, ln_b, w_s, bias_rows, w_out, jm, *, tm, seg_mode, emit_v):
    t, d = x.shape
    tn = SG_TILE
    gpt = tn // LANES
    ng = d // tn
    assert t % tm == 0 and tm % SG_CHUNK == 0 and d % tn == 0 and tm % ROW_CHUNK == 0
    last = ng - 1
    body = functools.partial(_sg_body, ng=ng, gpt=gpt, nchunk=tm // SG_CHUNK,
                             seg_mode=seg_mode, emit_v=emit_v)
    row_spec = pl.BlockSpec((tm, d), lambda i, j: (i, 0))
    vec_spec = pl.BlockSpec((1, d), lambda i, j: (0, 0))
    out_shape = jax.ShapeDtypeStruct((t, d), F32)
    return pl.pallas_call(
        body,
        out_shape=(out_shape, out_shape) if emit_v else out_shape,
        grid=(t // tm, 2 * ng),
        in_specs=[
            row_spec, vec_spec, vec_spec,
            pl.BlockSpec((None, d, tn), lambda i, j: (jm, 0, jnp.minimum(j, last))),
            pl.BlockSpec((None, d, tn), lambda i, j: (jm, 0, ng + jnp.minimum(j, last))),
            vec_spec, vec_spec,
            pl.BlockSpec((gpt, SG_CHUNK, SG_CHUNK), lambda i, j: (jnp.maximum(j - ng, 0), 0, 0)),
            pl.BlockSpec((SG_CHUNK, tn), lambda i, j: (0, jnp.maximum(j - ng, 0))),
            pl.BlockSpec((None, tn, d), lambda i, j: (jm, jnp.maximum(j - ng, 0), 0)),
        ],
        out_specs=(row_spec, row_spec) if emit_v else row_spec,
        scratch_shapes=[
            pltpu.VMEM((tm, d), BF16),
            pltpu.VMEM((ng, tm, tn), F32),
            pltpu.VMEM((ng, tm, tn), F32),
            pltpu.VMEM((tm, tn), BF16),
            pltpu.VMEM((tm, d), F32),
            pltpu.VMEM((tm, LANES), F32),
            pltpu.VMEM((tm, LANES), F32),
        ],
        compiler_params=_params(),
    )(x, g2, g3, w_in, w_in, ln_g, ln_b, w_s, bias_rows, w_out)


def _scan_tables(chunk, seg):
    levels = []
    m = seg // 2
    while m >= 1:
        levels.append(m)
        m //= 2
    nb = 2 + len(levels)
    t = np.arange(chunk)[:, None]
    j = np.arange(chunk)[None, :]
    same_seg = (t // seg) == (j // seg)
    sums = np.zeros((nb, chunk, chunk), np.float32)
    sums[0] = same_seg & (j <= t)
    sums[1] = same_seg & (j > t)
    level_id = np.where(t == j, 0, -1).astype(np.int32)
    for li, m in enumerate(levels):
        g = 2 * m
        ref = (t // g) * g + m - 1
        odd = (t % g) >= m
        sums[2 + li] = np.where(odd, (j > ref) & (j <= t), (j > t) & (j <= ref))
        pair = ((t // g) == (j // g)) & odd & ((j % g) < m)
        level_id = np.where(pair, li + 1, level_id).astype(np.int32)
    sums = sums.reshape(nb * chunk, chunk)
    return np.concatenate([sums, sums, sums], axis=1), level_id, nb


def _split3(x):
    hi = x.astype(BF16)
    r1 = x - hi.astype(F32)
    mid = r1.astype(BF16)
    lo = (r1 - mid.astype(F32)).astype(BF16)
    return jnp.concatenate([hi, mid, lo], axis=0)


def _hgrn_body(x_ref, g2_ref, g3_ref, wq_ref, wf_ref, wi_ref, wg_ref, hl_ref, ng_ref, wo_ref,
               sums_ref, lid_ref, s0_ref, o_ref, sout_ref, *scratch,
               layer, chunk, seg, nh, nb, nchunk, nhg):
    sets = (scratch[0:6], scratch[6:12])
    h_scr, e_scr, st_scr, on_scr, acc_scr, rs_scr, worst_smem = scratch[12:]
    i = pl.program_id(0)
    j = pl.program_id(1)
    nseg = chunk // seg
    nlev = nb - 2
    heads = nhg * nh
    lid = lid_ref[...]

    @pl.when(j == 0)
    def _():
        _pre_norm(x_ref, g2_ref, h_scr, acc_scr, rs_scr)

    @pl.when(jnp.logical_and(i == 0, j == 0))
    def _():
        for s in range(nseg):
            for hd in range(heads):
                st_scr[s, hd] = s0_ref[s, hd].T

    def project(p):
        q_scr, lf_scr, k_scr, v_scr, gate_scr, b_scr = sets[p]
        hl = hl_ref[...]
        e = jnp.exp(hl - jnp.max(hl, axis=0, keepdims=True))
        pr = e / jnp.sum(e, axis=0, keepdims=True)
        lb = jnp.zeros_like(pr[0:1])
        for r in range(1, layer + 1):
            lb = lb + pr[r:r + 1]
        h = h_scr[...]
        q_scr[...] = _silu(_dot(h, wq_ref[...]))
        f = lb + (1.0 - lb) * jax.nn.sigmoid(_dot(h, wf_ref[...]))
        lf = jnp.log(f)
        lf_scr[...] = lf
        k_scr[...] = 1.0 - f
        v_scr[...] = _dot(h, wi_ref[...]).astype(BF16)
        gate_scr[...] = _silu(_dot(h, wg_ref[...]))
        if nseg == 1:
            worst = jnp.float32(0.0)
            for c in range(nchunk):
                rows = slice(c * chunk, (c + 1) * chunk)
                b = _dot(sums_ref[0:chunk, :], _split3(lf[rows, :]))
                b_scr[rows, :] = b
                worst = jnp.maximum(worst, jnp.max(-b[chunk - 1:chunk, :]))
            worst_smem[p] = worst

    def scan(p, split):
        q_scr, lf_scr, k_scr, v_scr, gate_scr, b_scr = sets[p]
        head0 = (j - 1) * nh

        def carry_state(rows, cols, head, o_intra, q_pre, k_end, vh, e_end):
            inter = []
            for s in range(nseg):
                srows = slice(s * seg, (s + 1) * seg)
                st = st_scr[s, head]
                inter.append(_dot_nt(q_pre[srows], st.astype(BF16)))
                st_scr[s, head] = st * e_end[s] + _dot_tn(vh[srows], k_end[srows])
            o = o_intra + (inter[0] if nseg == 1 else jnp.concatenate(inter, axis=0))
            on_scr[rows, cols] = (_rms(o, ng_ref[...]) * gate_scr[rows, cols]).astype(BF16)

        for c in range(nchunk):
            rows = slice(c * chunk, (c + 1) * chunk)
            if split:
                b = b_scr[rows, :]
                half = 0.5 * b[chunk - 1:chunk, :]
                x = b - half
                e_q = jnp.exp(x)
                e_k = jnp.exp(-x)
                e_half = jnp.exp(half)
            else:
                e_scr[...] = jnp.exp(_dot(sums_ref[...], _split3(lf_scr[rows, :])))
            for hh in range(nh):
                cols = slice(hh * HEAD_DIM, (hh + 1) * HEAD_DIM)
                vh = v_scr[rows, cols]
                if split:
                    qt = q_scr[rows, cols] * e_q[:, cols]
                    kt = k_scr[rows, cols] * e_k[:, cols]
                    eh = e_half[:, cols]
                    sc = jnp.where(lid >= 0, _dot_nt(qt.astype(BF16), kt.astype(BF16)), 0.0)
                    carry_state(rows, cols, head0 + hh, _dot(sc.astype(BF16), vh),
                                (qt * eh).astype(BF16), (kt * eh).astype(BF16), vh, [eh * eh])
                else:
                    qh = q_scr[rows, cols]
                    kh = k_scr[rows, cols]
                    e_pre = e_scr[0:chunk, cols]
                    k_end = (kh * e_scr[chunk:2 * chunk, cols]).astype(BF16)
                    sc = jnp.where(lid == 0, _dot_nt(qh.astype(BF16), kh.astype(BF16)), 0.0)
                    for l in range(nlev):
                        el = e_scr[(2 + l) * chunk:(3 + l) * chunk, cols]
                        sc_l = _dot_nt((qh * el).astype(BF16), (kh * el).astype(BF16))
                        sc = jnp.where(lid == l + 1, sc_l, sc)
                    e_end = [e_pre[(s + 1) * seg - 1:(s + 1) * seg, :] for s in range(nseg)]
                    carry_state(rows, cols, head0 + hh, _dot(sc.astype(BF16), vh),
                                (qh * e_pre).astype(BF16), k_end, vh, e_end)
        acc_scr[...] += _dot(on_scr[...], wo_ref[...])

    def both(p, split):
        project(p)
        scan(1 - p, split)

    for p in (0, 1):
        mine = (j % 2) == p
        inner = jnp.logical_and(mine, jnp.logical_and(j >= 1, j < nhg))
        last = jnp.logical_and(mine, j == nhg)
        if p == 0:
            pl.when(j == 0)(functools.partial(project, 0))
        if nseg == 1:
            benign = worst_smem[1 - p] <= MAX_SPLIT_DECAY
            pl.when(jnp.logical_and(inner, benign))(functools.partial(both, p, True))
            pl.when(jnp.logical_and(inner, jnp.logical_not(benign)))(functools.partial(both, p, False))
            if nhg % 2 == p:
                pl.when(jnp.logical_and(last, benign))(functools.partial(scan, 1 - p, True))
                pl.when(jnp.logical_and(last, jnp.logical_not(benign)))(functools.partial(scan, 1 - p, False))
        else:
            pl.when(inner)(functools.partial(both, p, False))
            if nhg % 2 == p:
                pl.when(last)(functools.partial(scan, 1 - p, False))

    @pl.when(j == nhg)
    def _():
        _post_norm(x_ref, g3_ref, acc_scr, rs_scr, o_ref, 1.0)

    @pl.when(jnp.logical_and(i == pl.num_programs(0) - 1, j == nhg))
    def _():
        for s in range(nseg):
            for hd in range(heads):
                sout_ref[s, hd] = st_scr[s, hd].T


def _hgrn_mixer(x, g2, g3, w_in, hg_lower, norm_g, w_out, s0, jm, *, layer, tm, chunk, seg):
    t, d = x.shape
    heads = d // HEAD_DIM
    nseg = chunk // seg
    tn = HG_TILE
    nh = tn // HEAD_DIM
    nhg = heads // nh
    assert t % tm == 0 and tm % chunk == 0 and heads % nh == 0 and tm % ROW_CHUNK == 0
    assert s0.shape == (nseg, heads, HEAD_DIM, HEAD_DIM)
    assert nseg == 1 or t == chunk
    sums, level_id, nb = _scan_tables(chunk, seg)
    body = functools.partial(_hgrn_body, layer=layer, chunk=chunk, seg=seg, nh=nh, nb=nb,
                             nchunk=tm // chunk, nhg=nhg)
    row_spec = pl.BlockSpec((tm, d), lambda i, j: (i, 0))
    vec_spec = pl.BlockSpec((1, d), lambda i, j: (0, 0))
    nsec = d // tn
    state_spec = pl.BlockSpec(s0.shape, lambda i, j: (0, 0, 0, 0))

    def proj(j):
        return jnp.minimum(j, nhg - 1)

    def outp(j):
        return jnp.maximum(j - 1, 0)

    scratch_set = [
        pltpu.VMEM((tm, tn), F32),
        pltpu.VMEM((tm, tn), F32),
        pltpu.VMEM((tm, tn), F32),
        pltpu.VMEM((tm, tn), BF16),
        pltpu.VMEM((tm, tn), F32),
        pltpu.VMEM((tm, tn), F32),
    ]
    return pl.pallas_call(
        body,
        out_shape=(jax.ShapeDtypeStruct((t, d), F32), jax.ShapeDtypeStruct(s0.shape, F32)),
        grid=(t // tm, nhg + 1),
        in_specs=[
            row_spec, vec_spec, vec_spec,
            pl.BlockSpec((None, d, tn), lambda i, j: (jm, 0, proj(j))),
            pl.BlockSpec((None, d, tn), lambda i, j: (jm, 0, nsec + proj(j))),
            pl.BlockSpec((None, d, tn), lambda i, j: (jm, 0, 2 * nsec + proj(j))),
            pl.BlockSpec((None, d, tn), lambda i, j: (jm, 0, 3 * nsec + proj(j))),
            pl.BlockSpec((hg_lower.shape[0], tn), lambda i, j: (0, proj(j))),
            pl.BlockSpec((1, HEAD_DIM), lambda i, j: (0, 0)),
            pl.BlockSpec((None, tn, d), lambda i, j: (jm, outp(j), 0)),
            pl.BlockSpec(sums.shape, lambda i, j: (0, 0)),
            pl.BlockSpec(level_id.shape, lambda i, j: (0, 0)),
            state_spec,
        ],
        out_specs=(row_spec, state_spec),
        scratch_shapes=scratch_set + scratch_set + [
            pltpu.VMEM((tm, d), BF16),
            pltpu.VMEM((nb * chunk, tn), F32),
            pltpu.VMEM((nseg, heads, HEAD_DIM, HEAD_DIM), F32),
            pltpu.VMEM((tm, tn), BF16),
            pltpu.VMEM((tm, d), F32),
            pltpu.VMEM((tm, LANES), F32),
            pltpu.SMEM((2,), F32),
        ],
        compiler_params=_params(),
    )(x, g2, g3, w_in, w_in, w_in, w_in, hg_lower, norm_g, w_out,
      jnp.asarray(sums, BF16), jnp.asarray(level_id), s0)


def _trunk(x, hg_init, p, *, tm, ffn_tm, sample):
    depth = p["norm_g"].shape[0]
    n_mix = 2
    hg_states, sg_vs = [], []
    for i in range(depth):
        g = p["norm_g"][i]
        row = lambda r: g[r:r + 1]
        jm = i // n_mix
        x = _ffn(x, row(0), row(1), p["ffn_w_up"], p["ffn_w_down"], i, 0, tm=ffn_tm)
        if i % n_mix == 0:
            w_s, b_s = p["sg_w_s"][jm], p["sg_b_s"][jm]
            if sample is not None:
                n_streams, seq = sample
                w_s = jnp.tile(w_s[:, :seq, :seq], (1, n_streams, n_streams))
                b_s = jnp.tile(b_s[:, :seq], (1, n_streams))
            bias_rows = jnp.repeat(b_s.T, LANES, axis=1)
            res = _sg_mixer(x, row(2), row(3), p["sg_w_in"], p["sg_ln_g"][jm:jm + 1],
                            p["sg_ln_b"][jm:jm + 1], w_s, bias_rows, p["sg_w_out"], jm,
                            tm=tm, seg_mode=None if sample is None else sample[1],
                            emit_v=sample is not None)
            if sample is not None:
                x, v_rows = res
                sg_vs.append(v_rows)
            else:
                x = res
        else:
            x, st = _hgrn_mixer(x, row(2), row(3), p["hg_w_in"], p["hg_lower"],
                                p["hg_norm_g"][jm:jm + 1], p["hg_w_out"], hg_init[jm], jm,
                                layer=i, tm=tm, chunk=SG_CHUNK if sample is None else x.shape[0],
                                seg=SG_CHUNK if sample is None else sample[1])
            hg_states.append(st)
        x = _ffn(x, row(4), row(5), p["ffn_w_up"], p["ffn_w_down"], i, 1, tm=ffn_tm)
    return x, hg_states, sg_vs


def kernel(x_prompt, x_sample, state_hgrn, norm_g, ffn_w_up, ffn_w_down, sg_w_in, sg_ln_g, sg_ln_b,
           sg_w_s, sg_b_s, sg_w_out, hg_w_in, hg_lower, hg_norm_g, hg_w_out):
    batch, seq, d = x_prompt.shape
    dec_batch, dec_seq, _ = x_sample.shape
    n_hg = hg_w_in.shape[0]
    heads = d // HEAD_DIM
    p = dict(norm_g=norm_g, ffn_w_up=ffn_w_up.astype(BF16), ffn_w_down=ffn_w_down.astype(BF16),
             sg_w_in=sg_w_in.astype(BF16), sg_ln_g=sg_ln_g, sg_ln_b=sg_ln_b, sg_w_s=sg_w_s,
             sg_b_s=sg_b_s, sg_w_out=sg_w_out.astype(BF16), hg_w_in=hg_w_in.astype(BF16),
             hg_lower=hg_lower, hg_norm_g=hg_norm_g, hg_w_out=hg_w_out.astype(BF16))

    ys, sts = [], []
    for b in range(batch):
        init = [jnp.zeros((1, heads, HEAD_DIM, HEAD_DIM), F32) for _ in range(n_hg)]
        y, hg_p, _ = _trunk(x_prompt[b], init, p, tm=512, ffn_tm=1024, sample=None)
        ys.append(y)
        sts.append(jnp.stack([s[0] for s in hg_p], axis=0))
    y_prompt = jnp.stack(ys, axis=0)
    state_hgrn_prompt = jnp.stack(sts, axis=1)

    init = [state_hgrn[jm].astype(F32) for jm in range(n_hg)]
    y, hg_s, sg_v = _trunk(x_sample.reshape(dec_batch * dec_seq, d), init, p,
                           tm=dec_batch * dec_seq, ffn_tm=dec_batch * dec_seq, sample=(dec_batch, dec_seq))
    y_sample = y.reshape(dec_batch, dec_seq, d)
    state_hgrn_sample = jnp.stack(hg_s, axis=0)
    state_sg_v_sample = jnp.stack([v.reshape(dec_batch, dec_seq, d) for v in sg_v], axis=0)
    return (y_prompt, y_sample, state_hgrn_prompt, state_hgrn_sample, state_sg_v_sample)
```

```python
import functools

import numpy as np
import jax
import jax.numpy as jnp
from jax import lax
from jax.experimental import pallas as pl
from jax.experimental.pallas import tpu as pltpu

EPS = 1e-6
F32 = jnp.float32
BF16 = jnp.bfloat16

LANES = 128
ROW_CHUNK = 16
HEAD_DIM = 128
SG_CHUNK = 128
FFN_TILE = 512
SG_TILE = 512
HG_TILE = 256
STREAM_CHUNK = 64
V7X_VMEM_BYTES = 64 * 1024 * 1024
VMEM_LIMIT_BYTES = V7X_VMEM_BYTES - 8 * 1024 * 1024
MAX_SPLIT_DECAY = 140.0


def _dot(a, b):
    return jnp.dot(a, b, preferred_element_type=F32)


def _dot_nt(a, b):
    return lax.dot_general(a, b, (((1,), (1,)), ((), ())), preferred_element_type=F32)


def _dot_tn(a, b):
    return lax.dot_general(a, b, (((0,), (0,)), ((), ())), preferred_element_type=F32)


def _rms(x, g):
    return x * lax.rsqrt(jnp.mean(x * x, axis=-1, keepdims=True) + EPS) * g


def _silu(x):
    return x * jax.nn.sigmoid(x)


def _gelu(x):
    return 0.5 * x * (1.0 + lax.erf(x * np.float32(np.sqrt(0.5))))


def _for_tiles(n_rows, n_cols, fn):
    def step(c, carry):
        rows = pl.ds(pl.multiple_of(c * ROW_CHUNK, ROW_CHUNK), ROW_CHUNK)
        for ct in range(n_cols // LANES):
            fn(rows, slice(ct * LANES, (ct + 1) * LANES))
        return carry
    lax.fori_loop(0, n_rows // ROW_CHUNK, step, 0, unroll=max(2, 32 * LANES // n_cols))


def _row_rsqrt(x):
    r = lax.rsqrt(jnp.mean(x * x, axis=-1, keepdims=True) + EPS)
    return jnp.broadcast_to(r, (x.shape[0], LANES))


def _pre_norm(x_ref, g_ref, h_scr, acc_scr, rs_scr):
    rs_scr[...] = _row_rsqrt(x_ref[...])

    def tile(rows, cols):
        h_scr[rows, cols] = (x_ref[rows, cols] * rs_scr[rows, :] * g_ref[:, cols]).astype(BF16)
        acc_scr[rows, cols] = jnp.zeros((ROW_CHUNK, LANES), F32)
    _for_tiles(x_ref.shape[0], x_ref.shape[1], tile)


def _post_norm(x_ref, g_ref, acc_scr, rs_scr, o_ref, scale):
    rs_scr[...] = _row_rsqrt(acc_scr[...])

    def tile(rows, cols):
        o_ref[rows, cols] = x_ref[rows, cols] + (acc_scr[rows, cols] * rs_scr[rows, :]) * (scale * g_ref[:, cols])
    _for_tiles(x_ref.shape[0], x_ref.shape[1], tile)


def _params():
    return pltpu.CompilerParams(dimension_semantics=("arbitrary", "arbitrary"),
                                vmem_limit_bytes=VMEM_LIMIT_BYTES)


def _ffn_body(xn_ref, xp_ref, gpre_ref, gpost_ref, wa_ref, wb_ref, wd_ref, o_ref, h0, h1, acc0, acc1,
              *, nt, nslice):
    r = pl.program_id(0)
    j = pl.program_id(1)
    rs = xn_ref.shape[0]
    base = jnp.minimum(j, nslice - 1) * rs

    def chunk_rows(c):
        return pl.ds(pl.multiple_of(base + c * ROW_CHUNK, ROW_CHUNK), ROW_CHUNK)

    def stages(h_pre, h_main, acc_main, acc_post):
        def pre():
            for c in range(rs // ROW_CHUNK):
                xc = xn_ref[c * ROW_CHUNK:(c + 1) * ROW_CHUNK, :]
                h_pre[chunk_rows(c), :] = _rms(xc, gpre_ref[...]).astype(BF16)

        def main():
            h = h_main[...]
            a = _dot(h, wa_ref[...])
            b = _dot(h, wb_ref[...])
            contrib = _dot((_silu(a) * b).astype(BF16), wd_ref[...])
            acc_main[...] = jnp.where(j == 0, contrib, acc_main[...] + contrib)

        def post():
            for c in range(rs // ROW_CHUNK):
                rows = slice(c * ROW_CHUNK, (c + 1) * ROW_CHUNK)
                o_ref[rows, :] = xp_ref[rows, :] + 0.5 * _rms(acc_post[chunk_rows(c), :], gpost_ref[...])

        steady = jnp.logical_and(r >= 2, r < nt)

        @pl.when(steady)
        def _():
            pre()
            main()
            post()

        @pl.when(jnp.logical_not(steady))
        def _():
            @pl.when(jnp.logical_and(r == 0, j == 0))
            def _():
                acc0[...] = jnp.zeros_like(acc0)
                acc1[...] = jnp.zeros_like(acc1)
            pl.when(r < nt)(pre)
            pl.when(jnp.logical_and(r >= 1, r <= nt))(main)
            pl.when(r >= 2)(post)

    pl.when(r % 2 == 0)(lambda: stages(h0, h1, acc1, acc0))
    pl.when(r % 2 == 1)(lambda: stages(h1, h0, acc0, acc1))


def _ffn(x, g_pre, g_post, w_up, w_down, li, k, *, tm):
    t, d = x.shape
    f = w_down.shape[2]
    tf = w_up.shape[4]
    nf = f // tf
    nt = t // tm
    nslice = min(nf, tm // ROW_CHUNK)
    while tm % (nslice * ROW_CHUNK):
        nslice -= 1
    rs = tm // nslice
    assert t % tm == 0 and f % tf == 0

    def w_step(r, j):
        return jnp.where(r == 0, 0, jnp.where(r > nt, nf - 1, j))

    def next_slice(r, j):
        return jnp.where(r < nt, r * nslice + jnp.minimum(j, nslice - 1), nt * nslice - 1)

    def prev_slice(r, j):
        return jnp.where(r >= 2, (r - 2) * nslice + jnp.minimum(j, nslice - 1), 0)

    return pl.pallas_call(
        functools.partial(_ffn_body, nt=nt, nslice=nslice),
        out_shape=jax.ShapeDtypeStruct((t, d), F32),
        grid=(nt + 2, nf),
        in_specs=[
            pl.BlockSpec((rs, d), lambda r, j: (next_slice(r, j), 0)),
            pl.BlockSpec((rs, d), lambda r, j: (prev_slice(r, j), 0)),
            pl.BlockSpec((1, d), lambda r, j: (0, 0)),
            pl.BlockSpec((1, d), lambda r, j: (0, 0)),
            pl.BlockSpec((None, None, None, d, tf), lambda r, j: (li, k, w_step(r, j), 0, 0)),
            pl.BlockSpec((None, None, None, d, tf), lambda r, j: (li, k, w_step(r, j) + nf, 0, 0)),
            pl.BlockSpec((None, None, tf, d), lambda r, j: (li, k, w_step(r, j), 0)),
        ],
        out_specs=pl.BlockSpec((rs, d), lambda r, j: (prev_slice(r, j), 0)),
        scratch_shapes=[pltpu.VMEM((tm, d), BF16), pltpu.VMEM((tm, d), BF16),
                        pltpu.VMEM((tm, d), F32), pltpu.VMEM((tm, d), F32)],
        compiler_params=_params(),
    )(x, x, g_pre, g_post, w_up, w_up, w_down)


def _sg_mask(seg_mode):
    row = lax.broadcasted_iota(jnp.int32, (SG_CHUNK, SG_CHUNK), 0)
    col = lax.broadcasted_iota(jnp.int32, (SG_CHUNK, SG_CHUNK), 1)
    if seg_mode is None:
        return (row // STREAM_CHUNK) >= (col // STREAM_CHUNK)
    return (row // seg_mode) == (col // seg_mode)


def _sg_body(x_ref, g2_ref, g3_ref, wu_ref, wv_ref, lng_ref, lnb_ref, ws_ref, bias_ref, wo_ref,
             *rest, ng, gpt, nchunk, seg_mode, emit_v):
    if emit_v:
        o_ref, vout_ref, h_scr, u_scr, v_scr, y_scr, acc_scr, rs_scr, mu_scr = rest
    else:
        o_ref, h_scr, u_scr, v_scr, y_scr, acc_scr, rs_scr, mu_scr = rest
        vout_ref = None
    j = pl.program_id(1)
    tn = gpt * LANES
    d = ng * tn
    tm = x_ref.shape[0]

    @pl.when(j == 0)
    def _():
        _pre_norm(x_ref, g2_ref, h_scr, acc_scr, rs_scr)

    @pl.when(j < ng)
    def _():
        h = h_scr[...]
        u_scr[j] = _gelu(_dot(h, wu_ref[...]))
        v_scr[j] = _gelu(_dot(h, wv_ref[...]))

    @pl.when(j == ng)
    def _():
        mu = sum(jnp.sum(v_scr[jj], axis=-1, keepdims=True) for jj in range(ng)) * (1.0 / d)
        var = sum(jnp.sum(jnp.square(v_scr[jj] - mu), axis=-1, keepdims=True)
                  for jj in range(ng)) * (1.0 / d)
        mu_scr[...] = jnp.broadcast_to(mu, (tm, LANES))
        rs_scr[...] = jnp.broadcast_to(lax.rsqrt(var + EPS), (tm, LANES))
        for jj in range(ng):
            def tile(rows, cols, jj=jj):
                gcols = slice(jj * tn + cols.start, jj * tn + cols.stop)
                vn = ((v_scr[jj, rows, cols] - mu_scr[rows, :]) * rs_scr[rows, :] * lng_ref[:, gcols]
                      + lnb_ref[:, gcols])
                v_scr[jj, rows, cols] = vn
                if emit_v:
                    vout_ref[rows, gcols] = vn
            _for_tiles(tm, tn, tile)

    @pl.when(j >= ng)
    def _():
        jj = j - ng
        mask = _sg_mask(seg_mode)
        for g in range(gpt):
            cols = slice(g * LANES, (g + 1) * LANES)
            w = jnp.where(mask, ws_ref[g], 0.0).astype(BF16)
            for n in range(nchunk):
                rows = slice(n * SG_CHUNK, (n + 1) * SG_CHUNK)
                s = _dot(w, v_scr[jj, rows, cols].astype(BF16)) + bias_ref[:, cols]
                y_scr[rows, cols] = (u_scr[jj, rows, cols] * s).astype(BF16)
        acc_scr[...] += _dot(y_scr[...], wo_ref[...])

    @pl.when(j == 2 * ng - 1)
    def _():
        _post_norm(x_ref, g3_ref, acc_scr, rs_scr, o_ref, 1.0)


def _sg_mixer(x, g2, g3, w_in, ln_g, ln_b, w_s, bias_rows, w_out, jm, *, tm, seg_mode, emit_v):
    t, d = x.shape
    tn = w_in.shape[3]
    gpt = tn // LANES
    ng = d // tn
    assert t % tm == 0 and tm % SG_CHUNK == 0 and d % tn == 0 and tm % ROW_CHUNK == 0
    last = ng - 1
    body = functools.partial(_sg_body, ng=ng, gpt=gpt, nchunk=tm // SG_CHUNK,
                             seg_mode=seg_mode, emit_v=emit_v)
    row_spec = pl.BlockSpec((tm, d), lambda i, j: (i, 0))
    vec_spec = pl.BlockSpec((1, d), lambda i, j: (0, 0))
    out_shape = jax.ShapeDtypeStruct((t, d), F32)
    return pl.pallas_call(
        body,
        out_shape=(out_shape, out_shape) if emit_v else out_shape,
        grid=(t // tm, 2 * ng),
        in_specs=[
            row_spec, vec_spec, vec_spec,
            pl.BlockSpec((None, None, d, tn), lambda i, j: (jm, jnp.minimum(j, last), 0, 0)),
            pl.BlockSpec((None, None, d, tn), lambda i, j: (jm, ng + jnp.minimum(j, last), 0, 0)),
            vec_spec, vec_spec,
            pl.BlockSpec((gpt, SG_CHUNK, SG_CHUNK), lambda i, j: (jnp.maximum(j - ng, 0), 0, 0)),
            pl.BlockSpec((SG_CHUNK, tn), lambda i, j: (0, jnp.maximum(j - ng, 0))),
            pl.BlockSpec((None, tn, d), lambda i, j: (jm, jnp.maximum(j - ng, 0), 0)),
        ],
        out_specs=(row_spec, row_spec) if emit_v else row_spec,
        scratch_shapes=[
            pltpu.VMEM((tm, d), BF16),
            pltpu.VMEM((ng, tm, tn), F32),
            pltpu.VMEM((ng, tm, tn), F32),
            pltpu.VMEM((tm, tn), BF16),
            pltpu.VMEM((tm, d), F32),
            pltpu.VMEM((tm, LANES), F32),
            pltpu.VMEM((tm, LANES), F32),
        ],
        compiler_params=_params(),
    )(x, g2, g3, w_in, w_in, ln_g, ln_b, w_s, bias_rows, w_out)


def _scan_tables(chunk, seg):
    levels = []
    m = seg // 2
    while m >= 1:
        levels.append(m)
        m //= 2
    nb = 2 + len(levels)
    t = np.arange(chunk)[:, None]
    j = np.arange(chunk)[None, :]
    same_seg = (t // seg) == (j // seg)
    sums = np.zeros((nb, chunk, chunk), np.float32)
    sums[0] = same_seg & (j <= t)
    sums[1] = same_seg & (j > t)
    level_id = np.where(t == j, 0, -1).astype(np.int32)
    for li, m in enumerate(levels):
        g = 2 * m
        ref = (t // g) * g + m - 1
        odd = (t % g) >= m
        sums[2 + li] = np.where(odd, (j > ref) & (j <= t), (j > t) & (j <= ref))
        pair = ((t // g) == (j // g)) & odd & ((j % g) < m)
        level_id = np.where(pair, li + 1, level_id).astype(np.int32)
    sums = sums.reshape(nb * chunk, chunk)
    return np.concatenate([sums, sums, sums], axis=1), level_id, nb


def _split3(x):
    hi = x.astype(BF16)
    r1 = x - hi.astype(F32)
    mid = r1.astype(BF16)
    lo = (r1 - mid.astype(F32)).astype(BF16)
    return jnp.concatenate([hi, mid, lo], axis=0)


def _hgrn_body(x_ref, g2_ref, g3_ref, wq_ref, wf_ref, wi_ref, wg_ref, hl_ref, ng_ref, wo_ref,
               sums_ref, lid_ref, s0_ref, o_ref, sout_ref, *scratch,
               layer, chunk, seg, nh, nb, nchunk, nhg):
    sets = (scratch[0:6], scratch[6:12])
    h_scr, e_scr, st_scr, on_scr, acc_scr, rs_scr, worst_smem = scratch[12:]
    i = pl.program_id(0)
    j = pl.program_id(1)
    nseg = chunk // seg
    nlev = nb - 2
    heads = nhg * nh
    lid = lid_ref[...]

    @pl.when(j == 0)
    def _():
        _pre_norm(x_ref, g2_ref, h_scr, acc_scr, rs_scr)

    @pl.when(jnp.logical_and(i == 0, j == 0))
    def _():
        for s in range(nseg):
            for hd in range(heads):
                st_scr[s, hd] = s0_ref[s, hd].T

    def project(p):
        q_scr, lf_scr, k_scr, v_scr, gate_scr, b_scr = sets[p]
        hl = hl_ref[...]
        e = jnp.exp(hl - jnp.max(hl, axis=0, keepdims=True))
        pr = e / jnp.sum(e, axis=0, keepdims=True)
        lb = jnp.zeros_like(pr[0:1])
        for r in range(1, layer + 1):
            lb = lb + pr[r:r + 1]
        h = h_scr[...]
        q_scr[...] = _silu(_dot(h, wq_ref[...]))
        f = lb + (1.0 - lb) * jax.nn.sigmoid(_dot(h, wf_ref[...]))
        lf = jnp.log(f)
        lf_scr[...] = lf
        k_scr[...] = 1.0 - f
        v_scr[...] = _dot(h, wi_ref[...]).astype(BF16)
        gate_scr[...] = _silu(_dot(h, wg_ref[...]))
        if nseg == 1:
            worst = jnp.float32(0.0)
            for c in range(nchunk):
                rows = slice(c * chunk, (c + 1) * chunk)
                b = _dot(sums_ref[0:chunk, :], _split3(lf[rows, :]))
                b_scr[rows, :] = b
                worst = jnp.maximum(worst, jnp.max(-b[chunk - 1:chunk, :]))
            worst_smem[p] = worst

    def scan(p, split):
        q_scr, lf_scr, k_scr, v_scr, gate_scr, b_scr = sets[p]
        head0 = (j - 1) * nh

        def carry_state(rows, cols, head, o_intra, q_pre, k_end, vh, e_end):
            inter = []
            for s in range(nseg):
                srows = slice(s * seg, (s + 1) * seg)
                st = st_scr[s, head]
                inter.append(_dot_nt(q_pre[srows], st.astype(BF16)))
                st_scr[s, head] = st * e_end[s] + _dot_tn(vh[srows], k_end[srows])
            o = o_intra + (inter[0] if nseg == 1 else jnp.concatenate(inter, axis=0))
            on_scr[rows, cols] = (_rms(o, ng_ref[...]) * gate_scr[rows, cols]).astype(BF16)

        for c in range(nchunk):
            rows = slice(c * chunk, (c + 1) * chunk)
            if split:
                b = b_scr[rows, :]
                half = 0.5 * b[chunk - 1:chunk, :]
                x = b - half
                e_q = jnp.exp(x)
                e_k = jnp.exp(-x)
                e_half = jnp.exp(half)
            else:
                e_scr[...] = jnp.exp(_dot(sums_ref[...], _split3(lf_scr[rows, :])))
            for hh in range(nh):
                cols = slice(hh * HEAD_DIM, (hh + 1) * HEAD_DIM)
                vh = v_scr[rows, cols]
                if split:
                    qt = q_scr[rows, cols] * e_q[:, cols]
                    kt = k_scr[rows, cols] * e_k[:, cols]
                    eh = e_half[:, cols]
                    sc = jnp.where(lid >= 0, _dot_nt(qt.astype(BF16), kt.astype(BF16)), 0.0)
                    carry_state(rows, cols, head0 + hh, _dot(sc.astype(BF16), vh),
                                (qt * eh).astype(BF16), (kt * eh).astype(BF16), vh, [eh * eh])
                else:
                    qh = q_scr[rows, cols]
                    kh = k_scr[rows, cols]
                    e_pre = e_scr[0:chunk, cols]
                    k_end = (kh * e_scr[chunk:2 * chunk, cols]).astype(BF16)
                    sc = jnp.where(lid == 0, _dot_nt(qh.astype(BF16), kh.astype(BF16)), 0.0)
                    for l in range(nlev):
                        el = e_scr[(2 + l) * chunk:(3 + l) * chunk, cols]
                        sc_l = _dot_nt((qh * el).astype(BF16), (kh * el).astype(BF16))
                        sc = jnp.where(lid == l + 1, sc_l, sc)
                    e_end = [e_pre[(s + 1) * seg - 1:(s + 1) * seg, :] for s in range(nseg)]
                    carry_state(rows, cols, head0 + hh, _dot(sc.astype(BF16), vh),
                                (qh * e_pre).astype(BF16), k_end, vh, e_end)
        acc_scr[...] += _dot(on_scr[...], wo_ref[...])

    def both(p, split):
        project(p)
        scan(1 - p, split)

    for p in (0, 1):
        mine = (j % 2) == p
        inner = jnp.logical_and(mine, jnp.logical_and(j >= 1, j < nhg))
        last = jnp.logical_and(mine, j == nhg)
        if p == 0:
            pl.when(j == 0)(functools.partial(project, 0))
        if nseg == 1:
            benign = worst_smem[1 - p] <= MAX_SPLIT_DECAY
            pl.when(jnp.logical_and(inner, benign))(functools.partial(both, p, True))
            pl.when(jnp.logical_and(inner, jnp.logical_not(benign)))(functools.partial(both, p, False))
            if nhg % 2 == p:
                pl.when(jnp.logical_and(last, benign))(functools.partial(scan, 1 - p, True))
                pl.when(jnp.logical_and(last, jnp.logical_not(benign)))(functools.partial(scan, 1 - p, False))
        else:
            pl.when(inner)(functools.partial(both, p, False))
            if nhg % 2 == p:
                pl.when(last)(functools.partial(scan, 1 - p, False))

    @pl.when(j == nhg)
    def _():
        _post_norm(x_ref, g3_ref, acc_scr, rs_scr, o_ref, 1.0)

    @pl.when(jnp.logical_and(i == pl.num_programs(0) - 1, j == nhg))
    def _():
        for s in range(nseg):
            for hd in range(heads):
                sout_ref[s, hd] = st_scr[s, hd].T


def _hgrn_mixer(x, g2, g3, w_in, hg_lower, norm_g, w_out, s0, jm, *, layer, tm, chunk, seg):
    t, d = x.shape
    heads = d // HEAD_DIM
    nseg = chunk // seg
    tn = w_in.shape[3]
    nh = tn // HEAD_DIM
    nhg = heads // nh
    assert t % tm == 0 and tm % chunk == 0 and heads % nh == 0 and tm % ROW_CHUNK == 0
    assert s0.shape == (nseg, heads, HEAD_DIM, HEAD_DIM)
    assert nseg == 1 or t == chunk
    sums, level_id, nb = _scan_tables(chunk, seg)
    body = functools.partial(_hgrn_body, layer=layer, chunk=chunk, seg=seg, nh=nh, nb=nb,
                             nchunk=tm // chunk, nhg=nhg)
    row_spec = pl.BlockSpec((tm, d), lambda i, j: (i, 0))
    vec_spec = pl.BlockSpec((1, d), lambda i, j: (0, 0))
    nsec = d // tn
    state_spec = pl.BlockSpec(s0.shape, lambda i, j: (0, 0, 0, 0))

    def proj(j):
        return jnp.minimum(j, nhg - 1)

    def outp(j):
        return jnp.maximum(j - 1, 0)

    scratch_set = [
        pltpu.VMEM((tm, tn), F32),
        pltpu.VMEM((tm, tn), F32),
        pltpu.VMEM((tm, tn), F32),
        pltpu.VMEM((tm, tn), BF16),
        pltpu.VMEM((tm, tn), F32),
        pltpu.VMEM((tm, tn), F32),
    ]
    return pl.pallas_call(
        body,
        out_shape=(jax.ShapeDtypeStruct((t, d), F32), jax.ShapeDtypeStruct(s0.shape, F32)),
        grid=(t // tm, nhg + 1),
        in_specs=[
            row_spec, vec_spec, vec_spec,
            pl.BlockSpec((None, None, d, tn), lambda i, j: (jm, proj(j), 0, 0)),
            pl.BlockSpec((None, None, d, tn), lambda i, j: (jm, nsec + proj(j), 0, 0)),
            pl.BlockSpec((None, None, d, tn), lambda i, j: (jm, 2 * nsec + proj(j), 0, 0)),
            pl.BlockSpec((None, None, d, tn), lambda i, j: (jm, 3 * nsec + proj(j), 0, 0)),
            pl.BlockSpec((hg_lower.shape[0], tn), lambda i, j: (0, proj(j))),
            pl.BlockSpec((1, HEAD_DIM), lambda i, j: (0, 0)),
            pl.BlockSpec((None, tn, d), lambda i, j: (jm, outp(j), 0)),
            pl.BlockSpec(sums.shape, lambda i, j: (0, 0)),
            pl.BlockSpec(level_id.shape, lambda i, j: (0, 0)),
            state_spec,
        ],
        out_specs=(row_spec, state_spec),
        scratch_shapes=scratch_set + scratch_set + [
            pltpu.VMEM((tm, d), BF16),
            pltpu.VMEM((nb * chunk, tn), F32),
            pltpu.VMEM((nseg, heads, HEAD_DIM, HEAD_DIM), F32),
            pltpu.VMEM((tm, tn), BF16),
            pltpu.VMEM((tm, d), F32),
            pltpu.VMEM((tm, LANES), F32),
            pltpu.SMEM((2,), F32),
        ],
        compiler_params=_params(),
    )(x, g2, g3, w_in, w_in, w_in, w_in, hg_lower, norm_g, w_out,
      jnp.asarray(sums, BF16), jnp.asarray(level_id), s0)


def _column_tiles_body(w_ref, o_ref):
    per, _, tn = o_ref.shape
    for t in range(per):
        o_ref[t] = w_ref[:, t * tn:(t + 1) * tn].astype(BF16)


def _column_tiles(w, tn, per):
    *lead, d, n = w.shape
    nl = int(np.prod(lead))
    assert n % (tn * per) == 0
    out = pl.pallas_call(
        _column_tiles_body,
        out_shape=jax.ShapeDtypeStruct((nl, n // tn, d, tn), BF16),
        grid=(nl, n // (tn * per)),
        in_specs=[pl.BlockSpec((None, d, tn * per), lambda l, b: (l, 0, b))],
        out_specs=pl.BlockSpec((None, per, d, tn), lambda l, b: (l, b, 0, 0)),
        compiler_params=_params(),
    )(w.reshape(nl, d, n))
    return out.reshape(*lead, n // tn, d, tn)


def _trunk(x, hg_init, p, *, tm, ffn_tm, sample):
    depth = p["norm_g"].shape[0]
    n_mix = 2
    hg_states, sg_vs = [], []
    for i in range(depth):
        g = p["norm_g"][i]
        row = lambda r: g[r:r + 1]
        jm = i // n_mix
        x = _ffn(x, row(0), row(1), p["ffn_w_up"], p["ffn_w_down"], i, 0, tm=ffn_tm)
        if i % n_mix == 0:
            w_s, b_s = p["sg_w_s"][jm], p["sg_b_s"][jm]
            if sample is not None:
                n_streams, seq = sample
                w_s = jnp.tile(w_s[:, :seq, :seq], (1, n_streams, n_streams))
                b_s = jnp.tile(b_s[:, :seq], (1, n_streams))
            bias_rows = jnp.repeat(b_s.T, LANES, axis=1)
            res = _sg_mixer(x, row(2), row(3), p["sg_w_in"], p["sg_ln_g"][jm:jm + 1],
                            p["sg_ln_b"][jm:jm + 1], w_s, bias_rows, p["sg_w_out"], jm,
                            tm=tm, seg_mode=None if sample is None else sample[1],
                            emit_v=sample is not None)
            if sample is not None:
                x, v_rows = res
                sg_vs.append(v_rows)
            else:
                x = res
        else:
            x, st = _hgrn_mixer(x, row(2), row(3), p["hg_w_in"], p["hg_lower"],
                                p["hg_norm_g"][jm:jm + 1], p["hg_w_out"], hg_init[jm], jm,
                                layer=i, tm=tm, chunk=SG_CHUNK if sample is None else x.shape[0],
                                seg=SG_CHUNK if sample is None else sample[1])
            hg_states.append(st)
        x = _ffn(x, row(4), row(5), p["ffn_w_up"], p["ffn_w_down"], i, 1, tm=ffn_tm)
    return x, hg_states, sg_vs


def kernel(x_prompt, x_sample, state_hgrn, norm_g, ffn_w_up, ffn_w_down, sg_w_in, sg_ln_g, sg_ln_b,
           sg_w_s, sg_b_s, sg_w_out, hg_w_in, hg_lower, hg_norm_g, hg_w_out):
    batch, seq, d = x_prompt.shape
    dec_batch, dec_seq, _ = x_sample.shape
    n_hg = hg_w_in.shape[0]
    heads = d // HEAD_DIM
    p = dict(norm_g=norm_g, ffn_w_up=_column_tiles(ffn_w_up, FFN_TILE, 2), ffn_w_down=ffn_w_down.astype(BF16),
             sg_w_in=_column_tiles(sg_w_in, SG_TILE, 2), sg_ln_g=sg_ln_g, sg_ln_b=sg_ln_b, sg_w_s=sg_w_s,
             sg_b_s=sg_b_s, sg_w_out=sg_w_out.astype(BF16), hg_w_in=_column_tiles(hg_w_in, HG_TILE, 4),
             hg_lower=hg_lower, hg_norm_g=hg_norm_g, hg_w_out=hg_w_out.astype(BF16))

    ys, sts = [], []
    for b in range(batch):
        init = [jnp.zeros((1, heads, HEAD_DIM, HEAD_DIM), F32) for _ in range(n_hg)]
        y, hg_p, _ = _trunk(x_prompt[b], init, p, tm=512, ffn_tm=1024, sample=None)
        ys.append(y)
        sts.append(jnp.stack([s[0] for s in hg_p], axis=0))
    y_prompt = jnp.stack(ys, axis=0)
    state_hgrn_prompt = jnp.stack(sts, axis=1)

    init = [state_hgrn[jm].astype(F32) for jm in range(n_hg)]
    y, hg_s, sg_v = _trunk(x_sample.reshape(dec_batch * dec_seq, d), init, p,
                           tm=dec_batch * dec_seq, ffn_tm=dec_batch * dec_seq, sample=(dec_batch, dec_seq))
    y_sample = y.reshape(dec_batch, dec_seq, d)
    state_hgrn_sample = jnp.stack(hg_s, axis=0)
    state_sg_v_sample = jnp.stack([v.reshape(dec_batch, dec_seq, d) for v in sg_v], axis=0)
    return (y_prompt, y_sample, state_hgrn_prompt, state_hgrn_sample, state_sg_v_sample)
```

```python
import functools

import numpy as np
import jax
import jax.numpy as jnp
from jax import lax
from jax.experimental import pallas as pl
from jax.experimental.pallas import tpu as pltpu

EPS = 1e-6
F32 = jnp.float32
BF16 = jnp.bfloat16

LANES = 128
ROW_CHUNK = 16
HEAD_DIM = 128
SG_CHUNK = 128
FFN_TILE = 512
SG_TILE = 512
HG_TILE = 256
STREAM_CHUNK = 64
V7X_VMEM_BYTES = 64 * 1024 * 1024
VMEM_LIMIT_BYTES = V7X_VMEM_BYTES - 8 * 1024 * 1024
MAX_SPLIT_DECAY = 140.0


def _dot(a, b):
    return jnp.dot(a, b, preferred_element_type=F32)


def _dot_nt(a, b):
    return lax.dot_general(a, b, (((1,), (1,)), ((), ())), preferred_element_type=F32)


def _dot_tn(a, b):
    return lax.dot_general(a, b, (((0,), (0,)), ((), ())), preferred_element_type=F32)


def _rms(x, g):
    return x * lax.rsqrt(jnp.mean(x * x, axis=-1, keepdims=True) + EPS) * g


def _silu(x):
    return x * jax.nn.sigmoid(x)


def _gelu(x):
    return 0.5 * x * (1.0 + lax.erf(x * np.float32(np.sqrt(0.5))))


def _zero_after(v):
    u = pltpu.bitcast(v, jnp.uint32)
    u = lax.shift_right_logical(lax.shift_right_logical(u, jnp.uint32(16)), jnp.uint32(16))
    return u.astype(F32)


def _for_tiles(n_rows, n_cols, fn):
    def step(c, carry):
        rows = pl.ds(pl.multiple_of(c * ROW_CHUNK, ROW_CHUNK), ROW_CHUNK)
        for ct in range(n_cols // LANES):
            fn(rows, slice(ct * LANES, (ct + 1) * LANES))
        return carry
    lax.fori_loop(0, n_rows // ROW_CHUNK, step, 0, unroll=max(2, 32 * LANES // n_cols))


def _row_rsqrt(x):
    r = lax.rsqrt(jnp.mean(x * x, axis=-1, keepdims=True) + EPS)
    return jnp.broadcast_to(r, (x.shape[0], LANES))


def _pre_norm(x_ref, g_ref, h_scr, acc_scr, rs_scr):
    rs_scr[...] = _row_rsqrt(x_ref[...])

    def tile(rows, cols):
        h_scr[rows, cols] = (x_ref[rows, cols] * rs_scr[rows, :] * g_ref[:, cols]).astype(BF16)
        acc_scr[rows, cols] = jnp.zeros((ROW_CHUNK, LANES), F32)
    _for_tiles(x_ref.shape[0], x_ref.shape[1], tile)


def _post_norm(x_ref, g_ref, acc_scr, rs_scr, o_ref, scale):
    rs_scr[...] = _row_rsqrt(acc_scr[...])

    def tile(rows, cols):
        o_ref[rows, cols] = x_ref[rows, cols] + (acc_scr[rows, cols] * rs_scr[rows, :]) * (scale * g_ref[:, cols])
    _for_tiles(x_ref.shape[0], x_ref.shape[1], tile)


def _params():
    return pltpu.CompilerParams(dimension_semantics=("arbitrary", "arbitrary"),
                                vmem_limit_bytes=VMEM_LIMIT_BYTES)


def _ffn_body(xn_ref, xp_ref, gpre_ref, gpost_ref, wa_ref, wb_ref, wd_ref, o_ref, h0, h1, acc0, acc1,
              *, nt, nslice):
    r = pl.program_id(0)
    j = pl.program_id(1)
    rs = xn_ref.shape[0]
    base = jnp.minimum(j, nslice - 1) * rs

    def chunk_rows(c):
        return pl.ds(pl.multiple_of(base + c * ROW_CHUNK, ROW_CHUNK), ROW_CHUNK)

    def stages(h_pre, h_main, acc_main, acc_post):
        def pre():
            for c in range(rs // ROW_CHUNK):
                xc = xn_ref[c * ROW_CHUNK:(c + 1) * ROW_CHUNK, :]
                h_pre[chunk_rows(c), :] = _rms(xc, gpre_ref[...]).astype(BF16)

        def main(zero=None):
            h = h_main[...]
            a = _dot(h, wa_ref[...])
            if zero is not None:
                a = a + jnp.tile(zero[0:1, :], (1, a.shape[1] // LANES))
            b = _dot(h, wb_ref[...])
            contrib = _dot((_silu(a) * b).astype(BF16), wd_ref[...])
            acc_main[...] = jnp.where(j == 0, contrib, acc_main[...] + contrib)

        def post():
            for c in range(rs // ROW_CHUNK):
                rows = slice(c * ROW_CHUNK, (c + 1) * ROW_CHUNK)
                o_ref[rows, :] = xp_ref[rows, :] + 0.5 * _rms(acc_post[chunk_rows(c), :], gpost_ref[...])

        steady = jnp.logical_and(r >= 2, r < nt)

        @pl.when(steady)
        def _():
            pre()
            post()
            main(_zero_after(h_pre[chunk_rows(0), 0:LANES]) + _zero_after(o_ref[0:8, 0:LANES]))

        @pl.when(jnp.logical_not(steady))
        def _():
            @pl.when(jnp.logical_and(r == 0, j == 0))
            def _():
                acc0[...] = jnp.zeros_like(acc0)
                acc1[...] = jnp.zeros_like(acc1)
            pl.when(r < nt)(pre)
            pl.when(jnp.logical_and(r >= 1, r <= nt))(main)
            pl.when(r >= 2)(post)

    pl.when(r % 2 == 0)(lambda: stages(h0, h1, acc1, acc0))
    pl.when(r % 2 == 1)(lambda: stages(h1, h0, acc0, acc1))


def _ffn(x, g_pre, g_post, w_up, w_down, li, k, *, tm):
    t, d = x.shape
    f = w_down.shape[2]
    tf = w_up.shape[4]
    nf = f // tf
    nt = t // tm
    nslice = min(nf, tm // ROW_CHUNK)
    while tm % (nslice * ROW_CHUNK):
        nslice -= 1
    rs = tm // nslice
    assert t % tm == 0 and f % tf == 0

    def w_step(r, j):
        return jnp.where(r == 0, 0, jnp.where(r > nt, nf - 1, j))

    def next_slice(r, j):
        return jnp.where(r < nt, r * nslice + jnp.minimum(j, nslice - 1), nt * nslice - 1)

    def prev_slice(r, j):
        return jnp.where(r >= 2, (r - 2) * nslice + jnp.minimum(j, nslice - 1), 0)

    return pl.pallas_call(
        functools.partial(_ffn_body, nt=nt, nslice=nslice),
        out_shape=jax.ShapeDtypeStruct((t, d), F32),
        grid=(nt + 2, nf),
        in_specs=[
            pl.BlockSpec((rs, d), lambda r, j: (next_slice(r, j), 0)),
            pl.BlockSpec((rs, d), lambda r, j: (prev_slice(r, j), 0)),
            pl.BlockSpec((1, d), lambda r, j: (0, 0)),
            pl.BlockSpec((1, d), lambda r, j: (0, 0)),
            pl.BlockSpec((None, None, None, d, tf), lambda r, j: (li, k, w_step(r, j), 0, 0)),
            pl.BlockSpec((None, None, None, d, tf), lambda r, j: (li, k, w_step(r, j) + nf, 0, 0)),
            pl.BlockSpec((None, None, tf, d), lambda r, j: (li, k, w_step(r, j), 0)),
        ],
        out_specs=pl.BlockSpec((rs, d), lambda r, j: (prev_slice(r, j), 0)),
        scratch_shapes=[pltpu.VMEM((tm, d), BF16), pltpu.VMEM((tm, d), BF16),
                        pltpu.VMEM((tm, d), F32), pltpu.VMEM((tm, d), F32)],
        compiler_params=_params(),
    )(x, x, g_pre, g_post, w_up, w_up, w_down)


def _sg_mask(seg_mode):
    row = lax.broadcasted_iota(jnp.int32, (SG_CHUNK, SG_CHUNK), 0)
    col = lax.broadcasted_iota(jnp.int32, (SG_CHUNK, SG_CHUNK), 1)
    if seg_mode is None:
        return (row // STREAM_CHUNK) >= (col // STREAM_CHUNK)
    return (row // seg_mode) == (col // seg_mode)


def _sg_body(x_ref, g2_ref, g3_ref, wu_ref, wv_ref, lng_ref, lnb_ref, ws_ref, bias_ref, wo_ref,
             *rest, ng, gpt, nchunk, seg_mode, emit_v):
    if emit_v:
        o_ref, vout_ref, h_scr, u_scr, v_scr, y_scr, acc_scr, rs_scr, mu_scr = rest
    else:
        o_ref, h_scr, u_scr, v_scr, y_scr, acc_scr, rs_scr, mu_scr = rest
        vout_ref = None
    j = pl.program_id(1)
    tn = gpt * LANES
    d = ng * tn
    tm = x_ref.shape[0]

    @pl.when(j == 0)
    def _():
        _pre_norm(x_ref, g2_ref, h_scr, acc_scr, rs_scr)

    @pl.when(j < ng)
    def _():
        h = h_scr[...]
        u_scr[j] = _gelu(_dot(h, wu_ref[...]))
        v_scr[j] = _gelu(_dot(h, wv_ref[...]))

    @pl.when(j == ng)
    def _():
        mu = sum(jnp.sum(v_scr[jj], axis=-1, keepdims=True) for jj in range(ng)) * (1.0 / d)
        var = sum(jnp.sum(jnp.square(v_scr[jj] - mu), axis=-1, keepdims=True)
                  for jj in range(ng)) * (1.0 / d)
        mu_scr[...] = jnp.broadcast_to(mu, (tm, LANES))
        rs_scr[...] = jnp.broadcast_to(lax.rsqrt(var + EPS), (tm, LANES))
        for jj in range(ng):
            def tile(rows, cols, jj=jj):
                gcols = slice(jj * tn + cols.start, jj * tn + cols.stop)
                vn = ((v_scr[jj, rows, cols] - mu_scr[rows, :]) * rs_scr[rows, :] * lng_ref[:, gcols]
                      + lnb_ref[:, gcols])
                v_scr[jj, rows, cols] = vn
                if emit_v:
                    vout_ref[rows, gcols] = vn
            _for_tiles(tm, tn, tile)

    @pl.when(j >= ng)
    def _():
        jj = j - ng
        mask = _sg_mask(seg_mode)
        for g in range(gpt):
            cols = slice(g * LANES, (g + 1) * LANES)
            w = jnp.where(mask, ws_ref[g], 0.0).astype(BF16)
            for n in range(nchunk):
                rows = slice(n * SG_CHUNK, (n + 1) * SG_CHUNK)
                s = _dot(w, v_scr[jj, rows, cols].astype(BF16)) + bias_ref[:, cols]
                y_scr[rows, cols] = (u_scr[jj, rows, cols] * s).astype(BF16)
        acc_scr[...] += _dot(y_scr[...], wo_ref[...])

    @pl.when(j == 2 * ng - 1)
    def _():
        _post_norm(x_ref, g3_ref, acc_scr, rs_scr, o_ref, 1.0)


def _sg_mixer(x, g2, g3, w_in, ln_g, ln_b, w_s, bias_rows, w_out, jm, *, tm, seg_mode, emit_v):
    t, d = x.shape
    tn = w_in.shape[3]
    gpt = tn // LANES
    ng = d // tn
    assert t % tm == 0 and tm % SG_CHUNK == 0 and d % tn == 0 and tm % ROW_CHUNK == 0
    last = ng - 1
    body = functools.partial(_sg_body, ng=ng, gpt=gpt, nchunk=tm // SG_CHUNK,
                             seg_mode=seg_mode, emit_v=emit_v)
    row_spec = pl.BlockSpec((tm, d), lambda i, j: (i, 0))
    vec_spec = pl.BlockSpec((1, d), lambda i, j: (0, 0))
    out_shape = jax.ShapeDtypeStruct((t, d), F32)
    return pl.pallas_call(
        body,
        out_shape=(out_shape, out_shape) if emit_v else out_shape,
        grid=(t // tm, 2 * ng),
        in_specs=[
            row_spec, vec_spec, vec_spec,
            pl.BlockSpec((None, None, d, tn), lambda i, j: (jm, jnp.minimum(j, last), 0, 0)),
            pl.BlockSpec((None, None, d, tn), lambda i, j: (jm, ng + jnp.minimum(j, last), 0, 0)),
            vec_spec, vec_spec,
            pl.BlockSpec((gpt, SG_CHUNK, SG_CHUNK), lambda i, j: (jnp.maximum(j - ng, 0), 0, 0)),
            pl.BlockSpec((SG_CHUNK, tn), lambda i, j: (0, jnp.maximum(j - ng, 0))),
            pl.BlockSpec((None, tn, d), lambda i, j: (jm, jnp.maximum(j - ng, 0), 0)),
        ],
        out_specs=(row_spec, row_spec) if emit_v else row_spec,
        scratch_shapes=[
            pltpu.VMEM((tm, d), BF16),
            pltpu.VMEM((ng, tm, tn), F32),
            pltpu.VMEM((ng, tm, tn), F32),
            pltpu.VMEM((tm, tn), BF16),
            pltpu.VMEM((tm, d), F32),
            pltpu.VMEM((tm, LANES), F32),
            pltpu.VMEM((tm, LANES), F32),
        ],
        compiler_params=_params(),
    )(x, g2, g3, w_in, w_in, ln_g, ln_b, w_s, bias_rows, w_out)


def _scan_tables(chunk, seg):
    levels = []
    m = seg // 2
    while m >= 1:
        levels.append(m)
        m //= 2
    nb = 2 + len(levels)
    t = np.arange(chunk)[:, None]
    j = np.arange(chunk)[None, :]
    same_seg = (t // seg) == (j // seg)
    sums = np.zeros((nb, chunk, chunk), np.float32)
    sums[0] = same_seg & (j <= t)
    sums[1] = same_seg & (j > t)
    level_id = np.where(t == j, 0, -1).astype(np.int32)
    for li, m in enumerate(levels):
        g = 2 * m
        ref = (t // g) * g + m - 1
        odd = (t % g) >= m
        sums[2 + li] = np.where(odd, (j > ref) & (j <= t), (j > t) & (j <= ref))
        pair = ((t // g) == (j // g)) & odd & ((j % g) < m)
        level_id = np.where(pair, li + 1, level_id).astype(np.int32)
    sums = sums.reshape(nb * chunk, chunk)
    return np.concatenate([sums, sums, sums], axis=1), level_id, nb


def _split3(x):
    hi = x.astype(BF16)
    r1 = x - hi.astype(F32)
    mid = r1.astype(BF16)
    lo = (r1 - mid.astype(F32)).astype(BF16)
    return jnp.concatenate([hi, mid, lo], axis=0)


def _hgrn_body(x_ref, g2_ref, g3_ref, wq_ref, wf_ref, wi_ref, wg_ref, hl_ref, ng_ref, wo_ref,
               sums_ref, lid_ref, s0_ref, o_ref, sout_ref, *scratch,
               layer, chunk, seg, nh, nb, nchunk, nhg):
    sets = (scratch[0:6], scratch[6:12])
    h_scr, e_scr, st_scr, on_scr, acc_scr, rs_scr, worst_smem = scratch[12:]
    i = pl.program_id(0)
    j = pl.program_id(1)
    nseg = chunk // seg
    nlev = nb - 2
    heads = nhg * nh
    lid = lid_ref[...]

    @pl.when(j == 0)
    def _():
        _pre_norm(x_ref, g2_ref, h_scr, acc_scr, rs_scr)

    @pl.when(jnp.logical_and(i == 0, j == 0))
    def _():
        for s in range(nseg):
            for hd in range(heads):
                st_scr[s, hd] = s0_ref[s, hd].T

    def project(p):
        q_scr, lf_scr, k_scr, v_scr, gate_scr, b_scr = sets[p]
        hl = hl_ref[...]
        e = jnp.exp(hl - jnp.max(hl, axis=0, keepdims=True))
        pr = e / jnp.sum(e, axis=0, keepdims=True)
        lb = jnp.zeros_like(pr[0:1])
        for r in range(1, layer + 1):
            lb = lb + pr[r:r + 1]
        h = h_scr[...]
        q_scr[...] = _silu(_dot(h, wq_ref[...]))
        f = lb + (1.0 - lb) * jax.nn.sigmoid(_dot(h, wf_ref[...]))
        lf = jnp.log(f)
        lf_scr[...] = lf
        k_scr[...] = 1.0 - f
        v_scr[...] = _dot(h, wi_ref[...]).astype(BF16)
        gate_scr[...] = _silu(_dot(h, wg_ref[...]))
        if nseg == 1:
            worst = jnp.float32(0.0)
            for c in range(nchunk):
                rows = slice(c * chunk, (c + 1) * chunk)
                b = _dot(sums_ref[0:chunk, :], _split3(lf[rows, :]))
                b_scr[rows, :] = b
                worst = jnp.maximum(worst, jnp.max(-b[chunk - 1:chunk, :]))
            worst_smem[p] = worst

    def scan(p, split):
        q_scr, lf_scr, k_scr, v_scr, gate_scr, b_scr = sets[p]
        head0 = (j - 1) * nh

        def carry_state(rows, cols, head, o_intra, q_pre, k_end, vh, e_end):
            inter = []
            for s in range(nseg):
                srows = slice(s * seg, (s + 1) * seg)
                st = st_scr[s, head]
                inter.append(_dot_nt(q_pre[srows], st.astype(BF16)))
                st_scr[s, head] = st * e_end[s] + _dot_tn(vh[srows], k_end[srows])
            o = o_intra + (inter[0] if nseg == 1 else jnp.concatenate(inter, axis=0))
            on_scr[rows, cols] = (_rms(o, ng_ref[...]) * gate_scr[rows, cols]).astype(BF16)

        for c in range(nchunk):
            rows = slice(c * chunk, (c + 1) * chunk)
            if split:
                b = b_scr[rows, :]
                half = 0.5 * b[chunk - 1:chunk, :]
                x = b - half
                e_q = jnp.exp(x)
                e_k = jnp.exp(-x)
                e_half = jnp.exp(half)
            else:
                e_scr[...] = jnp.exp(_dot(sums_ref[...], _split3(lf_scr[rows, :])))
            for hh in range(nh):
                cols = slice(hh * HEAD_DIM, (hh + 1) * HEAD_DIM)
                vh = v_scr[rows, cols]
                if split:
                    qt = q_scr[rows, cols] * e_q[:, cols]
                    kt = k_scr[rows, cols] * e_k[:, cols]
                    eh = e_half[:, cols]
                    sc = jnp.where(lid >= 0, _dot_nt(qt.astype(BF16), kt.astype(BF16)), 0.0)
                    carry_state(rows, cols, head0 + hh, _dot(sc.astype(BF16), vh),
                                (qt * eh).astype(BF16), (kt * eh).astype(BF16), vh, [eh * eh])
                else:
                    qh = q_scr[rows, cols]
                    kh = k_scr[rows, cols]
                    e_pre = e_scr[0:chunk, cols]
                    k_end = (kh * e_scr[chunk:2 * chunk, cols]).astype(BF16)
                    sc = jnp.where(lid == 0, _dot_nt(qh.astype(BF16), kh.astype(BF16)), 0.0)
                    for l in range(nlev):
                        el = e_scr[(2 + l) * chunk:(3 + l) * chunk, cols]
                        sc_l = _dot_nt((qh * el).astype(BF16), (kh * el).astype(BF16))
                        sc = jnp.where(lid == l + 1, sc_l, sc)
                    e_end = [e_pre[(s + 1) * seg - 1:(s + 1) * seg, :] for s in range(nseg)]
                    carry_state(rows, cols, head0 + hh, _dot(sc.astype(BF16), vh),
                                (qh * e_pre).astype(BF16), k_end, vh, e_end)
        acc_scr[...] += _dot(on_scr[...], wo_ref[...])

    def both(p, split):
        project(p)
        scan(1 - p, split)

    for p in (0, 1):
        mine = (j % 2) == p
        inner = jnp.logical_and(mine, jnp.logical_and(j >= 1, j < nhg))
        last = jnp.logical_and(mine, j == nhg)
        if p == 0:
            pl.when(j == 0)(functools.partial(project, 0))
        if nseg == 1:
            benign = worst_smem[1 - p] <= MAX_SPLIT_DECAY
            pl.when(jnp.logical_and(inner, benign))(functools.partial(both, p, True))
            pl.when(jnp.logical_and(inner, jnp.logical_not(benign)))(functools.partial(both, p, False))
            if nhg % 2 == p:
                pl.when(jnp.logical_and(last, benign))(functools.partial(scan, 1 - p, True))
                pl.when(jnp.logical_and(last, jnp.logical_not(benign)))(functools.partial(scan, 1 - p, False))
        else:
            pl.when(inner)(functools.partial(both, p, False))
            if nhg % 2 == p:
                pl.when(last)(functools.partial(scan, 1 - p, False))

    @pl.when(j == nhg)
    def _():
        _post_norm(x_ref, g3_ref, acc_scr, rs_scr, o_ref, 1.0)

    @pl.when(jnp.logical_and(i == pl.num_programs(0) - 1, j == nhg))
    def _():
        for s in range(nseg):
            for hd in range(heads):
                sout_ref[s, hd] = st_scr[s, hd].T


def _hgrn_mixer(x, g2, g3, w_in, hg_lower, norm_g, w_out, s0, jm, *, layer, tm, chunk, seg):
    t, d = x.shape
    heads = d // HEAD_DIM
    nseg = chunk // seg
    tn = w_in.shape[3]
    nh = tn // HEAD_DIM
    nhg = heads // nh
    assert t % tm == 0 and tm % chunk == 0 and heads % nh == 0 and tm % ROW_CHUNK == 0
    assert s0.shape == (nseg, heads, HEAD_DIM, HEAD_DIM)
    assert nseg == 1 or t == chunk
    sums, level_id, nb = _scan_tables(chunk, seg)
    body = functools.partial(_hgrn_body, layer=layer, chunk=chunk, seg=seg, nh=nh, nb=nb,
                             nchunk=tm // chunk, nhg=nhg)
    row_spec = pl.BlockSpec((tm, d), lambda i, j: (i, 0))
    vec_spec = pl.BlockSpec((1, d), lambda i, j: (0, 0))
    nsec = d // tn
    state_spec = pl.BlockSpec(s0.shape, lambda i, j: (0, 0, 0, 0))

    def proj(j):
        return jnp.minimum(j, nhg - 1)

    def outp(j):
        return jnp.maximum(j - 1, 0)

    scratch_set = [
        pltpu.VMEM((tm, tn), F32),
        pltpu.VMEM((tm, tn), F32),
        pltpu.VMEM((tm, tn), F32),
        pltpu.VMEM((tm, tn), BF16),
        pltpu.VMEM((tm, tn), F32),
        pltpu.VMEM((tm, tn), F32),
    ]
    return pl.pallas_call(
        body,
        out_shape=(jax.ShapeDtypeStruct((t, d), F32), jax.ShapeDtypeStruct(s0.shape, F32)),
        grid=(t // tm, nhg + 1),
        in_specs=[
            row_spec, vec_spec, vec_spec,
            pl.BlockSpec((None, None, d, tn), lambda i, j: (jm, proj(j), 0, 0)),
            pl.BlockSpec((None, None, d, tn), lambda i, j: (jm, nsec + proj(j), 0, 0)),
            pl.BlockSpec((None, None, d, tn), lambda i, j: (jm, 2 * nsec + proj(j), 0, 0)),
            pl.BlockSpec((None, None, d, tn), lambda i, j: (jm, 3 * nsec + proj(j), 0, 0)),
            pl.BlockSpec((hg_lower.shape[0], tn), lambda i, j: (0, proj(j))),
            pl.BlockSpec((1, HEAD_DIM), lambda i, j: (0, 0)),
            pl.BlockSpec((None, tn, d), lambda i, j: (jm, outp(j), 0)),
            pl.BlockSpec(sums.shape, lambda i, j: (0, 0)),
            pl.BlockSpec(level_id.shape, lambda i, j: (0, 0)),
            state_spec,
        ],
        out_specs=(row_spec, state_spec),
        scratch_shapes=scratch_set + scratch_set + [
            pltpu.VMEM((tm, d), BF16),
            pltpu.VMEM((nb * chunk, tn), F32),
            pltpu.VMEM((nseg, heads, HEAD_DIM, HEAD_DIM), F32),
            pltpu.VMEM((tm, tn), BF16),
            pltpu.VMEM((tm, d), F32),
            pltpu.VMEM((tm, LANES), F32),
            pltpu.SMEM((2,), F32),
        ],
        compiler_params=_params(),
    )(x, g2, g3, w_in, w_in, w_in, w_in, hg_lower, norm_g, w_out,
      jnp.asarray(sums, BF16), jnp.asarray(level_id), s0)


def _column_tiles_body(w_ref, o_ref):
    per, _, tn = o_ref.shape
    for t in range(per):
        o_ref[t] = w_ref[:, t * tn:(t + 1) * tn].astype(BF16)


def _column_tiles(w, tn, per):
    *lead, d, n = w.shape
    nl = int(np.prod(lead))
    assert n % (tn * per) == 0
    out = pl.pallas_call(
        _column_tiles_body,
        out_shape=jax.ShapeDtypeStruct((nl, n // tn, d, tn), BF16),
        grid=(nl, n // (tn * per)),
        in_specs=[pl.BlockSpec((None, d, tn * per), lambda l, b: (l, 0, b))],
        out_specs=pl.BlockSpec((None, per, d, tn), lambda l, b: (l, b, 0, 0)),
        compiler_params=_params(),
    )(w.reshape(nl, d, n))
    return out.reshape(*lead, n // tn, d, tn)


def _trunk(x, hg_init, p, *, tm, ffn_tm, sample):
    depth = p["norm_g"].shape[0]
    n_mix = 2
    hg_states, sg_vs = [], []
    for i in range(depth):
        g = p["norm_g"][i]
        row = lambda r: g[r:r + 1]
        jm = i // n_mix
        x = _ffn(x, row(0), row(1), p["ffn_w_up"], p["ffn_w_down"], i, 0, tm=ffn_tm)
        if i % n_mix == 0:
            w_s, b_s = p["sg_w_s"][jm], p["sg_b_s"][jm]
            if sample is not None:
                n_streams, seq = sample
                w_s = jnp.tile(w_s[:, :seq, :seq], (1, n_streams, n_streams))
                b_s = jnp.tile(b_s[:, :seq], (1, n_streams))
            bias_rows = jnp.repeat(b_s.T, LANES, axis=1)
            res = _sg_mixer(x, row(2), row(3), p["sg_w_in"], p["sg_ln_g"][jm:jm + 1],
                            p["sg_ln_b"][jm:jm + 1], w_s, bias_rows, p["sg_w_out"], jm,
                            tm=tm, seg_mode=None if sample is None else sample[1],
                            emit_v=sample is not None)
            if sample is not None:
                x, v_rows = res
                sg_vs.append(v_rows)
            else:
                x = res
        else:
            x, st = _hgrn_mixer(x, row(2), row(3), p["hg_w_in"], p["hg_lower"],
                                p["hg_norm_g"][jm:jm + 1], p["hg_w_out"], hg_init[jm], jm,
                                layer=i, tm=tm, chunk=SG_CHUNK if sample is None else x.shape[0],
                                seg=SG_CHUNK if sample is None else sample[1])
            hg_states.append(st)
        x = _ffn(x, row(4), row(5), p["ffn_w_up"], p["ffn_w_down"], i, 1, tm=ffn_tm)
    return x, hg_states, sg_vs


def kernel(x_prompt, x_sample, state_hgrn, norm_g, ffn_w_up, ffn_w_down, sg_w_in, sg_ln_g, sg_ln_b,
           sg_w_s, sg_b_s, sg_w_out, hg_w_in, hg_lower, hg_norm_g, hg_w_out):
    batch, seq, d = x_prompt.shape
    dec_batch, dec_seq, _ = x_sample.shape
    n_hg = hg_w_in.shape[0]
    heads = d // HEAD_DIM
    p = dict(norm_g=norm_g, ffn_w_up=_column_tiles(ffn_w_up, FFN_TILE, 2), ffn_w_down=ffn_w_down.astype(BF16),
             sg_w_in=_column_tiles(sg_w_in, SG_TILE, 2), sg_ln_g=sg_ln_g, sg_ln_b=sg_ln_b, sg_w_s=sg_w_s,
             sg_b_s=sg_b_s, sg_w_out=sg_w_out.astype(BF16), hg_w_in=_column_tiles(hg_w_in, HG_TILE, 4),
             hg_lower=hg_lower, hg_norm_g=hg_norm_g, hg_w_out=hg_w_out.astype(BF16))

    ys, sts = [], []
    for b in range(batch):
        init = [jnp.zeros((1, heads, HEAD_DIM, HEAD_DIM), F32) for _ in range(n_hg)]
        y, hg_p, _ = _trunk(x_prompt[b], init, p, tm=512, ffn_tm=1024, sample=None)
        ys.append(y)
        sts.append(jnp.stack([s[0] for s in hg_p], axis=0))
    y_prompt = jnp.stack(ys, axis=0)
    state_hgrn_prompt = jnp.stack(sts, axis=1)

    init = [state_hgrn[jm].astype(F32) for jm in range(n_hg)]
    y, hg_s, sg_v = _trunk(x_sample.reshape(dec_batch * dec_seq, d), init, p,
                           tm=dec_batch * dec_seq, ffn_tm=dec_batch * dec_seq, sample=(dec_batch, dec_seq))
    y_sample = y.reshape(dec_batch, dec_seq, d)
    state_hgrn_sample = jnp.stack(hg_s, axis=0)
    state_sg_v_sample = jnp.stack([v.reshape(dec_batch, dec_seq, d) for v in sg_v], axis=0)
    return (y_prompt, y_sample, state_hgrn_prompt, state_hgrn_sample, state_sg_v_sample)
```

```python
import functools

import numpy as np
import jax
import jax.numpy as jnp
from jax import lax
from jax.experimental import pallas as pl
from jax.experimental.pallas import tpu as pltpu

EPS = 1e-6
F32 = jnp.float32
BF16 = jnp.bfloat16

LANES = 128
ROW_CHUNK = 16
HEAD_DIM = 128
SG_CHUNK = 128
FFN_TILE = 512
SG_TILE = 512
UP_CAST_ROWS = 16
DOWN_CAST_ROWS = 64
HG_TILE = 256
STREAM_CHUNK = 64
V7X_VMEM_BYTES = 64 * 1024 * 1024
VMEM_LIMIT_BYTES = V7X_VMEM_BYTES - 8 * 1024 * 1024
MAX_SPLIT_DECAY = 140.0


def _dot(a, b):
    return jnp.dot(a, b, preferred_element_type=F32)


def _dot_nt(a, b):
    return lax.dot_general(a, b, (((1,), (1,)), ((), ())), preferred_element_type=F32)


def _dot_tn(a, b):
    return lax.dot_general(a, b, (((0,), (0,)), ((), ())), preferred_element_type=F32)


def _rms(x, g):
    return x * lax.rsqrt(jnp.mean(x * x, axis=-1, keepdims=True) + EPS) * g


def _silu(x):
    return x * jax.nn.sigmoid(x)


def _gelu(x):
    return 0.5 * x * (1.0 + lax.erf(x * np.float32(np.sqrt(0.5))))


def _zero_after(v):
    u = pltpu.bitcast(v, jnp.uint32)
    u = lax.shift_right_logical(lax.shift_right_logical(u, jnp.uint32(16)), jnp.uint32(16))
    return u.astype(F32)


def _for_tiles(n_rows, n_cols, fn):
    def step(c, carry):
        rows = pl.ds(pl.multiple_of(c * ROW_CHUNK, ROW_CHUNK), ROW_CHUNK)
        for ct in range(n_cols // LANES):
            fn(rows, slice(ct * LANES, (ct + 1) * LANES))
        return carry
    lax.fori_loop(0, n_rows // ROW_CHUNK, step, 0, unroll=max(2, 32 * LANES // n_cols))


def _row_rsqrt(x):
    r = lax.rsqrt(jnp.mean(x * x, axis=-1, keepdims=True) + EPS)
    return jnp.broadcast_to(r, (x.shape[0], LANES))


def _pre_norm(x_ref, g_ref, h_scr, acc_scr, rs_scr):
    rs_scr[...] = _row_rsqrt(x_ref[...])

    def tile(rows, cols):
        h_scr[rows, cols] = (x_ref[rows, cols] * rs_scr[rows, :] * g_ref[:, cols]).astype(BF16)
        acc_scr[rows, cols] = jnp.zeros((ROW_CHUNK, LANES), F32)
    _for_tiles(x_ref.shape[0], x_ref.shape[1], tile)


def _post_norm(x_ref, g_ref, acc_scr, rs_scr, o_ref, scale):
    rs_scr[...] = _row_rsqrt(acc_scr[...])

    def tile(rows, cols):
        o_ref[rows, cols] = x_ref[rows, cols] + (acc_scr[rows, cols] * rs_scr[rows, :]) * (scale * g_ref[:, cols])
    _for_tiles(x_ref.shape[0], x_ref.shape[1], tile)


def _params():
    return pltpu.CompilerParams(dimension_semantics=("arbitrary", "arbitrary"),
                                vmem_limit_bytes=VMEM_LIMIT_BYTES)


def _ffn_body(*refs, nt, nslice, nside):
    xn_ref, xp_ref, gpre_ref, gpost_ref, wa_ref, wb_ref, wd_ref = refs[:7]
    side_src = refs[7:7 + nside]
    o_ref = refs[7 + nside]
    side_dst = refs[8 + nside:8 + 2 * nside]
    h0, h1, acc0, acc1 = refs[8 + 2 * nside:]
    r = pl.program_id(0)
    j = pl.program_id(1)
    rs = xn_ref.shape[0]
    base = jnp.minimum(j, nslice - 1) * rs

    def chunk_rows(c):
        return pl.ds(pl.multiple_of(base + c * ROW_CHUNK, ROW_CHUNK), ROW_CHUNK)

    def convert_side():
        for src, dst in zip(side_src, side_dst):
            dst[...] = src[...].astype(BF16)

    def stages(h_pre, h_main, acc_main, acc_post):
        def pre():
            for c in range(rs // ROW_CHUNK):
                xc = xn_ref[c * ROW_CHUNK:(c + 1) * ROW_CHUNK, :]
                h_pre[chunk_rows(c), :] = _rms(xc, gpre_ref[...]).astype(BF16)

        def main(zero=None):
            h = h_main[...]
            a = _dot(h, wa_ref[...])
            if zero is not None:
                a = a + jnp.tile(zero[0:1, :], (1, a.shape[1] // LANES))
            b = _dot(h, wb_ref[...])
            contrib = _dot((_silu(a) * b).astype(BF16), wd_ref[...])
            acc_main[...] = jnp.where(j == 0, contrib, acc_main[...] + contrib)

        def post():
            for c in range(rs // ROW_CHUNK):
                rows = slice(c * ROW_CHUNK, (c + 1) * ROW_CHUNK)
                o_ref[rows, :] = xp_ref[rows, :] + 0.5 * _rms(acc_post[chunk_rows(c), :], gpost_ref[...])

        steady = jnp.logical_and(r >= 2, r < nt)

        @pl.when(steady)
        def _():
            pre()
            post()
            convert_side()
            zero = _zero_after(h_pre[chunk_rows(0), 0:LANES]) + _zero_after(o_ref[0:8, 0:LANES])
            for dst in side_dst:
                zero = zero + _zero_after(dst[0:ROW_CHUNK, 0:LANES])
            main(zero)

        @pl.when(jnp.logical_not(steady))
        def _():
            @pl.when(jnp.logical_and(r == 0, j == 0))
            def _():
                acc0[...] = jnp.zeros_like(acc0)
                acc1[...] = jnp.zeros_like(acc1)
            pl.when(r < nt)(pre)
            pl.when(jnp.logical_and(r >= 1, r <= nt))(main)
            pl.when(r >= 2)(post)
            convert_side()

    pl.when(r % 2 == 0)(lambda: stages(h0, h1, acc1, acc0))
    pl.when(r % 2 == 1)(lambda: stages(h1, h0, acc0, acc1))


def _ffn(x, g_pre, g_post, w_up, w_down, *, tm, side=()):
    t, d = x.shape
    f = w_down.shape[0]
    tf = FFN_TILE
    nf = f // tf
    nt = t // tm
    nslice = min(nf, tm // ROW_CHUNK)
    while tm % (nslice * ROW_CHUNK):
        nslice -= 1
    rs = tm // nslice
    assert t % tm == 0 and f % tf == 0

    def w_step(r, j):
        return jnp.where(r == 0, 0, jnp.where(r > nt, nf - 1, j))

    def next_slice(r, j):
        return jnp.where(r < nt, r * nslice + jnp.minimum(j, nslice - 1), nt * nslice - 1)

    def prev_slice(r, j):
        return jnp.where(r >= 2, (r - 2) * nslice + jnp.minimum(j, nslice - 1), 0)

    side_in, side_out, side_shapes = [], [], []
    for src, lead, rb in side:
        rows, cols = src.shape[-2:]
        nblk = rows // rb
        assert rows % rb == 0 and nblk <= nt * nf and len(lead) == src.ndim - 2

        def blk(r, j, nblk=nblk):
            return jnp.clip((r - 1) * nf + j, 0, nblk - 1)

        side_in.append(pl.BlockSpec((None,) * len(lead) + (rb, cols),
                                    lambda r, j, lead=lead, blk=blk: (*lead, blk(r, j), 0)))
        side_out.append(pl.BlockSpec((rb, cols), lambda r, j, blk=blk: (blk(r, j), 0)))
        side_shapes.append(jax.ShapeDtypeStruct((rows, cols), BF16))

    out = pl.pallas_call(
        functools.partial(_ffn_body, nt=nt, nslice=nslice, nside=len(side)),
        out_shape=[jax.ShapeDtypeStruct((t, d), F32)] + side_shapes,
        grid=(nt + 2, nf),
        in_specs=[
            pl.BlockSpec((rs, d), lambda r, j: (next_slice(r, j), 0)),
            pl.BlockSpec((rs, d), lambda r, j: (prev_slice(r, j), 0)),
            pl.BlockSpec((1, d), lambda r, j: (0, 0)),
            pl.BlockSpec((1, d), lambda r, j: (0, 0)),
            pl.BlockSpec((d, tf), lambda r, j: (0, w_step(r, j))),
            pl.BlockSpec((d, tf), lambda r, j: (0, w_step(r, j) + nf)),
            pl.BlockSpec((tf, d), lambda r, j: (w_step(r, j), 0)),
        ] + side_in,
        out_specs=[pl.BlockSpec((rs, d), lambda r, j: (prev_slice(r, j), 0))] + side_out,
        scratch_shapes=[pltpu.VMEM((tm, d), BF16), pltpu.VMEM((tm, d), BF16),
                        pltpu.VMEM((tm, d), F32), pltpu.VMEM((tm, d), F32)],
        compiler_params=_params(),
    )(x, x, g_pre, g_post, w_up, w_up, w_down, *[src for src, _, _ in side])
    return out[0], tuple(out[1:])


def _sg_mask(seg_mode):
    row = lax.broadcasted_iota(jnp.int32, (SG_CHUNK, SG_CHUNK), 0)
    col = lax.broadcasted_iota(jnp.int32, (SG_CHUNK, SG_CHUNK), 1)
    if seg_mode is None:
        return (row // STREAM_CHUNK) >= (col // STREAM_CHUNK)
    return (row // seg_mode) == (col // seg_mode)


def _sg_body(x_ref, g2_ref, g3_ref, wu_ref, wv_ref, lng_ref, lnb_ref, ws_ref, bias_ref, wo_ref,
             *rest, ng, gpt, nchunk, seg_mode, emit_v):
    if emit_v:
        o_ref, vout_ref, h_scr, u_scr, v_scr, y_scr, acc_scr, rs_scr, mu_scr = rest
    else:
        o_ref, h_scr, u_scr, v_scr, y_scr, acc_scr, rs_scr, mu_scr = rest
        vout_ref = None
    j = pl.program_id(1)
    tn = gpt * LANES
    d = ng * tn
    tm = x_ref.shape[0]

    @pl.when(j == 0)
    def _():
        _pre_norm(x_ref, g2_ref, h_scr, acc_scr, rs_scr)

    @pl.when(j < ng)
    def _():
        h = h_scr[...]
        u_scr[j] = _gelu(_dot(h, wu_ref[...]))
        v_scr[j] = _gelu(_dot(h, wv_ref[...]))

    @pl.when(j == ng)
    def _():
        mu = sum(jnp.sum(v_scr[jj], axis=-1, keepdims=True) for jj in range(ng)) * (1.0 / d)
        var = sum(jnp.sum(jnp.square(v_scr[jj] - mu), axis=-1, keepdims=True)
                  for jj in range(ng)) * (1.0 / d)
        mu_scr[...] = jnp.broadcast_to(mu, (tm, LANES))
        rs_scr[...] = jnp.broadcast_to(lax.rsqrt(var + EPS), (tm, LANES))
        for jj in range(ng):
            def tile(rows, cols, jj=jj):
                gcols = slice(jj * tn + cols.start, jj * tn + cols.stop)
                vn = ((v_scr[jj, rows, cols] - mu_scr[rows, :]) * rs_scr[rows, :] * lng_ref[:, gcols]
                      + lnb_ref[:, gcols])
                v_scr[jj, rows, cols] = vn
                if emit_v:
                    vout_ref[rows, gcols] = vn
            _for_tiles(tm, tn, tile)

    @pl.when(j >= ng)
    def _():
        jj = j - ng
        mask = _sg_mask(seg_mode)
        for g in range(gpt):
            cols = slice(g * LANES, (g + 1) * LANES)
            w = jnp.where(mask, ws_ref[g], 0.0).astype(BF16)
            for n in range(nchunk):
                rows = slice(n * SG_CHUNK, (n + 1) * SG_CHUNK)
                s = _dot(w, v_scr[jj, rows, cols].astype(BF16)) + bias_ref[:, cols]
                y_scr[rows, cols] = (u_scr[jj, rows, cols] * s).astype(BF16)
        acc_scr[...] += _dot(y_scr[...], wo_ref[...])

    @pl.when(j == 2 * ng - 1)
    def _():
        _post_norm(x_ref, g3_ref, acc_scr, rs_scr, o_ref, 1.0)


def _sg_mixer(x, g2, g3, w_in, ln_g, ln_b, w_s, bias_rows, w_out, jm, *, tm, seg_mode, emit_v):
    t, d = x.shape
    tn = w_in.shape[3]
    gpt = tn // LANES
    ng = d // tn
    assert t % tm == 0 and tm % SG_CHUNK == 0 and d % tn == 0 and tm % ROW_CHUNK == 0
    last = ng - 1
    body = functools.partial(_sg_body, ng=ng, gpt=gpt, nchunk=tm // SG_CHUNK,
                             seg_mode=seg_mode, emit_v=emit_v)
    row_spec = pl.BlockSpec((tm, d), lambda i, j: (i, 0))
    vec_spec = pl.BlockSpec((1, d), lambda i, j: (0, 0))
    out_shape = jax.ShapeDtypeStruct((t, d), F32)
    return pl.pallas_call(
        body,
        out_shape=(out_shape, out_shape) if emit_v else out_shape,
        grid=(t // tm, 2 * ng),
        in_specs=[
            row_spec, vec_spec, vec_spec,
            pl.BlockSpec((None, None, d, tn), lambda i, j: (jm, jnp.minimum(j, last), 0, 0)),
            pl.BlockSpec((None, None, d, tn), lambda i, j: (jm, ng + jnp.minimum(j, last), 0, 0)),
            vec_spec, vec_spec,
            pl.BlockSpec((gpt, SG_CHUNK, SG_CHUNK), lambda i, j: (jnp.maximum(j - ng, 0), 0, 0)),
            pl.BlockSpec((SG_CHUNK, tn), lambda i, j: (0, jnp.maximum(j - ng, 0))),
            pl.BlockSpec((None, tn, d), lambda i, j: (jm, jnp.maximum(j - ng, 0), 0)),
        ],
        out_specs=(row_spec, row_spec) if emit_v else row_spec,
        scratch_shapes=[
            pltpu.VMEM((tm, d), BF16),
            pltpu.VMEM((ng, tm, tn), F32),
            pltpu.VMEM((ng, tm, tn), F32),
            pltpu.VMEM((tm, tn), BF16),
            pltpu.VMEM((tm, d), F32),
            pltpu.VMEM((tm, LANES), F32),
            pltpu.VMEM((tm, LANES), F32),
        ],
        compiler_params=_params(),
    )(x, g2, g3, w_in, w_in, ln_g, ln_b, w_s, bias_rows, w_out)


def _scan_tables(chunk, seg):
    levels = []
    m = seg // 2
    while m >= 1:
        levels.append(m)
        m //= 2
    nb = 2 + len(levels)
    t = np.arange(chunk)[:, None]
    j = np.arange(chunk)[None, :]
    same_seg = (t // seg) == (j // seg)
    sums = np.zeros((nb, chunk, chunk), np.float32)
    sums[0] = same_seg & (j <= t)
    sums[1] = same_seg & (j > t)
    level_id = np.where(t == j, 0, -1).astype(np.int32)
    for li, m in enumerate(levels):
        g = 2 * m
        ref = (t // g) * g + m - 1
        odd = (t % g) >= m
        sums[2 + li] = np.where(odd, (j > ref) & (j <= t), (j > t) & (j <= ref))
        pair = ((t // g) == (j // g)) & odd & ((j % g) < m)
        level_id = np.where(pair, li + 1, level_id).astype(np.int32)
    sums = sums.reshape(nb * chunk, chunk)
    return np.concatenate([sums, sums, sums], axis=1), level_id, nb


def _split3(x):
    hi = x.astype(BF16)
    r1 = x - hi.astype(F32)
    mid = r1.astype(BF16)
    lo = (r1 - mid.astype(F32)).astype(BF16)
    return jnp.concatenate([hi, mid, lo], axis=0)


def _hgrn_body(x_ref, g2_ref, g3_ref, wq_ref, wf_ref, wi_ref, wg_ref, hl_ref, ng_ref, wo_ref,
               sums_ref, lid_ref, s0_ref, o_ref, sout_ref, *scratch,
               layer, chunk, seg, nh, nb, nchunk, nhg):
    sets = (scratch[0:6], scratch[6:12])
    h_scr, e_scr, st_scr, on_scr, acc_scr, rs_scr, worst_smem = scratch[12:]
    i = pl.program_id(0)
    j = pl.program_id(1)
    nseg = chunk // seg
    nlev = nb - 2
    heads = nhg * nh
    lid = lid_ref[...]

    @pl.when(j == 0)
    def _():
        _pre_norm(x_ref, g2_ref, h_scr, acc_scr, rs_scr)

    @pl.when(jnp.logical_and(i == 0, j == 0))
    def _():
        for s in range(nseg):
            for hd in range(heads):
                st_scr[s, hd] = s0_ref[s, hd].T

    def project(p):
        q_scr, lf_scr, k_scr, v_scr, gate_scr, b_scr = sets[p]
        hl = hl_ref[...]
        e = jnp.exp(hl - jnp.max(hl, axis=0, keepdims=True))
        pr = e / jnp.sum(e, axis=0, keepdims=True)
        lb = jnp.zeros_like(pr[0:1])
        for r in range(1, layer + 1):
            lb = lb + pr[r:r + 1]
        h = h_scr[...]
        q_scr[...] = _silu(_dot(h, wq_ref[...]))
        f = lb + (1.0 - lb) * jax.nn.sigmoid(_dot(h, wf_ref[...]))
        lf = jnp.log(f)
        lf_scr[...] = lf
        k_scr[...] = 1.0 - f
        v_scr[...] = _dot(h, wi_ref[...]).astype(BF16)
        gate_scr[...] = _silu(_dot(h, wg_ref[...]))
        if nseg == 1:
            worst = jnp.float32(0.0)
            for c in range(nchunk):
                rows = slice(c * chunk, (c + 1) * chunk)
                b = _dot(sums_ref[0:chunk, :], _split3(lf[rows, :]))
                b_scr[rows, :] = b
                worst = jnp.maximum(worst, jnp.max(-b[chunk - 1:chunk, :]))
            worst_smem[p] = worst

    def scan(p, split):
        q_scr, lf_scr, k_scr, v_scr, gate_scr, b_scr = sets[p]
        head0 = (j - 1) * nh

        def carry_state(rows, cols, head, o_intra, q_pre, k_end, vh, e_end):
            inter = []
            for s in range(nseg):
                srows = slice(s * seg, (s + 1) * seg)
                st = st_scr[s, head]
                inter.append(_dot_nt(q_pre[srows], st.astype(BF16)))
                st_scr[s, head] = st * e_end[s] + _dot_tn(vh[srows], k_end[srows])
            o = o_intra + (inter[0] if nseg == 1 else jnp.concatenate(inter, axis=0))
            on_scr[rows, cols] = (_rms(o, ng_ref[...]) * gate_scr[rows, cols]).astype(BF16)

        for c in range(nchunk):
            rows = slice(c * chunk, (c + 1) * chunk)
            if split:
                b = b_scr[rows, :]
                half = 0.5 * b[chunk - 1:chunk, :]
                x = b - half
                e_q = jnp.exp(x)
                e_k = jnp.exp(-x)
                e_half = jnp.exp(half)
            else:
                e_scr[...] = jnp.exp(_dot(sums_ref[...], _split3(lf_scr[rows, :])))
            for hh in range(nh):
                cols = slice(hh * HEAD_DIM, (hh + 1) * HEAD_DIM)
                vh = v_scr[rows, cols]
                if split:
                    qt = q_scr[rows, cols] * e_q[:, cols]
                    kt = k_scr[rows, cols] * e_k[:, cols]
                    eh = e_half[:, cols]
                    sc = jnp.where(lid >= 0, _dot_nt(qt.astype(BF16), kt.astype(BF16)), 0.0)
                    carry_state(rows, cols, head0 + hh, _dot(sc.astype(BF16), vh),
                                (qt * eh).astype(BF16), (kt * eh).astype(BF16), vh, [eh * eh])
                else:
                    qh = q_scr[rows, cols]
                    kh = k_scr[rows, cols]
                    e_pre = e_scr[0:chunk, cols]
                    k_end = (kh * e_scr[chunk:2 * chunk, cols]).astype(BF16)
                    sc = jnp.where(lid == 0, _dot_nt(qh.astype(BF16), kh.astype(BF16)), 0.0)
                    for l in range(nlev):
                        el = e_scr[(2 + l) * chunk:(3 + l) * chunk, cols]
                        sc_l = _dot_nt((qh * el).astype(BF16), (kh * el).astype(BF16))
                        sc = jnp.where(lid == l + 1, sc_l, sc)
                    e_end = [e_pre[(s + 1) * seg - 1:(s + 1) * seg, :] for s in range(nseg)]
                    carry_state(rows, cols, head0 + hh, _dot(sc.astype(BF16), vh),
                                (qh * e_pre).astype(BF16), k_end, vh, e_end)
        acc_scr[...] += _dot(on_scr[...], wo_ref[...])

    def both(p, split):
        project(p)
        scan(1 - p, split)

    for p in (0, 1):
        mine = (j % 2) == p
        inner = jnp.logical_and(mine, jnp.logical_and(j >= 1, j < nhg))
        last = jnp.logical_and(mine, j == nhg)
        if p == 0:
            pl.when(j == 0)(functools.partial(project, 0))
        if nseg == 1:
            benign = worst_smem[1 - p] <= MAX_SPLIT_DECAY
            pl.when(jnp.logical_and(inner, benign))(functools.partial(both, p, True))
            pl.when(jnp.logical_and(inner, jnp.logical_not(benign)))(functools.partial(both, p, False))
            if nhg % 2 == p:
                pl.when(jnp.logical_and(last, benign))(functools.partial(scan, 1 - p, True))
                pl.when(jnp.logical_and(last, jnp.logical_not(benign)))(functools.partial(scan, 1 - p, False))
        else:
            pl.when(inner)(functools.partial(both, p, False))
            if nhg % 2 == p:
                pl.when(last)(functools.partial(scan, 1 - p, False))

    @pl.when(j == nhg)
    def _():
        _post_norm(x_ref, g3_ref, acc_scr, rs_scr, o_ref, 1.0)

    @pl.when(jnp.logical_and(i == pl.num_programs(0) - 1, j == nhg))
    def _():
        for s in range(nseg):
            for hd in range(heads):
                sout_ref[s, hd] = st_scr[s, hd].T


def _hgrn_mixer(x, g2, g3, w_in, hg_lower, norm_g, w_out, s0, jm, *, layer, tm, chunk, seg):
    t, d = x.shape
    heads = d // HEAD_DIM
    nseg = chunk // seg
    tn = w_in.shape[3]
    nh = tn // HEAD_DIM
    nhg = heads // nh
    assert t % tm == 0 and tm % chunk == 0 and heads % nh == 0 and tm % ROW_CHUNK == 0
    assert s0.shape == (nseg, heads, HEAD_DIM, HEAD_DIM)
    assert nseg == 1 or t == chunk
    sums, level_id, nb = _scan_tables(chunk, seg)
    body = functools.partial(_hgrn_body, layer=layer, chunk=chunk, seg=seg, nh=nh, nb=nb,
                             nchunk=tm // chunk, nhg=nhg)
    row_spec = pl.BlockSpec((tm, d), lambda i, j: (i, 0))
    vec_spec = pl.BlockSpec((1, d), lambda i, j: (0, 0))
    nsec = d // tn
    state_spec = pl.BlockSpec(s0.shape, lambda i, j: (0, 0, 0, 0))

    def proj(j):
        return jnp.minimum(j, nhg - 1)

    def outp(j):
        return jnp.maximum(j - 1, 0)

    scratch_set = [
        pltpu.VMEM((tm, tn), F32),
        pltpu.VMEM((tm, tn), F32),
        pltpu.VMEM((tm, tn), F32),
        pltpu.VMEM((tm, tn), BF16),
        pltpu.VMEM((tm, tn), F32),
        pltpu.VMEM((tm, tn), F32),
    ]
    return pl.pallas_call(
        body,
        out_shape=(jax.ShapeDtypeStruct((t, d), F32), jax.ShapeDtypeStruct(s0.shape, F32)),
        grid=(t // tm, nhg + 1),
        in_specs=[
            row_spec, vec_spec, vec_spec,
            pl.BlockSpec((None, None, d, tn), lambda i, j: (jm, proj(j), 0, 0)),
            pl.BlockSpec((None, None, d, tn), lambda i, j: (jm, nsec + proj(j), 0, 0)),
            pl.BlockSpec((None, None, d, tn), lambda i, j: (jm, 2 * nsec + proj(j), 0, 0)),
            pl.BlockSpec((None, None, d, tn), lambda i, j: (jm, 3 * nsec + proj(j), 0, 0)),
            pl.BlockSpec((hg_lower.shape[0], tn), lambda i, j: (0, proj(j))),
            pl.BlockSpec((1, HEAD_DIM), lambda i, j: (0, 0)),
            pl.BlockSpec((None, tn, d), lambda i, j: (jm, outp(j), 0)),
            pl.BlockSpec(sums.shape, lambda i, j: (0, 0)),
            pl.BlockSpec(level_id.shape, lambda i, j: (0, 0)),
            state_spec,
        ],
        out_specs=(row_spec, state_spec),
        scratch_shapes=scratch_set + scratch_set + [
            pltpu.VMEM((tm, d), BF16),
            pltpu.VMEM((nb * chunk, tn), F32),
            pltpu.VMEM((nseg, heads, HEAD_DIM, HEAD_DIM), F32),
            pltpu.VMEM((tm, tn), BF16),
            pltpu.VMEM((tm, d), F32),
            pltpu.VMEM((tm, LANES), F32),
            pltpu.SMEM((2,), F32),
        ],
        compiler_params=_params(),
    )(x, g2, g3, w_in, w_in, w_in, w_in, hg_lower, norm_g, w_out,
      jnp.asarray(sums, BF16), jnp.asarray(level_id), s0)


def _column_tiles_body(w_ref, o_ref):
    per, _, tn = o_ref.shape
    for t in range(per):
        o_ref[t] = w_ref[:, t * tn:(t + 1) * tn].astype(BF16)


def _column_tiles(w, tn, per):
    *lead, d, n = w.shape
    nl = int(np.prod(lead))
    assert n % (tn * per) == 0
    out = pl.pallas_call(
        _column_tiles_body,
        out_shape=jax.ShapeDtypeStruct((nl, n // tn, d, tn), BF16),
        grid=(nl, n // (tn * per)),
        in_specs=[pl.BlockSpec((None, d, tn * per), lambda l, b: (l, 0, b))],
        out_specs=pl.BlockSpec((None, per, d, tn), lambda l, b: (l, b, 0, 0)),
        compiler_params=_params(),
    )(w.reshape(nl, d, n))
    return out.reshape(*lead, n // tn, d, tn)


def _trunk(x, hg_init, p, ffn_w, *, tm, ffn_tm, sample):
    depth = p["norm_g"].shape[0]
    n_mix = 2
    hg_states, sg_vs = [], []
    order = [(i, k) for i in range(depth) for k in range(2)]

    def ffn(x, i, k, g_pre, g_post):
        nxt = order.index((i, k)) + 1
        side = ()
        if nxt < len(order) and order[nxt] not in ffn_w:
            side = ((p["ffn_w_up_f32"], order[nxt], UP_CAST_ROWS), (p["ffn_w_down_f32"], order[nxt], DOWN_CAST_ROWS))
        x, converted = _ffn(x, g_pre, g_post, *ffn_w[(i, k)], tm=ffn_tm, side=side)
        if side:
            ffn_w[order[nxt]] = converted
        return x

    for i in range(depth):
        g = p["norm_g"][i]
        row = lambda r: g[r:r + 1]
        jm = i // n_mix
        x = ffn(x, i, 0, row(0), row(1))
        if i % n_mix == 0:
            w_s, b_s = p["sg_w_s"][jm], p["sg_b_s"][jm]
            if sample is not None:
                n_streams, seq = sample
                w_s = jnp.tile(w_s[:, :seq, :seq], (1, n_streams, n_streams))
                b_s = jnp.tile(b_s[:, :seq], (1, n_streams))
            bias_rows = jnp.repeat(b_s.T, LANES, axis=1)
            res = _sg_mixer(x, row(2), row(3), p["sg_w_in"], p["sg_ln_g"][jm:jm + 1],
                            p["sg_ln_b"][jm:jm + 1], w_s, bias_rows, p["sg_w_out"], jm,
                            tm=tm, seg_mode=None if sample is None else sample[1],
                            emit_v=sample is not None)
            if sample is not None:
                x, v_rows = res
                sg_vs.append(v_rows)
            else:
                x = res
        else:
            x, st = _hgrn_mixer(x, row(2), row(3), p["hg_w_in"], p["hg_lower"],
                                p["hg_norm_g"][jm:jm + 1], p["hg_w_out"], hg_init[jm], jm,
                                layer=i, tm=tm, chunk=SG_CHUNK if sample is None else x.shape[0],
                                seg=SG_CHUNK if sample is None else sample[1])
            hg_states.append(st)
        x = ffn(x, i, 1, row(4), row(5))
    return x, hg_states, sg_vs


def kernel(x_prompt, x_sample, state_hgrn, norm_g, ffn_w_up, ffn_w_down, sg_w_in, sg_ln_g, sg_ln_b,
           sg_w_s, sg_b_s, sg_w_out, hg_w_in, hg_lower, hg_norm_g, hg_w_out):
    batch, seq, d = x_prompt.shape
    dec_batch, dec_seq, _ = x_sample.shape
    n_hg = hg_w_in.shape[0]
    heads = d // HEAD_DIM
    p = dict(norm_g=norm_g, ffn_w_up_f32=ffn_w_up, ffn_w_down_f32=ffn_w_down,
             sg_w_in=_column_tiles(sg_w_in, SG_TILE, 2), sg_ln_g=sg_ln_g, sg_ln_b=sg_ln_b, sg_w_s=sg_w_s,
             sg_b_s=sg_b_s, sg_w_out=sg_w_out.astype(BF16), hg_w_in=_column_tiles(hg_w_in, HG_TILE, 4),
             hg_lower=hg_lower, hg_norm_g=hg_norm_g, hg_w_out=hg_w_out.astype(BF16))

    ffn_w = {(0, 0): (ffn_w_up[0, 0].astype(BF16), ffn_w_down[0, 0].astype(BF16))}
    ys, sts = [], []
    for b in range(batch):
        init = [jnp.zeros((1, heads, HEAD_DIM, HEAD_DIM), F32) for _ in range(n_hg)]
        y, hg_p, _ = _trunk(x_prompt[b], init, p, ffn_w, tm=512, ffn_tm=1024, sample=None)
        ys.append(y)
        sts.append(jnp.stack([s[0] for s in hg_p], axis=0))
    y_prompt = jnp.stack(ys, axis=0)
    state_hgrn_prompt = jnp.stack(sts, axis=1)

    init = [state_hgrn[jm].astype(F32) for jm in range(n_hg)]
    y, hg_s, sg_v = _trunk(x_sample.reshape(dec_batch * dec_seq, d), init, p, ffn_w,
                           tm=dec_batch * dec_seq, ffn_tm=dec_batch * dec_seq, sample=(dec_batch, dec_seq))
    y_sample = y.reshape(dec_batch, dec_seq, d)
    state_hgrn_sample = jnp.stack(hg_s, axis=0)
    state_sg_v_sample = jnp.stack([v.reshape(dec_batch, dec_seq, d) for v in sg_v], axis=0)
    return (y_prompt, y_sample, state_hgrn_prompt, state_hgrn_sample, state_sg_v_sample)
```

```python
import functools

import numpy as np
import jax
import jax.numpy as jnp
from jax import lax
from jax.experimental import pallas as pl
from jax.experimental.pallas import tpu as pltpu

EPS = 1e-6
F32 = jnp.float32
BF16 = jnp.bfloat16

LANES = 128
ROW_CHUNK = 16
HEAD_DIM = 128
SG_CHUNK = 128
FFN_TILE = 512
SG_TILE = 512
UP_CAST_ROWS = 16
MIX_CAST_ROWS = 16
DOWN_CAST_ROWS = 64
HG_TILE = 256
STREAM_CHUNK = 64
V7X_VMEM_BYTES = 64 * 1024 * 1024
VMEM_LIMIT_BYTES = V7X_VMEM_BYTES - 8 * 1024 * 1024
MAX_SPLIT_DECAY = 140.0


def _dot(a, b):
    return jnp.dot(a, b, preferred_element_type=F32)


def _dot_nt(a, b):
    return lax.dot_general(a, b, (((1,), (1,)), ((), ())), preferred_element_type=F32)


def _dot_tn(a, b):
    return lax.dot_general(a, b, (((0,), (0,)), ((), ())), preferred_element_type=F32)


def _rms(x, g):
    return x * lax.rsqrt(jnp.mean(x * x, axis=-1, keepdims=True) + EPS) * g


def _silu(x):
    return x * jax.nn.sigmoid(x)


def _gelu(x):
    return 0.5 * x * (1.0 + lax.erf(x * np.float32(np.sqrt(0.5))))


def _zero_after(v):
    u = pltpu.bitcast(v, jnp.uint32)
    u = lax.shift_right_logical(lax.shift_right_logical(u, jnp.uint32(16)), jnp.uint32(16))
    return u.astype(F32)


def _for_tiles(n_rows, n_cols, fn):
    def step(c, carry):
        rows = pl.ds(pl.multiple_of(c * ROW_CHUNK, ROW_CHUNK), ROW_CHUNK)
        for ct in range(n_cols // LANES):
            fn(rows, slice(ct * LANES, (ct + 1) * LANES))
        return carry
    lax.fori_loop(0, n_rows // ROW_CHUNK, step, 0, unroll=max(2, 32 * LANES // n_cols))


def _row_rsqrt(x):
    r = lax.rsqrt(jnp.mean(x * x, axis=-1, keepdims=True) + EPS)
    return jnp.broadcast_to(r, (x.shape[0], LANES))


def _pre_norm(x_ref, g_ref, h_scr, acc_scr, rs_scr):
    rs_scr[...] = _row_rsqrt(x_ref[...])

    def tile(rows, cols):
        h_scr[rows, cols] = (x_ref[rows, cols] * rs_scr[rows, :] * g_ref[:, cols]).astype(BF16)
        acc_scr[rows, cols] = jnp.zeros((ROW_CHUNK, LANES), F32)
    _for_tiles(x_ref.shape[0], x_ref.shape[1], tile)


def _post_norm(x_ref, g_ref, acc_scr, rs_scr, o_ref, scale):
    rs_scr[...] = _row_rsqrt(acc_scr[...])

    def tile(rows, cols):
        o_ref[rows, cols] = x_ref[rows, cols] + (acc_scr[rows, cols] * rs_scr[rows, :]) * (scale * g_ref[:, cols])
    _for_tiles(x_ref.shape[0], x_ref.shape[1], tile)


def _params():
    return pltpu.CompilerParams(dimension_semantics=("arbitrary", "arbitrary"),
                                vmem_limit_bytes=VMEM_LIMIT_BYTES)


def _ffn_body(*refs, nt, nslice, nside):
    xn_ref, xp_ref, gpre_ref, gpost_ref, wa_ref, wb_ref, wd_ref = refs[:7]
    side_src = refs[7:7 + nside]
    o_ref = refs[7 + nside]
    side_dst = refs[8 + nside:8 + 2 * nside]
    h0, h1, acc0, acc1 = refs[8 + 2 * nside:]
    r = pl.program_id(0)
    j = pl.program_id(1)
    rs = xn_ref.shape[0]
    base = jnp.minimum(j, nslice - 1) * rs

    def chunk_rows(c):
        return pl.ds(pl.multiple_of(base + c * ROW_CHUNK, ROW_CHUNK), ROW_CHUNK)

    def convert_side():
        for src, dst in zip(side_src, side_dst):
            dst[...] = src[...].astype(BF16)

    def stages(h_pre, h_main, acc_main, acc_post):
        def pre():
            for c in range(rs // ROW_CHUNK):
                xc = xn_ref[c * ROW_CHUNK:(c + 1) * ROW_CHUNK, :]
                h_pre[chunk_rows(c), :] = _rms(xc, gpre_ref[...]).astype(BF16)

        def main(zero=None):
            h = h_main[...]
            a = _dot(h, wa_ref[...])
            if zero is not None:
                a = a + jnp.tile(zero[0:1, :], (1, a.shape[1] // LANES))
            b = _dot(h, wb_ref[...])
            contrib = _dot((_silu(a) * b).astype(BF16), wd_ref[...])
            acc_main[...] = jnp.where(j == 0, contrib, acc_main[...] + contrib)

        def post():
            for c in range(rs // ROW_CHUNK):
                rows = slice(c * ROW_CHUNK, (c + 1) * ROW_CHUNK)
                o_ref[rows, :] = xp_ref[rows, :] + 0.5 * _rms(acc_post[chunk_rows(c), :], gpost_ref[...])

        steady = jnp.logical_and(r >= 2, r < nt)

        @pl.when(steady)
        def _():
            pre()
            post()
            convert_side()
            zero = _zero_after(h_pre[chunk_rows(0), 0:LANES]) + _zero_after(o_ref[0:8, 0:LANES])
            for dst in side_dst:
                zero = zero + _zero_after(dst[0:ROW_CHUNK, 0:LANES])
            main(zero)

        @pl.when(jnp.logical_not(steady))
        def _():
            @pl.when(jnp.logical_and(r == 0, j == 0))
            def _():
                acc0[...] = jnp.zeros_like(acc0)
                acc1[...] = jnp.zeros_like(acc1)
            pl.when(r < nt)(pre)
            pl.when(jnp.logical_and(r >= 1, r <= nt))(main)
            pl.when(r >= 2)(post)
            convert_side()

    pl.when(r % 2 == 0)(lambda: stages(h0, h1, acc1, acc0))
    pl.when(r % 2 == 1)(lambda: stages(h1, h0, acc0, acc1))


def _ffn(x, g_pre, g_post, w_up, w_down, *, tm, side=()):
    t, d = x.shape
    f = w_down.shape[0]
    tf = FFN_TILE
    nf = f // tf
    nt = t // tm
    nslice = min(nf, tm // ROW_CHUNK)
    while tm % (nslice * ROW_CHUNK):
        nslice -= 1
    rs = tm // nslice
    assert t % tm == 0 and f % tf == 0

    def w_step(r, j):
        return jnp.where(r == 0, 0, jnp.where(r > nt, nf - 1, j))

    def next_slice(r, j):
        return jnp.where(r < nt, r * nslice + jnp.minimum(j, nslice - 1), nt * nslice - 1)

    def prev_slice(r, j):
        return jnp.where(r >= 2, (r - 2) * nslice + jnp.minimum(j, nslice - 1), 0)

    side_in, side_out, side_shapes = [], [], []
    for src, lead, rb in side:
        rows, cols = src.shape[-2:]
        nblk = rows // rb
        assert rows % rb == 0 and nblk <= nt * nf and len(lead) == src.ndim - 2

        def blk(r, j, nblk=nblk):
            return jnp.clip((r - 1) * nf + j, 0, nblk - 1)

        side_in.append(pl.BlockSpec((None,) * len(lead) + (rb, cols),
                                    lambda r, j, lead=lead, blk=blk: (*lead, blk(r, j), 0)))
        side_out.append(pl.BlockSpec((rb, cols), lambda r, j, blk=blk: (blk(r, j), 0)))
        side_shapes.append(jax.ShapeDtypeStruct((rows, cols), BF16))

    out = pl.pallas_call(
        functools.partial(_ffn_body, nt=nt, nslice=nslice, nside=len(side)),
        out_shape=[jax.ShapeDtypeStruct((t, d), F32)] + side_shapes,
        grid=(nt + 2, nf),
        in_specs=[
            pl.BlockSpec((rs, d), lambda r, j: (next_slice(r, j), 0)),
            pl.BlockSpec((rs, d), lambda r, j: (prev_slice(r, j), 0)),
            pl.BlockSpec((1, d), lambda r, j: (0, 0)),
            pl.BlockSpec((1, d), lambda r, j: (0, 0)),
            pl.BlockSpec((d, tf), lambda r, j: (0, w_step(r, j))),
            pl.BlockSpec((d, tf), lambda r, j: (0, w_step(r, j) + nf)),
            pl.BlockSpec((tf, d), lambda r, j: (w_step(r, j), 0)),
        ] + side_in,
        out_specs=[pl.BlockSpec((rs, d), lambda r, j: (prev_slice(r, j), 0))] + side_out,
        scratch_shapes=[pltpu.VMEM((tm, d), BF16), pltpu.VMEM((tm, d), BF16),
                        pltpu.VMEM((tm, d), F32), pltpu.VMEM((tm, d), F32)],
        compiler_params=_params(),
    )(x, x, g_pre, g_post, w_up, w_up, w_down, *[src for src, _, _ in side])
    return out[0], tuple(out[1:])


def _sg_mask(seg_mode):
    row = lax.broadcasted_iota(jnp.int32, (SG_CHUNK, SG_CHUNK), 0)
    col = lax.broadcasted_iota(jnp.int32, (SG_CHUNK, SG_CHUNK), 1)
    if seg_mode is None:
        return (row // STREAM_CHUNK) >= (col // STREAM_CHUNK)
    return (row // seg_mode) == (col // seg_mode)


def _sg_body(x_ref, g2_ref, g3_ref, wu_ref, wv_ref, lng_ref, lnb_ref, ws_ref, bias_ref, wo_ref,
             *rest, ng, gpt, nchunk, seg_mode, emit_v):
    if emit_v:
        o_ref, vout_ref, h_scr, u_scr, v_scr, y_scr, acc_scr, rs_scr, mu_scr = rest
    else:
        o_ref, h_scr, u_scr, v_scr, y_scr, acc_scr, rs_scr, mu_scr = rest
        vout_ref = None
    j = pl.program_id(1)
    tn = gpt * LANES
    d = ng * tn
    tm = x_ref.shape[0]

    @pl.when(j == 0)
    def _():
        _pre_norm(x_ref, g2_ref, h_scr, acc_scr, rs_scr)

    @pl.when(j < ng)
    def _():
        h = h_scr[...]
        u_scr[j] = _gelu(_dot(h, wu_ref[...]))
        v_scr[j] = _gelu(_dot(h, wv_ref[...]))

    @pl.when(j == ng)
    def _():
        mu = sum(jnp.sum(v_scr[jj], axis=-1, keepdims=True) for jj in range(ng)) * (1.0 / d)
        var = sum(jnp.sum(jnp.square(v_scr[jj] - mu), axis=-1, keepdims=True)
                  for jj in range(ng)) * (1.0 / d)
        mu_scr[...] = jnp.broadcast_to(mu, (tm, LANES))
        rs_scr[...] = jnp.broadcast_to(lax.rsqrt(var + EPS), (tm, LANES))
        for jj in range(ng):
            def tile(rows, cols, jj=jj):
                gcols = slice(jj * tn + cols.start, jj * tn + cols.stop)
                vn = ((v_scr[jj, rows, cols] - mu_scr[rows, :]) * rs_scr[rows, :] * lng_ref[:, gcols]
                      + lnb_ref[:, gcols])
                v_scr[jj, rows, cols] = vn
                if emit_v:
                    vout_ref[rows, gcols] = vn
            _for_tiles(tm, tn, tile)

    @pl.when(j >= ng)
    def _():
        jj = j - ng
        mask = _sg_mask(seg_mode)
        for g in range(gpt):
            cols = slice(g * LANES, (g + 1) * LANES)
            w = jnp.where(mask, ws_ref[g], 0.0).astype(BF16)
            for n in range(nchunk):
                rows = slice(n * SG_CHUNK, (n + 1) * SG_CHUNK)
                s = _dot(w, v_scr[jj, rows, cols].astype(BF16)) + bias_ref[:, cols]
                y_scr[rows, cols] = (u_scr[jj, rows, cols] * s).astype(BF16)
        acc_scr[...] += _dot(y_scr[...], wo_ref[...])

    @pl.when(j == 2 * ng - 1)
    def _():
        _post_norm(x_ref, g3_ref, acc_scr, rs_scr, o_ref, 1.0)


def _sg_mixer(x, g2, g3, w_in, ln_g, ln_b, w_s, bias_rows, w_out, *, tm, seg_mode, emit_v):
    t, d = x.shape
    tn = SG_TILE
    gpt = tn // LANES
    ng = d // tn
    assert t % tm == 0 and tm % SG_CHUNK == 0 and d % tn == 0 and tm % ROW_CHUNK == 0
    last = ng - 1
    body = functools.partial(_sg_body, ng=ng, gpt=gpt, nchunk=tm // SG_CHUNK,
                             seg_mode=seg_mode, emit_v=emit_v)
    row_spec = pl.BlockSpec((tm, d), lambda i, j: (i, 0))
    vec_spec = pl.BlockSpec((1, d), lambda i, j: (0, 0))
    out_shape = jax.ShapeDtypeStruct((t, d), F32)
    return pl.pallas_call(
        body,
        out_shape=(out_shape, out_shape) if emit_v else out_shape,
        grid=(t // tm, 2 * ng),
        in_specs=[
            row_spec, vec_spec, vec_spec,
            pl.BlockSpec((d, tn), lambda i, j: (0, jnp.minimum(j, last))),
            pl.BlockSpec((d, tn), lambda i, j: (0, ng + jnp.minimum(j, last))),
            vec_spec, vec_spec,
            pl.BlockSpec((gpt, SG_CHUNK, SG_CHUNK), lambda i, j: (jnp.maximum(j - ng, 0), 0, 0)),
            pl.BlockSpec((SG_CHUNK, tn), lambda i, j: (0, jnp.maximum(j - ng, 0))),
            pl.BlockSpec((tn, d), lambda i, j: (jnp.maximum(j - ng, 0), 0)),
        ],
        out_specs=(row_spec, row_spec) if emit_v else row_spec,
        scratch_shapes=[
            pltpu.VMEM((tm, d), BF16),
            pltpu.VMEM((ng, tm, tn), F32),
            pltpu.VMEM((ng, tm, tn), F32),
            pltpu.VMEM((tm, tn), BF16),
            pltpu.VMEM((tm, d), F32),
            pltpu.VMEM((tm, LANES), F32),
            pltpu.VMEM((tm, LANES), F32),
        ],
        compiler_params=_params(),
    )(x, g2, g3, w_in, w_in, ln_g, ln_b, w_s, bias_rows, w_out)


def _scan_tables(chunk, seg):
    levels = []
    m = seg // 2
    while m >= 1:
        levels.append(m)
        m //= 2
    nb = 2 + len(levels)
    t = np.arange(chunk)[:, None]
    j = np.arange(chunk)[None, :]
    same_seg = (t // seg) == (j // seg)
    sums = np.zeros((nb, chunk, chunk), np.float32)
    sums[0] = same_seg & (j <= t)
    sums[1] = same_seg & (j > t)
    level_id = np.where(t == j, 0, -1).astype(np.int32)
    for li, m in enumerate(levels):
        g = 2 * m
        ref = (t // g) * g + m - 1
        odd = (t % g) >= m
        sums[2 + li] = np.where(odd, (j > ref) & (j <= t), (j > t) & (j <= ref))
        pair = ((t // g) == (j // g)) & odd & ((j % g) < m)
        level_id = np.where(pair, li + 1, level_id).astype(np.int32)
    sums = sums.reshape(nb * chunk, chunk)
    return np.concatenate([sums, sums, sums], axis=1), level_id, nb


def _split3(x):
    hi = x.astype(BF16)
    r1 = x - hi.astype(F32)
    mid = r1.astype(BF16)
    lo = (r1 - mid.astype(F32)).astype(BF16)
    return jnp.concatenate([hi, mid, lo], axis=0)


def _hgrn_body(x_ref, g2_ref, g3_ref, wq_ref, wf_ref, wi_ref, wg_ref, hl_ref, ng_ref, wo_ref,
               sums_ref, lid_ref, s0_ref, o_ref, sout_ref, *scratch,
               layer, chunk, seg, nh, nb, nchunk, nhg):
    sets = (scratch[0:6], scratch[6:12])
    h_scr, e_scr, st_scr, on_scr, acc_scr, rs_scr, worst_smem = scratch[12:]
    i = pl.program_id(0)
    j = pl.program_id(1)
    nseg = chunk // seg
    nlev = nb - 2
    heads = nhg * nh
    lid = lid_ref[...]

    @pl.when(j == 0)
    def _():
        _pre_norm(x_ref, g2_ref, h_scr, acc_scr, rs_scr)

    @pl.when(jnp.logical_and(i == 0, j == 0))
    def _():
        for s in range(nseg):
            for hd in range(heads):
                st_scr[s, hd] = s0_ref[s, hd].T

    def project(p):
        q_scr, lf_scr, k_scr, v_scr, gate_scr, b_scr = sets[p]
        hl = hl_ref[...]
        e = jnp.exp(hl - jnp.max(hl, axis=0, keepdims=True))
        pr = e / jnp.sum(e, axis=0, keepdims=True)
        lb = jnp.zeros_like(pr[0:1])
        for r in range(1, layer + 1):
            lb = lb + pr[r:r + 1]
        h = h_scr[...]
        q_scr[...] = _silu(_dot(h, wq_ref[...]))
        f = lb + (1.0 - lb) * jax.nn.sigmoid(_dot(h, wf_ref[...]))
        lf = jnp.log(f)
        lf_scr[...] = lf
        k_scr[...] = 1.0 - f
        v_scr[...] = _dot(h, wi_ref[...]).astype(BF16)
        gate_scr[...] = _silu(_dot(h, wg_ref[...]))
        if nseg == 1:
            worst = jnp.float32(0.0)
            for c in range(nchunk):
                rows = slice(c * chunk, (c + 1) * chunk)
                b = _dot(sums_ref[0:chunk, :], _split3(lf[rows, :]))
                b_scr[rows, :] = b
                worst = jnp.maximum(worst, jnp.max(-b[chunk - 1:chunk, :]))
            worst_smem[p] = worst

    def scan(p, split):
        q_scr, lf_scr, k_scr, v_scr, gate_scr, b_scr = sets[p]
        head0 = (j - 1) * nh

        def carry_state(rows, cols, head, o_intra, q_pre, k_end, vh, e_end):
            inter = []
            for s in range(nseg):
                srows = slice(s * seg, (s + 1) * seg)
                st = st_scr[s, head]
                inter.append(_dot_nt(q_pre[srows], st.astype(BF16)))
                st_scr[s, head] = st * e_end[s] + _dot_tn(vh[srows], k_end[srows])
            o = o_intra + (inter[0] if nseg == 1 else jnp.concatenate(inter, axis=0))
            on_scr[rows, cols] = (_rms(o, ng_ref[...]) * gate_scr[rows, cols]).astype(BF16)

        for c in range(nchunk):
            rows = slice(c * chunk, (c + 1) * chunk)
            if split:
                b = b_scr[rows, :]
                half = 0.5 * b[chunk - 1:chunk, :]
                x = b - half
                e_q = jnp.exp(x)
                e_k = jnp.exp(-x)
                e_half = jnp.exp(half)
            else:
                e_scr[...] = jnp.exp(_dot(sums_ref[...], _split3(lf_scr[rows, :])))
            for hh in range(nh):
                cols = slice(hh * HEAD_DIM, (hh + 1) * HEAD_DIM)
                vh = v_scr[rows, cols]
                if split:
                    qt = q_scr[rows, cols] * e_q[:, cols]
                    kt = k_scr[rows, cols] * e_k[:, cols]
                    eh = e_half[:, cols]
                    sc = jnp.where(lid >= 0, _dot_nt(qt.astype(BF16), kt.astype(BF16)), 0.0)
                    carry_state(rows, cols, head0 + hh, _dot(sc.astype(BF16), vh),
                                (qt * eh).astype(BF16), (kt * eh).astype(BF16), vh, [eh * eh])
                else:
                    qh = q_scr[rows, cols]
                    kh = k_scr[rows, cols]
                    e_pre = e_scr[0:chunk, cols]
                    k_end = (kh * e_scr[chunk:2 * chunk, cols]).astype(BF16)
                    sc = jnp.where(lid == 0, _dot_nt(qh.astype(BF16), kh.astype(BF16)), 0.0)
                    for l in range(nlev):
                        el = e_scr[(2 + l) * chunk:(3 + l) * chunk, cols]
                        sc_l = _dot_nt((qh * el).astype(BF16), (kh * el).astype(BF16))
                        sc = jnp.where(lid == l + 1, sc_l, sc)
                    e_end = [e_pre[(s + 1) * seg - 1:(s + 1) * seg, :] for s in range(nseg)]
                    carry_state(rows, cols, head0 + hh, _dot(sc.astype(BF16), vh),
                                (qh * e_pre).astype(BF16), k_end, vh, e_end)
        acc_scr[...] += _dot(on_scr[...], wo_ref[...])

    def both(p, split):
        project(p)
        scan(1 - p, split)

    for p in (0, 1):
        mine = (j % 2) == p
        inner = jnp.logical_and(mine, jnp.logical_and(j >= 1, j < nhg))
        last = jnp.logical_and(mine, j == nhg)
        if p == 0:
            pl.when(j == 0)(functools.partial(project, 0))
        if nseg == 1:
            benign = worst_smem[1 - p] <= MAX_SPLIT_DECAY
            pl.when(jnp.logical_and(inner, benign))(functools.partial(both, p, True))
            pl.when(jnp.logical_and(inner, jnp.logical_not(benign)))(functools.partial(both, p, False))
            if nhg % 2 == p:
                pl.when(jnp.logical_and(last, benign))(functools.partial(scan, 1 - p, True))
                pl.when(jnp.logical_and(last, jnp.logical_not(benign)))(functools.partial(scan, 1 - p, False))
        else:
            pl.when(inner)(functools.partial(both, p, False))
            if nhg % 2 == p:
                pl.when(last)(functools.partial(scan, 1 - p, False))

    @pl.when(j == nhg)
    def _():
        _post_norm(x_ref, g3_ref, acc_scr, rs_scr, o_ref, 1.0)

    @pl.when(jnp.logical_and(i == pl.num_programs(0) - 1, j == nhg))
    def _():
        for s in range(nseg):
            for hd in range(heads):
                sout_ref[s, hd] = st_scr[s, hd].T


def _hgrn_mixer(x, g2, g3, w_in, hg_lower, norm_g, w_out, s0, *, layer, tm, chunk, seg):
    t, d = x.shape
    heads = d // HEAD_DIM
    nseg = chunk // seg
    tn = HG_TILE
    nh = tn // HEAD_DIM
    nhg = heads // nh
    assert t % tm == 0 and tm % chunk == 0 and heads % nh == 0 and tm % ROW_CHUNK == 0
    assert s0.shape == (nseg, heads, HEAD_DIM, HEAD_DIM)
    assert nseg == 1 or t == chunk
    sums, level_id, nb = _scan_tables(chunk, seg)
    body = functools.partial(_hgrn_body, layer=layer, chunk=chunk, seg=seg, nh=nh, nb=nb,
                             nchunk=tm // chunk, nhg=nhg)
    row_spec = pl.BlockSpec((tm, d), lambda i, j: (i, 0))
    vec_spec = pl.BlockSpec((1, d), lambda i, j: (0, 0))
    nsec = d // tn
    state_spec = pl.BlockSpec(s0.shape, lambda i, j: (0, 0, 0, 0))

    def proj(j):
        return jnp.minimum(j, nhg - 1)

    def outp(j):
        return jnp.maximum(j - 1, 0)

    scratch_set = [
        pltpu.VMEM((tm, tn), F32),
        pltpu.VMEM((tm, tn), F32),
        pltpu.VMEM((tm, tn), F32),
        pltpu.VMEM((tm, tn), BF16),
        pltpu.VMEM((tm, tn), F32),
        pltpu.VMEM((tm, tn), F32),
    ]
    return pl.pallas_call(
        body,
        out_shape=(jax.ShapeDtypeStruct((t, d), F32), jax.ShapeDtypeStruct(s0.shape, F32)),
        grid=(t // tm, nhg + 1),
        in_specs=[
            row_spec, vec_spec, vec_spec,
            pl.BlockSpec((d, tn), lambda i, j: (0, proj(j))),
            pl.BlockSpec((d, tn), lambda i, j: (0, nsec + proj(j))),
            pl.BlockSpec((d, tn), lambda i, j: (0, 2 * nsec + proj(j))),
            pl.BlockSpec((d, tn), lambda i, j: (0, 3 * nsec + proj(j))),
            pl.BlockSpec((hg_lower.shape[0], tn), lambda i, j: (0, proj(j))),
            pl.BlockSpec((1, HEAD_DIM), lambda i, j: (0, 0)),
            pl.BlockSpec((tn, d), lambda i, j: (outp(j), 0)),
            pl.BlockSpec(sums.shape, lambda i, j: (0, 0)),
            pl.BlockSpec(level_id.shape, lambda i, j: (0, 0)),
            state_spec,
        ],
        out_specs=(row_spec, state_spec),
        scratch_shapes=scratch_set + scratch_set + [
            pltpu.VMEM((tm, d), BF16),
            pltpu.VMEM((nb * chunk, tn), F32),
            pltpu.VMEM((nseg, heads, HEAD_DIM, HEAD_DIM), F32),
            pltpu.VMEM((tm, tn), BF16),
            pltpu.VMEM((tm, d), F32),
            pltpu.VMEM((tm, LANES), F32),
            pltpu.SMEM((2,), F32),
        ],
        compiler_params=_params(),
    )(x, g2, g3, w_in, w_in, w_in, w_in, hg_lower, norm_g, w_out,
      jnp.asarray(sums, BF16), jnp.asarray(level_id), s0)


def _trunk(x, hg_init, p, wts, *, tm, ffn_tm, sample):
    depth = p["norm_g"].shape[0]
    n_mix = 2
    hg_states, sg_vs = [], []

    def conversions(keys):
        jobs = []
        for key in keys:
            if key not in wts and key[0] == "ffn" and key[1] < depth:
                jobs.append((key, ((p["ffn_w_up"], key[1:], UP_CAST_ROWS), (p["ffn_w_down"], key[1:], DOWN_CAST_ROWS))))
            elif key not in wts and key[0] in ("sg", "hg"):
                jobs.append((key, ((p[key[0] + "_w_in"], key[1:], MIX_CAST_ROWS), (p[key[0] + "_w_out"], key[1:], MIX_CAST_ROWS))))
        return jobs

    def ffn(x, i, k, g_pre, g_post):
        nxt = ("ffn", i, 1) if k == 0 else ("ffn", i + 1, 0)
        mixer = [(("sg", "hg")[i % n_mix], i // n_mix)] if k == 0 else []
        jobs = conversions([nxt] + mixer)
        side = tuple(job for _, pair in jobs for job in pair)
        x, converted = _ffn(x, g_pre, g_post, *wts[("ffn", i, k)], tm=ffn_tm, side=side)
        for n, (key, _) in enumerate(jobs):
            wts[key] = converted[2 * n:2 * n + 2]
        return x

    for i in range(depth):
        g = p["norm_g"][i]
        row = lambda r: g[r:r + 1]
        jm = i // n_mix
        x = ffn(x, i, 0, row(0), row(1))
        if i % n_mix == 0:
            w_s, b_s = p["sg_w_s"][jm], p["sg_b_s"][jm]
            if sample is not None:
                n_streams, seq = sample
                w_s = jnp.tile(w_s[:, :seq, :seq], (1, n_streams, n_streams))
                b_s = jnp.tile(b_s[:, :seq], (1, n_streams))
            bias_rows = jnp.repeat(b_s.T, LANES, axis=1)
            w_in, w_out = wts[("sg", jm)]
            res = _sg_mixer(x, row(2), row(3), w_in, p["sg_ln_g"][jm:jm + 1],
                            p["sg_ln_b"][jm:jm + 1], w_s, bias_rows, w_out,
                            tm=tm, seg_mode=None if sample is None else sample[1],
                            emit_v=sample is not None)
            if sample is not None:
                x, v_rows = res
                sg_vs.append(v_rows)
            else:
                x = res
        else:
            w_in, w_out = wts[("hg", jm)]
            x, st = _hgrn_mixer(x, row(2), row(3), w_in, p["hg_lower"],
                                p["hg_norm_g"][jm:jm + 1], w_out, hg_init[jm],
                                layer=i, tm=tm, chunk=SG_CHUNK if sample is None else x.shape[0],
                                seg=SG_CHUNK if sample is None else sample[1])
            hg_states.append(st)
        x = ffn(x, i, 1, row(4), row(5))
    return x, hg_states, sg_vs


def kernel(x_prompt, x_sample, state_hgrn, norm_g, ffn_w_up, ffn_w_down, sg_w_in, sg_ln_g, sg_ln_b,
           sg_w_s, sg_b_s, sg_w_out, hg_w_in, hg_lower, hg_norm_g, hg_w_out):
    batch, seq, d = x_prompt.shape
    dec_batch, dec_seq, _ = x_sample.shape
    n_hg = hg_w_in.shape[0]
    heads = d // HEAD_DIM
    p = dict(norm_g=norm_g, ffn_w_up=ffn_w_up, ffn_w_down=ffn_w_down, sg_w_in=sg_w_in, sg_ln_g=sg_ln_g,
             sg_ln_b=sg_ln_b, sg_w_s=sg_w_s, sg_b_s=sg_b_s, sg_w_out=sg_w_out, hg_w_in=hg_w_in,
             hg_lower=hg_lower, hg_norm_g=hg_norm_g, hg_w_out=hg_w_out)
    wts = {("ffn", 0, 0): (ffn_w_up[0, 0].astype(BF16), ffn_w_down[0, 0].astype(BF16))}
    ys, sts = [], []
    for b in range(batch):
        init = [jnp.zeros((1, heads, HEAD_DIM, HEAD_DIM), F32) for _ in range(n_hg)]
        y, hg_p, _ = _trunk(x_prompt[b], init, p, wts, tm=512, ffn_tm=1024, sample=None)
        ys.append(y)
        sts.append(jnp.stack([s[0] for s in hg_p], axis=0))
    y_prompt = jnp.stack(ys, axis=0)
    state_hgrn_prompt = jnp.stack(sts, axis=1)

    init = [state_hgrn[jm].astype(F32) for jm in range(n_hg)]
    y, hg_s, sg_v = _trunk(x_sample.reshape(dec_batch * dec_seq, d), init, p, wts,
                           tm=dec_batch * dec_seq, ffn_tm=dec_batch * dec_seq, sample=(dec_batch, dec_seq))
    y_sample = y.reshape(dec_batch, dec_seq, d)
    state_hgrn_sample = jnp.stack(hg_s, axis=0)
    state_sg_v_sample = jnp.stack([v.reshape(dec_batch, dec_seq, d) for v in sg_v], axis=0)
    return (y_prompt, y_sample, state_hgrn_prompt, state_hgrn_sample, state_sg_v_sample)
```

```python
import functools

import numpy as np
import jax
import jax.numpy as jnp
from jax import lax
from jax.experimental import pallas as pl
from jax.experimental.pallas import tpu as pltpu

EPS = 1e-6
F32 = jnp.float32
BF16 = jnp.bfloat16

LANES = 128
ROW_CHUNK = 16
HEAD_DIM = 128
SG_CHUNK = 128
FFN_TILE = 512
SG_TILE = 512
UP_CAST_ROWS = 16
MIX_CAST_ROWS = 16
DOWN_CAST_ROWS = 64
HG_TILE = 256
STREAM_CHUNK = 64
V7X_VMEM_BYTES = 64 * 1024 * 1024
VMEM_LIMIT_BYTES = V7X_VMEM_BYTES - 8 * 1024 * 1024
MAX_SPLIT_DECAY = 140.0


def _dot(a, b):
    return jnp.dot(a, b, preferred_element_type=F32)


def _dot_nt(a, b):
    return lax.dot_general(a, b, (((1,), (1,)), ((), ())), preferred_element_type=F32)


def _dot_tn(a, b):
    return lax.dot_general(a, b, (((0,), (0,)), ((), ())), preferred_element_type=F32)


def _rms(x, g):
    return x * lax.rsqrt(jnp.mean(x * x, axis=-1, keepdims=True) + EPS) * g


def _silu(x):
    return x * jax.nn.sigmoid(x)


def _gelu(x):
    return 0.5 * x * (1.0 + lax.erf(x * np.float32(np.sqrt(0.5))))


def _zero_after(v):
    u = pltpu.bitcast(v, jnp.uint32)
    u = lax.shift_right_logical(lax.shift_right_logical(u, jnp.uint32(16)), jnp.uint32(16))
    return u.astype(F32)


def _for_tiles(n_rows, n_cols, fn):
    def step(c, carry):
        rows = pl.ds(pl.multiple_of(c * ROW_CHUNK, ROW_CHUNK), ROW_CHUNK)
        for ct in range(n_cols // LANES):
            fn(rows, slice(ct * LANES, (ct + 1) * LANES))
        return carry
    lax.fori_loop(0, n_rows // ROW_CHUNK, step, 0, unroll=max(2, 32 * LANES // n_cols))


def _row_rsqrt(x):
    r = lax.rsqrt(jnp.mean(x * x, axis=-1, keepdims=True) + EPS)
    return jnp.broadcast_to(r, (x.shape[0], LANES))


def _pre_norm(x_ref, g_ref, h_scr, acc_scr, rs_scr):
    rs_scr[...] = _row_rsqrt(x_ref[...])

    def tile(rows, cols):
        h_scr[rows, cols] = (x_ref[rows, cols] * rs_scr[rows, :] * g_ref[:, cols]).astype(BF16)
        acc_scr[rows, cols] = jnp.zeros((ROW_CHUNK, LANES), F32)
    _for_tiles(x_ref.shape[0], x_ref.shape[1], tile)


def _post_norm(x_ref, g_ref, acc_scr, rs_scr, o_ref, scale):
    rs_scr[...] = _row_rsqrt(acc_scr[...])

    def tile(rows, cols):
        o_ref[rows, cols] = x_ref[rows, cols] + (acc_scr[rows, cols] * rs_scr[rows, :]) * (scale * g_ref[:, cols])
    _for_tiles(x_ref.shape[0], x_ref.shape[1], tile)


def _params():
    return pltpu.CompilerParams(dimension_semantics=("arbitrary", "arbitrary"),
                                vmem_limit_bytes=VMEM_LIMIT_BYTES)


def _ffn_body(*refs, nt, nslice, nside):
    xn_ref, xp_ref, gpre_ref, gpost_ref, wa_ref, wb_ref, wd_ref = refs[:7]
    side_src = refs[7:7 + nside]
    o_ref = refs[7 + nside]
    side_dst = refs[8 + nside:8 + 2 * nside]
    h0, h1, acc0, acc1 = refs[8 + 2 * nside:]
    r = pl.program_id(0)
    j = pl.program_id(1)
    rs = xn_ref.shape[0]
    base = jnp.minimum(j, nslice - 1) * rs

    def chunk_rows(c):
        return pl.ds(pl.multiple_of(base + c * ROW_CHUNK, ROW_CHUNK), ROW_CHUNK)

    def convert_side():
        for src, dst in zip(side_src, side_dst):
            dst[...] = src[...].astype(BF16)

    def stages(h_pre, h_main, acc_main, acc_post):
        def pre():
            for c in range(rs // ROW_CHUNK):
                xc = xn_ref[c * ROW_CHUNK:(c + 1) * ROW_CHUNK, :]
                h_pre[chunk_rows(c), :] = _rms(xc, gpre_ref[...]).astype(BF16)

        def main(zero=None):
            h = h_main[...]
            a = _dot(h, wa_ref[...])
            if zero is not None:
                a = a + jnp.tile(zero[0:1, :], (1, a.shape[1] // LANES))
            b = _dot(h, wb_ref[...])
            contrib = _dot((_silu(a) * b).astype(BF16), wd_ref[...])
            acc_main[...] = jnp.where(j == 0, contrib, acc_main[...] + contrib)

        def post():
            for c in range(rs // ROW_CHUNK):
                rows = slice(c * ROW_CHUNK, (c + 1) * ROW_CHUNK)
                o_ref[rows, :] = xp_ref[rows, :] + 0.5 * _rms(acc_post[chunk_rows(c), :], gpost_ref[...])

        steady = jnp.logical_and(r >= 2, r < nt)

        @pl.when(steady)
        def _():
            pre()
            post()
            convert_side()
            zero = _zero_after(h_pre[chunk_rows(0), 0:LANES]) + _zero_after(o_ref[0:8, 0:LANES])
            for dst in side_dst:
                zero = zero + _zero_after(dst[0:ROW_CHUNK, 0:LANES])
            main(zero)

        @pl.when(jnp.logical_not(steady))
        def _():
            @pl.when(jnp.logical_and(r == 0, j == 0))
            def _():
                acc0[...] = jnp.zeros_like(acc0)
                acc1[...] = jnp.zeros_like(acc1)
            pl.when(r < nt)(pre)
            pl.when(jnp.logical_and(r >= 1, r <= nt))(main)
            pl.when(r >= 2)(post)
            convert_side()

    pl.when(r % 2 == 0)(lambda: stages(h0, h1, acc1, acc0))
    pl.when(r % 2 == 1)(lambda: stages(h1, h0, acc0, acc1))


def _ffn(x, g_pre, g_post, w_up, w_down, *, tm, side=()):
    t, d = x.shape
    f = w_down.shape[0]
    tf = FFN_TILE
    nf = f // tf
    nt = t // tm
    nslice = min(nf, tm // ROW_CHUNK)
    while tm % (nslice * ROW_CHUNK):
        nslice -= 1
    rs = tm // nslice
    assert t % tm == 0 and f % tf == 0

    def w_step(r, j):
        return jnp.where(r == 0, 0, jnp.where(r > nt, nf - 1, j))

    def next_slice(r, j):
        return jnp.where(r < nt, r * nslice + jnp.minimum(j, nslice - 1), nt * nslice - 1)

    def prev_slice(r, j):
        return jnp.where(r >= 2, (r - 2) * nslice + jnp.minimum(j, nslice - 1), 0)

    side_in, side_out, side_shapes = [], [], []
    for src, lead, rb in side:
        rows, cols = src.shape[-2:]
        nblk = rows // rb
        assert rows % rb == 0 and nblk <= nt * nf and len(lead) == src.ndim - 2

        def blk(r, j, nblk=nblk):
            return jnp.clip((r - 1) * nf + j, 0, nblk - 1)

        side_in.append(pl.BlockSpec((None,) * len(lead) + (rb, cols),
                                    lambda r, j, lead=lead, blk=blk: (*lead, blk(r, j), 0)))
        side_out.append(pl.BlockSpec((rb, cols), lambda r, j, blk=blk: (blk(r, j), 0)))
        side_shapes.append(jax.ShapeDtypeStruct((rows, cols), BF16))

    out = pl.pallas_call(
        functools.partial(_ffn_body, nt=nt, nslice=nslice, nside=len(side)),
        out_shape=[jax.ShapeDtypeStruct((t, d), F32)] + side_shapes,
        grid=(nt + 2, nf),
        in_specs=[
            pl.BlockSpec((rs, d), lambda r, j: (next_slice(r, j), 0)),
            pl.BlockSpec((rs, d), lambda r, j: (prev_slice(r, j), 0)),
            pl.BlockSpec((1, d), lambda r, j: (0, 0)),
            pl.BlockSpec((1, d), lambda r, j: (0, 0)),
            pl.BlockSpec((d, tf), lambda r, j: (0, w_step(r, j))),
            pl.BlockSpec((d, tf), lambda r, j: (0, w_step(r, j) + nf)),
            pl.BlockSpec((tf, d), lambda r, j: (w_step(r, j), 0)),
        ] + side_in,
        out_specs=[pl.BlockSpec((rs, d), lambda r, j: (prev_slice(r, j), 0))] + side_out,
        scratch_shapes=[pltpu.VMEM((tm, d), BF16), pltpu.VMEM((tm, d), BF16),
                        pltpu.VMEM((tm, d), F32), pltpu.VMEM((tm, d), F32)],
        compiler_params=_params(),
    )(x, x, g_pre, g_post, w_up, w_up, w_down, *[src for src, _, _ in side])
    return out[0], tuple(out[1:])


def _sg_mask(seg_mode):
    row = lax.broadcasted_iota(jnp.int32, (SG_CHUNK, SG_CHUNK), 0)
    col = lax.broadcasted_iota(jnp.int32, (SG_CHUNK, SG_CHUNK), 1)
    if seg_mode is None:
        return (row // STREAM_CHUNK) >= (col // STREAM_CHUNK)
    return (row // seg_mode) == (col // seg_mode)


def _sg_body(x_ref, g2_ref, g3_ref, wu_ref, wv_ref, lng_ref, lnb_ref, ws_ref, bias_ref, wo_ref,
             *rest, ng, gpt, nchunk, seg_mode, emit_v):
    if emit_v:
        o_ref, vout_ref, h_scr, u_scr, v_scr, y_scr, acc_scr, rs_scr = rest
    else:
        o_ref, h_scr, u_scr, v_scr, y_scr, acc_scr, rs_scr = rest
        vout_ref = None
    j = pl.program_id(1)
    tn = gpt * LANES
    d = ng * tn
    tm = x_ref.shape[0]

    @pl.when(j == 0)
    def _():
        _pre_norm(x_ref, g2_ref, h_scr, acc_scr, rs_scr)

    @pl.when(j < ng)
    def _():
        h = h_scr[...]
        u_scr[j] = _gelu(_dot(h, wu_ref[...]))
        v_scr[j] = _gelu(_dot(h, wv_ref[...]))

    @pl.when(j == ng)
    def _():
        mu = sum(jnp.sum(v_scr[jj], axis=-1, keepdims=True) for jj in range(ng)) * (1.0 / d)
        var = sum(jnp.sum(jnp.square(v_scr[jj] - mu), axis=-1, keepdims=True)
                  for jj in range(ng)) * (1.0 / d)
        rstd = lax.rsqrt(var + EPS)
        for jj in range(ng):
            cols = slice(jj * tn, (jj + 1) * tn)
            vn = (v_scr[jj] - mu) * rstd * lng_ref[:, cols] + lnb_ref[:, cols]
            v_scr[jj] = vn
            if emit_v:
                vout_ref[:, cols] = vn

    @pl.when(j >= ng)
    def _():
        jj = j - ng
        mask = _sg_mask(seg_mode)
        for g in range(gpt):
            cols = slice(g * LANES, (g + 1) * LANES)
            w = jnp.where(mask, ws_ref[g], 0.0).astype(BF16)
            for n in range(nchunk):
                rows = slice(n * SG_CHUNK, (n + 1) * SG_CHUNK)
                s = _dot(w, v_scr[jj, rows, cols].astype(BF16)) + bias_ref[:, cols]
                y_scr[rows, cols] = (u_scr[jj, rows, cols] * s).astype(BF16)
        acc_scr[...] += _dot(y_scr[...], wo_ref[...])

    @pl.when(j == 2 * ng - 1)
    def _():
        _post_norm(x_ref, g3_ref, acc_scr, rs_scr, o_ref, 1.0)


def _sg_mixer(x, g2, g3, w_in, ln_g, ln_b, w_s, bias_rows, w_out, *, tm, seg_mode, emit_v):
    t, d = x.shape
    tn = SG_TILE
    gpt = tn // LANES
    ng = d // tn
    assert t % tm == 0 and tm % SG_CHUNK == 0 and d % tn == 0 and tm % ROW_CHUNK == 0
    last = ng - 1
    body = functools.partial(_sg_body, ng=ng, gpt=gpt, nchunk=tm // SG_CHUNK,
                             seg_mode=seg_mode, emit_v=emit_v)
    row_spec = pl.BlockSpec((tm, d), lambda i, j: (i, 0))
    vec_spec = pl.BlockSpec((1, d), lambda i, j: (0, 0))
    out_shape = jax.ShapeDtypeStruct((t, d), F32)
    return pl.pallas_call(
        body,
        out_shape=(out_shape, out_shape) if emit_v else out_shape,
        grid=(t // tm, 2 * ng),
        in_specs=[
            row_spec, vec_spec, vec_spec,
            pl.BlockSpec((d, tn), lambda i, j: (0, jnp.minimum(j, last))),
            pl.BlockSpec((d, tn), lambda i, j: (0, ng + jnp.minimum(j, last))),
            vec_spec, vec_spec,
            pl.BlockSpec((gpt, SG_CHUNK, SG_CHUNK), lambda i, j: (jnp.maximum(j - ng, 0), 0, 0)),
            pl.BlockSpec((SG_CHUNK, tn), lambda i, j: (0, jnp.maximum(j - ng, 0))),
            pl.BlockSpec((tn, d), lambda i, j: (jnp.maximum(j - ng, 0), 0)),
        ],
        out_specs=(row_spec, row_spec) if emit_v else row_spec,
        scratch_shapes=[
            pltpu.VMEM((tm, d), BF16),
            pltpu.VMEM((ng, tm, tn), F32),
            pltpu.VMEM((ng, tm, tn), F32),
            pltpu.VMEM((tm, tn), BF16),
            pltpu.VMEM((tm, d), F32),
            pltpu.VMEM((tm, LANES), F32),
        ],
        compiler_params=_params(),
    )(x, g2, g3, w_in, w_in, ln_g, ln_b, w_s, bias_rows, w_out)


def _scan_tables(chunk, seg):
    levels = []
    m = seg // 2
    while m >= 1:
        levels.append(m)
        m //= 2
    nb = 2 + len(levels)
    t = np.arange(chunk)[:, None]
    j = np.arange(chunk)[None, :]
    same_seg = (t // seg) == (j // seg)
    sums = np.zeros((nb, chunk, chunk), np.float32)
    sums[0] = same_seg & (j <= t)
    sums[1] = same_seg & (j > t)
    level_id = np.where(t == j, 0, -1).astype(np.int32)
    for li, m in enumerate(levels):
        g = 2 * m
        ref = (t // g) * g + m - 1
        odd = (t % g) >= m
        sums[2 + li] = np.where(odd, (j > ref) & (j <= t), (j > t) & (j <= ref))
        pair = ((t // g) == (j // g)) & odd & ((j % g) < m)
        level_id = np.where(pair, li + 1, level_id).astype(np.int32)
    sums = sums.reshape(nb * chunk, chunk)
    return np.concatenate([sums, sums, sums], axis=1), level_id, nb


def _split3(x):
    hi = x.astype(BF16)
    r1 = x - hi.astype(F32)
    mid = r1.astype(BF16)
    lo = (r1 - mid.astype(F32)).astype(BF16)
    return jnp.concatenate([hi, mid, lo], axis=0)


def _hgrn_body(x_ref, g2_ref, g3_ref, wq_ref, wf_ref, wi_ref, wg_ref, hl_ref, ng_ref, wo_ref,
               sums_ref, lid_ref, s0_ref, o_ref, sout_ref, *scratch,
               layer, chunk, seg, nh, nb, nchunk, nhg):
    sets = (scratch[0:6], scratch[6:12])
    h_scr, e_scr, st_scr, on_scr, acc_scr, rs_scr, worst_smem = scratch[12:]
    i = pl.program_id(0)
    j = pl.program_id(1)
    nseg = chunk // seg
    nlev = nb - 2
    heads = nhg * nh
    lid = lid_ref[...]

    @pl.when(j == 0)
    def _():
        _pre_norm(x_ref, g2_ref, h_scr, acc_scr, rs_scr)

    @pl.when(jnp.logical_and(i == 0, j == 0))
    def _():
        for s in range(nseg):
            for hd in range(heads):
                st_scr[s, hd] = s0_ref[s, hd].T

    def project(p):
        q_scr, lf_scr, k_scr, v_scr, gate_scr, b_scr = sets[p]
        hl = hl_ref[...]
        e = jnp.exp(hl - jnp.max(hl, axis=0, keepdims=True))
        pr = e / jnp.sum(e, axis=0, keepdims=True)
        lb = jnp.zeros_like(pr[0:1])
        for r in range(1, layer + 1):
            lb = lb + pr[r:r + 1]
        h = h_scr[...]
        q_scr[...] = _silu(_dot(h, wq_ref[...]))
        f = lb + (1.0 - lb) * jax.nn.sigmoid(_dot(h, wf_ref[...]))
        lf = jnp.log(f)
        lf_scr[...] = lf
        k_scr[...] = 1.0 - f
        v_scr[...] = _dot(h, wi_ref[...]).astype(BF16)
        gate_scr[...] = _silu(_dot(h, wg_ref[...]))
        if nseg == 1:
            worst = jnp.zeros((1, lf.shape[1]), F32)
            for c in range(nchunk):
                rows = slice(c * chunk, (c + 1) * chunk)
                b = _dot(sums_ref[0:chunk, :], _split3(lf[rows, :]))
                b_scr[rows, :] = b
                worst = jnp.maximum(worst, -b[chunk - 1:chunk, :])
            worst_smem[p] = jnp.max(worst)

    def scan(p, split):
        q_scr, lf_scr, k_scr, v_scr, gate_scr, b_scr = sets[p]
        head0 = (j - 1) * nh

        def carry_state(rows, cols, head, o_intra, q_pre, k_end, vh, e_end):
            inter = []
            for s in range(nseg):
                srows = slice(s * seg, (s + 1) * seg)
                st = st_scr[s, head]
                inter.append(_dot_nt(q_pre[srows], st.astype(BF16)))
                st_scr[s, head] = st * e_end[s] + _dot_tn(vh[srows], k_end[srows])
            o = o_intra + (inter[0] if nseg == 1 else jnp.concatenate(inter, axis=0))
            on_scr[rows, cols] = (_rms(o, ng_ref[...]) * gate_scr[rows, cols]).astype(BF16)

        for c in range(nchunk):
            rows = slice(c * chunk, (c + 1) * chunk)
            if split:
                b = b_scr[rows, :]
                half = 0.5 * b[chunk - 1:chunk, :]
                x = b - half
                e_q = jnp.exp(x)
                e_k = jnp.exp(-x)
                e_half = jnp.exp(half)
            else:
                e_scr[...] = jnp.exp(_dot(sums_ref[...], _split3(lf_scr[rows, :])))
            for hh in range(nh):
                cols = slice(hh * HEAD_DIM, (hh + 1) * HEAD_DIM)
                vh = v_scr[rows, cols]
                if split:
                    qt = q_scr[rows, cols] * e_q[:, cols]
                    kt = k_scr[rows, cols] * e_k[:, cols]
                    eh = e_half[:, cols]
                    sc = jnp.where(lid >= 0, _dot_nt(qt.astype(BF16), kt.astype(BF16)), 0.0)
                    carry_state(rows, cols, head0 + hh, _dot(sc.astype(BF16), vh),
                                (qt * eh).astype(BF16), (kt * eh).astype(BF16), vh, [eh * eh])
                else:
                    qh = q_scr[rows, cols]
                    kh = k_scr[rows, cols]
                    e_pre = e_scr[0:chunk, cols]
                    k_end = (kh * e_scr[chunk:2 * chunk, cols]).astype(BF16)
                    sc = jnp.where(lid == 0, _dot_nt(qh.astype(BF16), kh.astype(BF16)), 0.0)
                    for l in range(nlev):
                        el = e_scr[(2 + l) * chunk:(3 + l) * chunk, cols]
                        sc_l = _dot_nt((qh * el).astype(BF16), (kh * el).astype(BF16))
                        sc = jnp.where(lid == l + 1, sc_l, sc)
                    e_end = [e_pre[(s + 1) * seg - 1:(s + 1) * seg, :] for s in range(nseg)]
                    carry_state(rows, cols, head0 + hh, _dot(sc.astype(BF16), vh),
                                (qh * e_pre).astype(BF16), k_end, vh, e_end)
        acc_scr[...] += _dot(on_scr[...], wo_ref[...])

    def both(p, split):
        project(p)
        scan(1 - p, split)

    for p in (0, 1):
        mine = (j % 2) == p
        inner = jnp.logical_and(mine, jnp.logical_and(j >= 1, j < nhg))
        last = jnp.logical_and(mine, j == nhg)
        if p == 0:
            pl.when(j == 0)(functools.partial(project, 0))
        if nseg == 1:
            benign = worst_smem[1 - p] <= MAX_SPLIT_DECAY
            pl.when(jnp.logical_and(inner, benign))(functools.partial(both, p, True))
            pl.when(jnp.logical_and(inner, jnp.logical_not(benign)))(functools.partial(both, p, False))
            if nhg % 2 == p:
                pl.when(jnp.logical_and(last, benign))(functools.partial(scan, 1 - p, True))
                pl.when(jnp.logical_and(last, jnp.logical_not(benign)))(functools.partial(scan, 1 - p, False))
        else:
            pl.when(inner)(functools.partial(both, p, False))
            if nhg % 2 == p:
                pl.when(last)(functools.partial(scan, 1 - p, False))

    @pl.when(j == nhg)
    def _():
        _post_norm(x_ref, g3_ref, acc_scr, rs_scr, o_ref, 1.0)

    @pl.when(jnp.logical_and(i == pl.num_programs(0) - 1, j == nhg))
    def _():
        for s in range(nseg):
            for hd in range(heads):
                sout_ref[s, hd] = st_scr[s, hd].T


def _hgrn_mixer(x, g2, g3, w_in, hg_lower, norm_g, w_out, s0, *, layer, tm, chunk, seg):
    t, d = x.shape
    heads = d // HEAD_DIM
    nseg = chunk // seg
    tn = HG_TILE
    nh = tn // HEAD_DIM
    nhg = heads // nh
    assert t % tm == 0 and tm % chunk == 0 and heads % nh == 0 and tm % ROW_CHUNK == 0
    assert s0.shape == (nseg, heads, HEAD_DIM, HEAD_DIM)
    assert nseg == 1 or t == chunk
    sums, level_id, nb = _scan_tables(chunk, seg)
    body = functools.partial(_hgrn_body, layer=layer, chunk=chunk, seg=seg, nh=nh, nb=nb,
                             nchunk=tm // chunk, nhg=nhg)
    row_spec = pl.BlockSpec((tm, d), lambda i, j: (i, 0))
    vec_spec = pl.BlockSpec((1, d), lambda i, j: (0, 0))
    nsec = d // tn
    state_spec = pl.BlockSpec(s0.shape, lambda i, j: (0, 0, 0, 0))

    def proj(j):
        return jnp.minimum(j, nhg - 1)

    def outp(j):
        return jnp.maximum(j - 1, 0)

    scratch_set = [
        pltpu.VMEM((tm, tn), F32),
        pltpu.VMEM((tm, tn), F32),
        pltpu.VMEM((tm, tn), F32),
        pltpu.VMEM((tm, tn), BF16),
        pltpu.VMEM((tm, tn), F32),
        pltpu.VMEM((tm, tn), F32),
    ]
    return pl.pallas_call(
        body,
        out_shape=(jax.ShapeDtypeStruct((t, d), F32), jax.ShapeDtypeStruct(s0.shape, F32)),
        grid=(t // tm, nhg + 1),
        in_specs=[
            row_spec, vec_spec, vec_spec,
            pl.BlockSpec((d, tn), lambda i, j: (0, proj(j))),
            pl.BlockSpec((d, tn), lambda i, j: (0, nsec + proj(j))),
            pl.BlockSpec((d, tn), lambda i, j: (0, 2 * nsec + proj(j))),
            pl.BlockSpec((d, tn), lambda i, j: (0, 3 * nsec + proj(j))),
            pl.BlockSpec((hg_lower.shape[0], tn), lambda i, j: (0, proj(j))),
            pl.BlockSpec((1, HEAD_DIM), lambda i, j: (0, 0)),
            pl.BlockSpec((tn, d), lambda i, j: (outp(j), 0)),
            pl.BlockSpec(sums.shape, lambda i, j: (0, 0)),
            pl.BlockSpec(level_id.shape, lambda i, j: (0, 0)),
            state_spec,
        ],
        out_specs=(row_spec, state_spec),
        scratch_shapes=scratch_set + scratch_set + [
            pltpu.VMEM((tm, d), BF16),
            pltpu.VMEM((nb * chunk, tn), F32),
            pltpu.VMEM((nseg, heads, HEAD_DIM, HEAD_DIM), F32),
            pltpu.VMEM((tm, tn), BF16),
            pltpu.VMEM((tm, d), F32),
            pltpu.VMEM((tm, LANES), F32),
            pltpu.SMEM((2,), F32),
        ],
        compiler_params=_params(),
    )(x, g2, g3, w_in, w_in, w_in, w_in, hg_lower, norm_g, w_out,
      jnp.asarray(sums, BF16), jnp.asarray(level_id), s0)


def _trunk(x, hg_init, p, wts, *, tm, ffn_tm, sample):
    depth = p["norm_g"].shape[0]
    n_mix = 2
    hg_states, sg_vs = [], []

    def conversions(keys):
        jobs = []
        for key in keys:
            if key not in wts and key[0] == "ffn" and key[1] < depth:
                jobs.append((key, ((p["ffn_w_up"], key[1:], UP_CAST_ROWS), (p["ffn_w_down"], key[1:], DOWN_CAST_ROWS))))
            elif key not in wts and key[0] in ("sg", "hg"):
                jobs.append((key, ((p[key[0] + "_w_in"], key[1:], MIX_CAST_ROWS), (p[key[0] + "_w_out"], key[1:], MIX_CAST_ROWS))))
        return jobs

    def ffn(x, i, k, g_pre, g_post):
        nxt = ("ffn", i, 1) if k == 0 else ("ffn", i + 1, 0)
        mixer = [(("sg", "hg")[i % n_mix], i // n_mix)] if k == 0 else []
        jobs = conversions([nxt] + mixer)
        side = tuple(job for _, pair in jobs for job in pair)
        x, converted = _ffn(x, g_pre, g_post, *wts[("ffn", i, k)], tm=ffn_tm, side=side)
        for n, (key, _) in enumerate(jobs):
            wts[key] = converted[2 * n:2 * n + 2]
        return x

    for i in range(depth):
        g = p["norm_g"][i]
        row = lambda r: g[r:r + 1]
        jm = i // n_mix
        x = ffn(x, i, 0, row(0), row(1))
        if i % n_mix == 0:
            w_s, b_s = p["sg_w_s"][jm], p["sg_b_s"][jm]
            if sample is not None:
                n_streams, seq = sample
                w_s = jnp.tile(w_s[:, :seq, :seq], (1, n_streams, n_streams))
                b_s = jnp.tile(b_s[:, :seq], (1, n_streams))
            bias_rows = jnp.repeat(b_s.T, LANES, axis=1)
            w_in, w_out = wts[("sg", jm)]
            res = _sg_mixer(x, row(2), row(3), w_in, p["sg_ln_g"][jm:jm + 1],
                            p["sg_ln_b"][jm:jm + 1], w_s, bias_rows, w_out,
                            tm=tm, seg_mode=None if sample is None else sample[1],
                            emit_v=sample is not None)
            if sample is not None:
                x, v_rows = res
                sg_vs.append(v_rows)
            else:
                x = res
        else:
            w_in, w_out = wts[("hg", jm)]
            x, st = _hgrn_mixer(x, row(2), row(3), w_in, p["hg_lower"],
                                p["hg_norm_g"][jm:jm + 1], w_out, hg_init[jm],
                                layer=i, tm=tm, chunk=SG_CHUNK if sample is None else x.shape[0],
                                seg=SG_CHUNK if sample is None else sample[1])
            hg_states.append(st)
        x = ffn(x, i, 1, row(4), row(5))
    return x, hg_states, sg_vs


def kernel(x_prompt, x_sample, state_hgrn, norm_g, ffn_w_up, ffn_w_down, sg_w_in, sg_ln_g, sg_ln_b,
           sg_w_s, sg_b_s, sg_w_out, hg_w_in, hg_lower, hg_norm_g, hg_w_out):
    batch, seq, d = x_prompt.shape
    dec_batch, dec_seq, _ = x_sample.shape
    n_hg = hg_w_in.shape[0]
    heads = d // HEAD_DIM
    p = dict(norm_g=norm_g, ffn_w_up=ffn_w_up, ffn_w_down=ffn_w_down, sg_w_in=sg_w_in, sg_ln_g=sg_ln_g,
             sg_ln_b=sg_ln_b, sg_w_s=sg_w_s, sg_b_s=sg_b_s, sg_w_out=sg_w_out, hg_w_in=hg_w_in,
             hg_lower=hg_lower, hg_norm_g=hg_norm_g, hg_w_out=hg_w_out)
    wts = {("ffn", 0, 0): (ffn_w_up[0, 0].astype(BF16), ffn_w_down[0, 0].astype(BF16))}
    ys, sts = [], []
    for b in range(batch):
        init = [jnp.zeros((1, heads, HEAD_DIM, HEAD_DIM), F32) for _ in range(n_hg)]
        y, hg_p, _ = _trunk(x_prompt[b], init, p, wts, tm=512, ffn_tm=1024, sample=None)
        ys.append(y)
        sts.append(jnp.stack([s[0] for s in hg_p], axis=0))
    y_prompt = jnp.stack(ys, axis=0)
    state_hgrn_prompt = jnp.stack(sts, axis=1)

    init = [state_hgrn[jm].astype(F32) for jm in range(n_hg)]
    y, hg_s, sg_v = _trunk(x_sample.reshape(dec_batch * dec_seq, d), init, p, wts,
                           tm=dec_batch * dec_seq, ffn_tm=dec_batch * dec_seq, sample=(dec_batch, dec_seq))
    y_sample = y.reshape(dec_batch, dec_seq, d)
    state_hgrn_sample = jnp.stack(hg_s, axis=0)
    state_sg_v_sample = jnp.stack([v.reshape(dec_batch, dec_seq, d) for v in sg_v], axis=0)
    return (y_prompt, y_sample, state_hgrn_prompt, state_hgrn_sample, state_sg_v_sample)
```

```python
import functools

import numpy as np
import jax
import jax.numpy as jnp
from jax import lax
from jax.experimental import pallas as pl
from jax.experimental.pallas import tpu as pltpu

EPS = 1e-6
F32 = jnp.float32
BF16 = jnp.bfloat16

LANES = 128
ROW_CHUNK = 16
HEAD_DIM = 128
SG_CHUNK = 128
FFN_TILE = 512
SG_TILE = 512
UP_CAST_ROWS = 16
MIX_CAST_ROWS = 16
DOWN_CAST_ROWS = 64
HG_TILE = 256
STREAM_CHUNK = 64
V7X_VMEM_BYTES = 64 * 1024 * 1024
VMEM_LIMIT_BYTES = V7X_VMEM_BYTES - 8 * 1024 * 1024
MAX_SPLIT_DECAY = 140.0


def _dot(a, b):
    return jnp.dot(a, b, preferred_element_type=F32)


def _dot_nt(a, b):
    return lax.dot_general(a, b, (((1,), (1,)), ((), ())), preferred_element_type=F32)


def _dot_tn(a, b):
    return lax.dot_general(a, b, (((0,), (0,)), ((), ())), preferred_element_type=F32)


def _rms(x, g):
    return x * lax.rsqrt(jnp.mean(x * x, axis=-1, keepdims=True) + EPS) * g


def _silu(x):
    return x * jax.nn.sigmoid(x)


def _gelu(x):
    return 0.5 * x * (1.0 + lax.erf(x * np.float32(np.sqrt(0.5))))


def _zero_after(v):
    u = pltpu.bitcast(v, jnp.uint32)
    u = lax.shift_right_logical(lax.shift_right_logical(u, jnp.uint32(16)), jnp.uint32(16))
    return u.astype(F32)


def _for_tiles(n_rows, n_cols, fn):
    def step(c, carry):
        rows = pl.ds(pl.multiple_of(c * ROW_CHUNK, ROW_CHUNK), ROW_CHUNK)
        for ct in range(n_cols // LANES):
            fn(rows, slice(ct * LANES, (ct + 1) * LANES))
        return carry
    lax.fori_loop(0, n_rows // ROW_CHUNK, step, 0, unroll=max(2, 32 * LANES // n_cols))


def _row_rsqrt(x):
    r = lax.rsqrt(jnp.mean(x * x, axis=-1, keepdims=True) + EPS)
    return jnp.broadcast_to(r, (x.shape[0], LANES))


def _pre_norm(x_ref, g_ref, h_scr, acc_scr, rs_scr):
    rs_scr[...] = _row_rsqrt(x_ref[...])

    def tile(rows, cols):
        h_scr[rows, cols] = (x_ref[rows, cols] * rs_scr[rows, :] * g_ref[:, cols]).astype(BF16)
        acc_scr[rows, cols] = jnp.zeros((ROW_CHUNK, LANES), F32)
    _for_tiles(x_ref.shape[0], x_ref.shape[1], tile)


def _post_norm(x_ref, g_ref, acc_scr, rs_scr, o_ref, scale):
    rs_scr[...] = _row_rsqrt(acc_scr[...])

    def tile(rows, cols):
        o_ref[rows, cols] = x_ref[rows, cols] + (acc_scr[rows, cols] * rs_scr[rows, :]) * (scale * g_ref[:, cols])
    _for_tiles(x_ref.shape[0], x_ref.shape[1], tile)


def _params(grid_rank=2):
    return pltpu.CompilerParams(dimension_semantics=("arbitrary",) * grid_rank,
                                vmem_limit_bytes=VMEM_LIMIT_BYTES)


def _ffn_body(*refs, nt, nslice, nside):
    xn_ref, xp_ref, gpre_ref, gpost_ref, wa_ref, wb_ref, wd_ref = refs[:7]
    side_src = refs[7:7 + nside]
    o_ref = refs[7 + nside]
    side_dst = refs[8 + nside:8 + 2 * nside]
    h0, h1, acc0, acc1 = refs[8 + 2 * nside:]
    r = pl.program_id(0)
    j = pl.program_id(1)
    rs = xn_ref.shape[0]
    base = jnp.minimum(j, nslice - 1) * rs

    def chunk_rows(c):
        return pl.ds(pl.multiple_of(base + c * ROW_CHUNK, ROW_CHUNK), ROW_CHUNK)

    def convert_side():
        for src, dst in zip(side_src, side_dst):
            dst[...] = src[...].astype(BF16)

    def stages(h_pre, h_main, acc_main, acc_post):
        def pre():
            for c in range(rs // ROW_CHUNK):
                xc = xn_ref[c * ROW_CHUNK:(c + 1) * ROW_CHUNK, :]
                h_pre[chunk_rows(c), :] = _rms(xc, gpre_ref[...]).astype(BF16)

        def main(zero=None):
            h = h_main[...]
            a = _dot(h, wa_ref[...])
            if zero is not None:
                a = a + jnp.tile(zero[0:1, :], (1, a.shape[1] // LANES))
            b = _dot(h, wb_ref[...])
            contrib = _dot((_silu(a) * b).astype(BF16), wd_ref[...])
            acc_main[...] = jnp.where(j == 0, contrib, acc_main[...] + contrib)

        def post():
            for c in range(rs // ROW_CHUNK):
                rows = slice(c * ROW_CHUNK, (c + 1) * ROW_CHUNK)
                o_ref[rows, :] = xp_ref[rows, :] + 0.5 * _rms(acc_post[chunk_rows(c), :], gpost_ref[...])

        steady = jnp.logical_and(r >= 2, r < nt)

        @pl.when(steady)
        def _():
            pre()
            post()
            convert_side()
            zero = _zero_after(h_pre[chunk_rows(0), 0:LANES]) + _zero_after(o_ref[0:8, 0:LANES])
            for dst in side_dst:
                zero = zero + _zero_after(dst[0:ROW_CHUNK, 0:LANES])
            main(zero)

        @pl.when(jnp.logical_not(steady))
        def _():
            @pl.when(jnp.logical_and(r == 0, j == 0))
            def _():
                acc0[...] = jnp.zeros_like(acc0)
                acc1[...] = jnp.zeros_like(acc1)
            pl.when(r < nt)(pre)
            pl.when(jnp.logical_and(r >= 1, r <= nt))(main)
            pl.when(r >= 2)(post)
            convert_side()

    pl.when(r % 2 == 0)(lambda: stages(h0, h1, acc1, acc0))
    pl.when(r % 2 == 1)(lambda: stages(h1, h0, acc0, acc1))


def _ffn_single_body(x_ref, gpre_ref, gpost_ref, wa_ref, wb_ref, wd_ref, o_ref, h_scr, acc_scr):
    j = pl.program_id(0)

    @pl.when(j == 0)
    def _():
        h_scr[...] = _rms(x_ref[...], gpre_ref[...]).astype(BF16)
        acc_scr[...] = jnp.zeros_like(acc_scr)

    h = h_scr[...]
    a = _dot(h, wa_ref[...])
    b = _dot(h, wb_ref[...])
    acc_scr[...] += _dot((_silu(a) * b).astype(BF16), wd_ref[...])

    @pl.when(j == pl.num_programs(0) - 1)
    def _():
        o_ref[...] = x_ref[...] + 0.5 * _rms(acc_scr[...], gpost_ref[...])


def _ffn(x, g_pre, g_post, w_up, w_down, *, tm, side=()):
    t, d = x.shape
    f = w_down.shape[0]
    tf = FFN_TILE
    nf = f // tf
    nt = t // tm
    nslice = min(nf, tm // ROW_CHUNK)
    while tm % (nslice * ROW_CHUNK):
        nslice -= 1
    rs = tm // nslice
    assert t % tm == 0 and f % tf == 0
    if nt == 1 and not side:
        out = pl.pallas_call(
            _ffn_single_body,
            out_shape=jax.ShapeDtypeStruct((t, d), F32),
            grid=(nf,),
            in_specs=[
                pl.BlockSpec((t, d), lambda j: (0, 0)),
                pl.BlockSpec((1, d), lambda j: (0, 0)),
                pl.BlockSpec((1, d), lambda j: (0, 0)),
                pl.BlockSpec((d, tf), lambda j: (0, j)),
                pl.BlockSpec((d, tf), lambda j: (0, j + nf)),
                pl.BlockSpec((tf, d), lambda j: (j, 0)),
            ],
            out_specs=pl.BlockSpec((t, d), lambda j: (0, 0)),
            scratch_shapes=[pltpu.VMEM((t, d), BF16), pltpu.VMEM((t, d), F32)],
            compiler_params=_params(1),
        )(x, g_pre, g_post, w_up, w_up, w_down)
        return out, ()

    def w_step(r, j):
        return jnp.where(r == 0, 0, jnp.where(r > nt, nf - 1, j))

    def next_slice(r, j):
        return jnp.where(r < nt, r * nslice + jnp.minimum(j, nslice - 1), nt * nslice - 1)

    def prev_slice(r, j):
        return jnp.where(r >= 2, (r - 2) * nslice + jnp.minimum(j, nslice - 1), 0)

    side_in, side_out, side_shapes = [], [], []
    for src, lead, rb in side:
        rows, cols = src.shape[-2:]
        nblk = rows // rb
        assert rows % rb == 0 and nblk <= nt * nf and len(lead) == src.ndim - 2

        def blk(r, j, nblk=nblk):
            return jnp.clip((r - 1) * nf + j, 0, nblk - 1)

        side_in.append(pl.BlockSpec((None,) * len(lead) + (rb, cols),
                                    lambda r, j, lead=lead, blk=blk: (*lead, blk(r, j), 0)))
        side_out.append(pl.BlockSpec((rb, cols), lambda r, j, blk=blk: (blk(r, j), 0)))
        side_shapes.append(jax.ShapeDtypeStruct((rows, cols), BF16))

    out = pl.pallas_call(
        functools.partial(_ffn_body, nt=nt, nslice=nslice, nside=len(side)),
        out_shape=[jax.ShapeDtypeStruct((t, d), F32)] + side_shapes,
        grid=(nt + 2, nf),
        in_specs=[
            pl.BlockSpec((rs, d), lambda r, j: (next_slice(r, j), 0)),
            pl.BlockSpec((rs, d), lambda r, j: (prev_slice(r, j), 0)),
            pl.BlockSpec((1, d), lambda r, j: (0, 0)),
            pl.BlockSpec((1, d), lambda r, j: (0, 0)),
            pl.BlockSpec((d, tf), lambda r, j: (0, w_step(r, j))),
            pl.BlockSpec((d, tf), lambda r, j: (0, w_step(r, j) + nf)),
            pl.BlockSpec((tf, d), lambda r, j: (w_step(r, j), 0)),
        ] + side_in,
        out_specs=[pl.BlockSpec((rs, d), lambda r, j: (prev_slice(r, j), 0))] + side_out,
        scratch_shapes=[pltpu.VMEM((tm, d), BF16), pltpu.VMEM((tm, d), BF16),
                        pltpu.VMEM((tm, d), F32), pltpu.VMEM((tm, d), F32)],
        compiler_params=_params(),
    )(x, x, g_pre, g_post, w_up, w_up, w_down, *[src for src, _, _ in side])
    return out[0], tuple(out[1:])


def _sg_mask(seg_mode):
    row = lax.broadcasted_iota(jnp.int32, (SG_CHUNK, SG_CHUNK), 0)
    col = lax.broadcasted_iota(jnp.int32, (SG_CHUNK, SG_CHUNK), 1)
    if seg_mode is None:
        return (row // STREAM_CHUNK) >= (col // STREAM_CHUNK)
    return (row // seg_mode) == (col // seg_mode)


def _sg_body(x_ref, g2_ref, g3_ref, wu_ref, wv_ref, lng_ref, lnb_ref, ws_ref, bias_ref, wo_ref,
             *rest, ng, gpt, nchunk, seg_mode, emit_v):
    if emit_v:
        o_ref, vout_ref, h_scr, u_scr, v_scr, y_scr, acc_scr, rs_scr = rest
    else:
        o_ref, h_scr, u_scr, v_scr, y_scr, acc_scr, rs_scr = rest
        vout_ref = None
    j = pl.program_id(1)
    tn = gpt * LANES
    d = ng * tn
    tm = x_ref.shape[0]

    @pl.when(j == 0)
    def _():
        _pre_norm(x_ref, g2_ref, h_scr, acc_scr, rs_scr)

    @pl.when(j < ng)
    def _():
        h = h_scr[...]
        u_scr[j] = _gelu(_dot(h, wu_ref[...]))
        v_scr[j] = _gelu(_dot(h, wv_ref[...]))

    @pl.when(j == ng)
    def _():
        mu = sum(jnp.sum(v_scr[jj], axis=-1, keepdims=True) for jj in range(ng)) * (1.0 / d)
        var = sum(jnp.sum(jnp.square(v_scr[jj] - mu), axis=-1, keepdims=True)
                  for jj in range(ng)) * (1.0 / d)
        rstd = lax.rsqrt(var + EPS)
        for jj in range(ng):
            cols = slice(jj * tn, (jj + 1) * tn)
            vn = (v_scr[jj] - mu) * rstd * lng_ref[:, cols] + lnb_ref[:, cols]
            v_scr[jj] = vn
            if emit_v:
                vout_ref[:, cols] = vn

    @pl.when(j >= ng)
    def _():
        jj = j - ng
        mask = _sg_mask(seg_mode)
        for g in range(gpt):
            cols = slice(g * LANES, (g + 1) * LANES)
            w = jnp.where(mask, ws_ref[g], 0.0).astype(BF16)
            for n in range(nchunk):
                rows = slice(n * SG_CHUNK, (n + 1) * SG_CHUNK)
                s = _dot(w, v_scr[jj, rows, cols].astype(BF16)) + bias_ref[:, cols]
                y_scr[rows, cols] = (u_scr[jj, rows, cols] * s).astype(BF16)
        acc_scr[...] += _dot(y_scr[...], wo_ref[...])

    @pl.when(j == 2 * ng - 1)
    def _():
        _post_norm(x_ref, g3_ref, acc_scr, rs_scr, o_ref, 1.0)


def _sg_mixer(x, g2, g3, w_in, ln_g, ln_b, w_s, bias_rows, w_out, *, tm, seg_mode, emit_v):
    t, d = x.shape
    tn = SG_TILE
    gpt = tn // LANES
    ng = d // tn
    assert t % tm == 0 and tm % SG_CHUNK == 0 and d % tn == 0 and tm % ROW_CHUNK == 0
    last = ng - 1
    body = functools.partial(_sg_body, ng=ng, gpt=gpt, nchunk=tm // SG_CHUNK,
                             seg_mode=seg_mode, emit_v=emit_v)
    row_spec = pl.BlockSpec((tm, d), lambda i, j: (i, 0))
    vec_spec = pl.BlockSpec((1, d), lambda i, j: (0, 0))
    out_shape = jax.ShapeDtypeStruct((t, d), F32)
    return pl.pallas_call(
        body,
        out_shape=(out_shape, out_shape) if emit_v else out_shape,
        grid=(t // tm, 2 * ng),
        in_specs=[
            row_spec, vec_spec, vec_spec,
            pl.BlockSpec((d, tn), lambda i, j: (0, jnp.minimum(j, last))),
            pl.BlockSpec((d, tn), lambda i, j: (0, ng + jnp.minimum(j, last))),
            vec_spec, vec_spec,
            pl.BlockSpec((gpt, SG_CHUNK, SG_CHUNK), lambda i, j: (jnp.maximum(j - ng, 0), 0, 0)),
            pl.BlockSpec((SG_CHUNK, tn), lambda i, j: (0, jnp.maximum(j - ng, 0))),
            pl.BlockSpec((tn, d), lambda i, j: (jnp.maximum(j - ng, 0), 0)),
        ],
        out_specs=(row_spec, row_spec) if emit_v else row_spec,
        scratch_shapes=[
            pltpu.VMEM((tm, d), BF16),
            pltpu.VMEM((ng, tm, tn), F32),
            pltpu.VMEM((ng, tm, tn), F32),
            pltpu.VMEM((tm, tn), BF16),
            pltpu.VMEM((tm, d), F32),
            pltpu.VMEM((tm, LANES), F32),
        ],
        compiler_params=_params(),
    )(x, g2, g3, w_in, w_in, ln_g, ln_b, w_s, bias_rows, w_out)


def _scan_tables(chunk, seg):
    levels = []
    m = seg // 2
    while m >= 1:
        levels.append(m)
        m //= 2
    nb = 2 + len(levels)
    t = np.arange(chunk)[:, None]
    j = np.arange(chunk)[None, :]
    same_seg = (t // seg) == (j // seg)
    sums = np.zeros((nb, chunk, chunk), np.float32)
    sums[0] = same_seg & (j <= t)
    sums[1] = same_seg & (j > t)
    level_id = np.where(t == j, 0, -1).astype(np.int32)
    for li, m in enumerate(levels):
        g = 2 * m
        ref = (t // g) * g + m - 1
        odd = (t % g) >= m
        sums[2 + li] = np.where(odd, (j > ref) & (j <= t), (j > t) & (j <= ref))
        pair = ((t // g) == (j // g)) & odd & ((j % g) < m)
        level_id = np.where(pair, li + 1, level_id).astype(np.int32)
    sums = sums.reshape(nb * chunk, chunk)
    return np.concatenate([sums, sums, sums], axis=1), level_id, nb


def _split3(x):
    hi = x.astype(BF16)
    r1 = x - hi.astype(F32)
    mid = r1.astype(BF16)
    lo = (r1 - mid.astype(F32)).astype(BF16)
    return jnp.concatenate([hi, mid, lo], axis=0)


def _hgrn_body(x_ref, g2_ref, g3_ref, wq_ref, wf_ref, wi_ref, wg_ref, hl_ref, ng_ref, wo_ref,
               sums_ref, lid_ref, s0_ref, o_ref, sout_ref, *scratch,
               layer, chunk, seg, nh, nb, nchunk, nhg):
    sets = (scratch[0:6], scratch[6:12])
    h_scr, e_scr, st_scr, on_scr, acc_scr, rs_scr, worst_smem = scratch[12:]
    i = pl.program_id(0)
    j = pl.program_id(1)
    nseg = chunk // seg
    nlev = nb - 2
    heads = nhg * nh
    lid = lid_ref[...]

    @pl.when(j == 0)
    def _():
        _pre_norm(x_ref, g2_ref, h_scr, acc_scr, rs_scr)

    @pl.when(jnp.logical_and(i == 0, j == 0))
    def _():
        for s in range(nseg):
            for hd in range(heads):
                st_scr[s, hd] = s0_ref[s, hd].T

    def project(p):
        q_scr, lf_scr, k_scr, v_scr, gate_scr, b_scr = sets[p]
        hl = hl_ref[...]
        e = jnp.exp(hl - jnp.max(hl, axis=0, keepdims=True))
        pr = e / jnp.sum(e, axis=0, keepdims=True)
        lb = jnp.zeros_like(pr[0:1])
        for r in range(1, layer + 1):
            lb = lb + pr[r:r + 1]
        h = h_scr[...]
        q_scr[...] = _silu(_dot(h, wq_ref[...]))
        f = lb + (1.0 - lb) * jax.nn.sigmoid(_dot(h, wf_ref[...]))
        lf = jnp.log(f)
        lf_scr[...] = lf
        k_scr[...] = 1.0 - f
        v_scr[...] = _dot(h, wi_ref[...]).astype(BF16)
        gate_scr[...] = _silu(_dot(h, wg_ref[...]))
        if nseg == 1:
            worst = jnp.zeros((1, lf.shape[1]), F32)
            for c in range(nchunk):
                rows = slice(c * chunk, (c + 1) * chunk)
                b = _dot(sums_ref[0:chunk, :], _split3(lf[rows, :]))
                b_scr[rows, :] = b
                worst = jnp.maximum(worst, -b[chunk - 1:chunk, :])
            worst_smem[p] = jnp.max(worst)

    def scan(p, split):
        q_scr, lf_scr, k_scr, v_scr, gate_scr, b_scr = sets[p]
        head0 = (j - 1) * nh

        def carry_state(rows, cols, head, o_intra, q_pre, k_end, vh, e_end):
            inter = []
            for s in range(nseg):
                srows = slice(s * seg, (s + 1) * seg)
                st = st_scr[s, head]
                inter.append(_dot_nt(q_pre[srows], st.astype(BF16)))
                st_scr[s, head] = st * e_end[s] + _dot_tn(vh[srows], k_end[srows])
            o = o_intra + (inter[0] if nseg == 1 else jnp.concatenate(inter, axis=0))
            on_scr[rows, cols] = (_rms(o, ng_ref[...]) * gate_scr[rows, cols]).astype(BF16)

        for c in range(nchunk):
            rows = slice(c * chunk, (c + 1) * chunk)
            if split:
                b = b_scr[rows, :]
                half = 0.5 * b[chunk - 1:chunk, :]
                x = b - half
                e_q = jnp.exp(x)
                e_k = jnp.exp(-x)
                e_half = jnp.exp(half)
            else:
                e_scr[...] = jnp.exp(_dot(sums_ref[...], _split3(lf_scr[rows, :])))
            for hh in range(nh):
                cols = slice(hh * HEAD_DIM, (hh + 1) * HEAD_DIM)
                vh = v_scr[rows, cols]
                if split:
                    qt = q_scr[rows, cols] * e_q[:, cols]
                    kt = k_scr[rows, cols] * e_k[:, cols]
                    eh = e_half[:, cols]
                    sc = jnp.where(lid >= 0, _dot_nt(qt.astype(BF16), kt.astype(BF16)), 0.0)
                    carry_state(rows, cols, head0 + hh, _dot(sc.astype(BF16), vh),
                                (qt * eh).astype(BF16), (kt * eh).astype(BF16), vh, [eh * eh])
                else:
                    qh = q_scr[rows, cols]
                    kh = k_scr[rows, cols]
                    e_pre = e_scr[0:chunk, cols]
                    k_end = (kh * e_scr[chunk:2 * chunk, cols]).astype(BF16)
                    sc = jnp.where(lid == 0, _dot_nt(qh.astype(BF16), kh.astype(BF16)), 0.0)
                    for l in range(nlev):
                        el = e_scr[(2 + l) * chunk:(3 + l) * chunk, cols]
                        sc_l = _dot_nt((qh * el).astype(BF16), (kh * el).astype(BF16))
                        sc = jnp.where(lid == l + 1, sc_l, sc)
                    e_end = [e_pre[(s + 1) * seg - 1:(s + 1) * seg, :] for s in range(nseg)]
                    carry_state(rows, cols, head0 + hh, _dot(sc.astype(BF16), vh),
                                (qh * e_pre).astype(BF16), k_end, vh, e_end)
        acc_scr[...] += _dot(on_scr[...], wo_ref[...])

    def both(p, split):
        project(p)
        scan(1 - p, split)

    for p in (0, 1):
        mine = (j % 2) == p
        inner = jnp.logical_and(mine, jnp.logical_and(j >= 1, j < nhg))
        last = jnp.logical_and(mine, j == nhg)
        if p == 0:
            pl.when(j == 0)(functools.partial(project, 0))
        if nseg == 1:
            benign = worst_smem[1 - p] <= MAX_SPLIT_DECAY
            pl.when(jnp.logical_and(inner, benign))(functools.partial(both, p, True))
            pl.when(jnp.logical_and(inner, jnp.logical_not(benign)))(functools.partial(both, p, False))
            if nhg % 2 == p:
                pl.when(jnp.logical_and(last, benign))(functools.partial(scan, 1 - p, True))
                pl.when(jnp.logical_and(last, jnp.logical_not(benign)))(functools.partial(scan, 1 - p, False))
        else:
            pl.when(inner)(functools.partial(both, p, False))
            if nhg % 2 == p:
                pl.when(last)(functools.partial(scan, 1 - p, False))

    @pl.when(j == nhg)
    def _():
        _post_norm(x_ref, g3_ref, acc_scr, rs_scr, o_ref, 1.0)

    @pl.when(jnp.logical_and(i == pl.num_programs(0) - 1, j == nhg))
    def _():
        for s in range(nseg):
            for hd in range(heads):
                sout_ref[s, hd] = st_scr[s, hd].T


def _hgrn_mixer(x, g2, g3, w_in, hg_lower, norm_g, w_out, s0, *, layer, tm, chunk, seg):
    t, d = x.shape
    heads = d // HEAD_DIM
    nseg = chunk // seg
    tn = HG_TILE
    nh = tn // HEAD_DIM
    nhg = heads // nh
    assert t % tm == 0 and tm % chunk == 0 and heads % nh == 0 and tm % ROW_CHUNK == 0
    assert s0.shape == (nseg, heads, HEAD_DIM, HEAD_DIM)
    assert nseg == 1 or t == chunk
    sums, level_id, nb = _scan_tables(chunk, seg)
    body = functools.partial(_hgrn_body, layer=layer, chunk=chunk, seg=seg, nh=nh, nb=nb,
                             nchunk=tm // chunk, nhg=nhg)
    row_spec = pl.BlockSpec((tm, d), lambda i, j: (i, 0))
    vec_spec = pl.BlockSpec((1, d), lambda i, j: (0, 0))
    nsec = d // tn
    state_spec = pl.BlockSpec(s0.shape, lambda i, j: (0, 0, 0, 0))

    def proj(j):
        return jnp.minimum(j, nhg - 1)

    def outp(j):
        return jnp.maximum(j - 1, 0)

    scratch_set = [
        pltpu.VMEM((tm, tn), F32),
        pltpu.VMEM((tm, tn), F32),
        pltpu.VMEM((tm, tn), F32),
        pltpu.VMEM((tm, tn), BF16),
        pltpu.VMEM((tm, tn), F32),
        pltpu.VMEM((tm, tn), F32),
    ]
    return pl.pallas_call(
        body,
        out_shape=(jax.ShapeDtypeStruct((t, d), F32), jax.ShapeDtypeStruct(s0.shape, F32)),
        grid=(t // tm, nhg + 1),
        in_specs=[
            row_spec, vec_spec, vec_spec,
            pl.BlockSpec((d, tn), lambda i, j: (0, proj(j))),
            pl.BlockSpec((d, tn), lambda i, j: (0, nsec + proj(j))),
            pl.BlockSpec((d, tn), lambda i, j: (0, 2 * nsec + proj(j))),
            pl.BlockSpec((d, tn), lambda i, j: (0, 3 * nsec + proj(j))),
            pl.BlockSpec((hg_lower.shape[0], tn), lambda i, j: (0, proj(j))),
            pl.BlockSpec((1, HEAD_DIM), lambda i, j: (0, 0)),
            pl.BlockSpec((tn, d), lambda i, j: (outp(j), 0)),
            pl.BlockSpec(sums.shape, lambda i, j: (0, 0)),
            pl.BlockSpec(level_id.shape, lambda i, j: (0, 0)),
            state_spec,
        ],
        out_specs=(row_spec, state_spec),
        scratch_shapes=scratch_set + scratch_set + [
            pltpu.VMEM((tm, d), BF16),
            pltpu.VMEM((nb * chunk, tn), F32),
            pltpu.VMEM((nseg, heads, HEAD_DIM, HEAD_DIM), F32),
            pltpu.VMEM((tm, tn), BF16),
            pltpu.VMEM((tm, d), F32),
            pltpu.VMEM((tm, LANES), F32),
            pltpu.SMEM((2,), F32),
        ],
        compiler_params=_params(),
    )(x, g2, g3, w_in, w_in, w_in, w_in, hg_lower, norm_g, w_out,
      jnp.asarray(sums, BF16), jnp.asarray(level_id), s0)


def _trunk(x, hg_init, p, wts, *, tm, ffn_tm, sample):
    depth = p["norm_g"].shape[0]
    n_mix = 2
    hg_states, sg_vs = [], []

    def conversions(keys):
        jobs = []
        for key in keys:
            if key not in wts and key[0] == "ffn" and key[1] < depth:
                jobs.append((key, ((p["ffn_w_up"], key[1:], UP_CAST_ROWS), (p["ffn_w_down"], key[1:], DOWN_CAST_ROWS))))
            elif key not in wts and key[0] in ("sg", "hg"):
                jobs.append((key, ((p[key[0] + "_w_in"], key[1:], MIX_CAST_ROWS), (p[key[0] + "_w_out"], key[1:], MIX_CAST_ROWS))))
        return jobs

    def ffn(x, i, k, g_pre, g_post):
        nxt = ("ffn", i, 1) if k == 0 else ("ffn", i + 1, 0)
        mixer = [(("sg", "hg")[i % n_mix], i // n_mix)] if k == 0 else []
        jobs = conversions([nxt] + mixer)
        side = tuple(job for _, pair in jobs for job in pair)
        x, converted = _ffn(x, g_pre, g_post, *wts[("ffn", i, k)], tm=ffn_tm, side=side)
        for n, (key, _) in enumerate(jobs):
            wts[key] = converted[2 * n:2 * n + 2]
        return x

    for i in range(depth):
        g = p["norm_g"][i]
        row = lambda r: g[r:r + 1]
        jm = i // n_mix
        x = ffn(x, i, 0, row(0), row(1))
        if i % n_mix == 0:
            w_s, b_s = p["sg_w_s"][jm], p["sg_b_s"][jm]
            if sample is not None:
                n_streams, seq = sample
                w_s = jnp.tile(w_s[:, :seq, :seq], (1, n_streams, n_streams))
                b_s = jnp.tile(b_s[:, :seq], (1, n_streams))
            bias_rows = jnp.repeat(b_s.T, LANES, axis=1)
            w_in, w_out = wts[("sg", jm)]
            res = _sg_mixer(x, row(2), row(3), w_in, p["sg_ln_g"][jm:jm + 1],
                            p["sg_ln_b"][jm:jm + 1], w_s, bias_rows, w_out,
                            tm=tm, seg_mode=None if sample is None else sample[1],
                            emit_v=sample is not None)
            if sample is not None:
                x, v_rows = res
                sg_vs.append(v_rows)
            else:
                x = res
        else:
            w_in, w_out = wts[("hg", jm)]
            x, st = _hgrn_mixer(x, row(2), row(3), w_in, p["hg_lower"],
                                p["hg_norm_g"][jm:jm + 1], w_out, hg_init[jm],
                                layer=i, tm=tm, chunk=SG_CHUNK if sample is None else x.shape[0],
                                seg=SG_CHUNK if sample is None else sample[1])
            hg_states.append(st)
        x = ffn(x, i, 1, row(4), row(5))
    return x, hg_states, sg_vs


def kernel(x_prompt, x_sample, state_hgrn, norm_g, ffn_w_up, ffn_w_down, sg_w_in, sg_ln_g, sg_ln_b,
           sg_w_s, sg_b_s, sg_w_out, hg_w_in, hg_lower, hg_norm_g, hg_w_out):
    batch, seq, d = x_prompt.shape
    dec_batch, dec_seq, _ = x_sample.shape
    n_hg = hg_w_in.shape[0]
    heads = d // HEAD_DIM
    p = dict(norm_g=norm_g, ffn_w_up=ffn_w_up, ffn_w_down=ffn_w_down, sg_w_in=sg_w_in, sg_ln_g=sg_ln_g,
             sg_ln_b=sg_ln_b, sg_w_s=sg_w_s, sg_b_s=sg_b_s, sg_w_out=sg_w_out, hg_w_in=hg_w_in,
             hg_lower=hg_lower, hg_norm_g=hg_norm_g, hg_w_out=hg_w_out)
    wts = {("ffn", 0, 0): (ffn_w_up[0, 0].astype(BF16), ffn_w_down[0, 0].astype(BF16))}
    ys, sts = [], []
    for b in range(batch):
        init = [jnp.zeros((1, heads, HEAD_DIM, HEAD_DIM), F32) for _ in range(n_hg)]
        y, hg_p, _ = _trunk(x_prompt[b], init, p, wts, tm=512, ffn_tm=1024, sample=None)
        ys.append(y)
        sts.append(jnp.stack([s[0] for s in hg_p], axis=0))
    y_prompt = jnp.stack(ys, axis=0)
    state_hgrn_prompt = jnp.stack(sts, axis=1)

    init = [state_hgrn[jm].astype(F32) for jm in range(n_hg)]
    y, hg_s, sg_v = _trunk(x_sample.reshape(dec_batch * dec_seq, d), init, p, wts,
                           tm=dec_batch * dec_seq, ffn_tm=dec_batch * dec_seq, sample=(dec_batch, dec_seq))
    y_sample = y.reshape(dec_batch, dec_seq, d)
    state_hgrn_sample = jnp.stack(hg_s, axis=0)
    state_sg_v_sample = jnp.stack([v.reshape(dec_batch, dec_seq, d) for v in sg_v], axis=0)
    return (y_prompt, y_sample, state_hgrn_prompt, state_hgrn_sample, state_sg_v_sample)
```

```python
import functools

import numpy as np
import jax
import jax.numpy as jnp
from jax import lax
from jax.experimental import pallas as pl
from jax.experimental.pallas import tpu as pltpu

EPS = 1e-6
F32 = jnp.float32
BF16 = jnp.bfloat16

LANES = 128
ROW_CHUNK = 16
HEAD_DIM = 128
SG_CHUNK = 128
FFN_TILE = 512
SG_TILE = 512
UP_CAST_ROWS = 16
MIX_CAST_ROWS = 16
DOWN_CAST_ROWS = 64
HG_TILE = 256
STREAM_CHUNK = 64
V7X_VMEM_BYTES = 64 * 1024 * 1024
VMEM_LIMIT_BYTES = V7X_VMEM_BYTES - 8 * 1024 * 1024
MAX_SPLIT_DECAY = 140.0


def _dot(a, b):
    return jnp.dot(a, b, preferred_element_type=F32)


def _dot_nt(a, b):
    return lax.dot_general(a, b, (((1,), (1,)), ((), ())), preferred_element_type=F32)


def _dot_tn(a, b):
    return lax.dot_general(a, b, (((0,), (0,)), ((), ())), preferred_element_type=F32)


def _rms(x, g):
    return x * lax.rsqrt(jnp.mean(x * x, axis=-1, keepdims=True) + EPS) * g


def _silu(x):
    return x * jax.nn.sigmoid(x)


def _gelu(x):
    return 0.5 * x * (1.0 + lax.erf(x * np.float32(np.sqrt(0.5))))


def _zero_after(v):
    u = pltpu.bitcast(v, jnp.uint32)
    u = lax.shift_right_logical(lax.shift_right_logical(u, jnp.uint32(16)), jnp.uint32(16))
    return u.astype(F32)


def _for_tiles(n_rows, n_cols, fn):
    def step(c, carry):
        rows = pl.ds(pl.multiple_of(c * ROW_CHUNK, ROW_CHUNK), ROW_CHUNK)
        for ct in range(n_cols // LANES):
            fn(rows, slice(ct * LANES, (ct + 1) * LANES))
        return carry
    lax.fori_loop(0, n_rows // ROW_CHUNK, step, 0, unroll=max(2, 32 * LANES // n_cols))


def _row_rsqrt(x):
    r = lax.rsqrt(jnp.mean(x * x, axis=-1, keepdims=True) + EPS)
    return jnp.broadcast_to(r, (x.shape[0], LANES))


def _pre_norm(x_ref, g_ref, h_scr, acc_scr, rs_scr):
    rs_scr[...] = _row_rsqrt(x_ref[...])

    def tile(rows, cols):
        h_scr[rows, cols] = (x_ref[rows, cols] * rs_scr[rows, :] * g_ref[:, cols]).astype(BF16)
        acc_scr[rows, cols] = jnp.zeros((ROW_CHUNK, LANES), F32)
    _for_tiles(x_ref.shape[0], x_ref.shape[1], tile)


def _post_norm(x_ref, g_ref, acc_scr, rs_scr, o_ref, scale):
    rs_scr[...] = _row_rsqrt(acc_scr[...])

    def tile(rows, cols):
        o_ref[rows, cols] = x_ref[rows, cols] + (acc_scr[rows, cols] * rs_scr[rows, :]) * (scale * g_ref[:, cols])
    _for_tiles(x_ref.shape[0], x_ref.shape[1], tile)


def _params(grid_rank=2):
    return pltpu.CompilerParams(dimension_semantics=("arbitrary",) * grid_rank,
                                vmem_limit_bytes=VMEM_LIMIT_BYTES)


def _ffn_body(*refs, nt, nslice, nside):
    xn_ref, xp_ref, gpre_ref, gpost_ref, wa_ref, wb_ref, wd_ref = refs[:7]
    side_src = refs[7:7 + nside]
    o_ref = refs[7 + nside]
    side_dst = refs[8 + nside:8 + 2 * nside]
    h0, h1, acc0, acc1 = refs[8 + 2 * nside:]
    r = pl.program_id(0)
    j = pl.program_id(1)
    rs = xn_ref.shape[0]
    base = jnp.minimum(j, nslice - 1) * rs

    def chunk_rows(c):
        return pl.ds(pl.multiple_of(base + c * ROW_CHUNK, ROW_CHUNK), ROW_CHUNK)

    def convert_side():
        for src, dst in zip(side_src, side_dst):
            dst[...] = src[...].astype(BF16)

    def stages(h_pre, h_main, acc_main, acc_post):
        def pre():
            for c in range(rs // ROW_CHUNK):
                xc = xn_ref[c * ROW_CHUNK:(c + 1) * ROW_CHUNK, :]
                h_pre[chunk_rows(c), :] = _rms(xc, gpre_ref[...]).astype(BF16)

        def main(zero=None):
            h = h_main[...]
            a = _dot(h, wa_ref[...])
            if zero is not None:
                a = a + jnp.tile(zero[0:1, :], (1, a.shape[1] // LANES))
            b = _dot(h, wb_ref[...])
            contrib = _dot((_silu(a) * b).astype(BF16), wd_ref[...])
            acc_main[...] = jnp.where(j == 0, contrib, acc_main[...] + contrib)

        def post():
            for c in range(rs // ROW_CHUNK):
                rows = slice(c * ROW_CHUNK, (c + 1) * ROW_CHUNK)
                o_ref[rows, :] = xp_ref[rows, :] + 0.5 * _rms(acc_post[chunk_rows(c), :], gpost_ref[...])

        steady = jnp.logical_and(r >= 2, r < nt)

        @pl.when(steady)
        def _():
            pre()
            post()
            convert_side()
            zero = _zero_after(h_pre[chunk_rows(0), 0:LANES]) + _zero_after(o_ref[0:8, 0:LANES])
            for dst in side_dst:
                zero = zero + _zero_after(dst[0:ROW_CHUNK, 0:LANES])
            main(zero)

        @pl.when(jnp.logical_not(steady))
        def _():
            @pl.when(jnp.logical_and(r == 0, j == 0))
            def _():
                acc0[...] = jnp.zeros_like(acc0)
                acc1[...] = jnp.zeros_like(acc1)
            pl.when(r < nt)(pre)
            pl.when(jnp.logical_and(r >= 1, r <= nt))(main)
            pl.when(r >= 2)(post)
            convert_side()

    pl.when(r % 2 == 0)(lambda: stages(h0, h1, acc1, acc0))
    pl.when(r % 2 == 1)(lambda: stages(h1, h0, acc0, acc1))


def _ffn_single_body(x_ref, gpre_ref, gpost_ref, wa_ref, wb_ref, wd_ref, o_ref, h_scr, acc_scr):
    j = pl.program_id(0)

    @pl.when(j == 0)
    def _():
        h_scr[...] = _rms(x_ref[...], gpre_ref[...]).astype(BF16)
        acc_scr[...] = jnp.zeros_like(acc_scr)

    h = h_scr[...]
    a = _dot(h, wa_ref[...])
    b = _dot(h, wb_ref[...])
    acc_scr[...] += _dot((_silu(a) * b).astype(BF16), wd_ref[...])

    @pl.when(j == pl.num_programs(0) - 1)
    def _():
        o_ref[...] = x_ref[...] + 0.5 * _rms(acc_scr[...], gpost_ref[...])


def _ffn(x, g_pre, g_post, w_up, w_down, *, tm, side=()):
    t, d = x.shape
    f = w_down.shape[0]
    tf = FFN_TILE
    nf = f // tf
    nt = t // tm
    nslice = min(nf, tm // ROW_CHUNK)
    while tm % (nslice * ROW_CHUNK):
        nslice -= 1
    rs = tm // nslice
    assert t % tm == 0 and f % tf == 0
    if nt == 1 and not side:
        out = pl.pallas_call(
            _ffn_single_body,
            out_shape=jax.ShapeDtypeStruct((t, d), F32),
            grid=(nf,),
            in_specs=[
                pl.BlockSpec((t, d), lambda j: (0, 0)),
                pl.BlockSpec((1, d), lambda j: (0, 0)),
                pl.BlockSpec((1, d), lambda j: (0, 0)),
                pl.BlockSpec((d, tf), lambda j: (0, j)),
                pl.BlockSpec((d, tf), lambda j: (0, j + nf)),
                pl.BlockSpec((tf, d), lambda j: (j, 0)),
            ],
            out_specs=pl.BlockSpec((t, d), lambda j: (0, 0)),
            scratch_shapes=[pltpu.VMEM((t, d), BF16), pltpu.VMEM((t, d), F32)],
            compiler_params=_params(1),
        )(x, g_pre, g_post, w_up, w_up, w_down)
        return out, ()

    def w_step(r, j):
        return jnp.where(r == 0, 0, jnp.where(r > nt, nf - 1, j))

    def next_slice(r, j):
        return jnp.where(r < nt, r * nslice + jnp.minimum(j, nslice - 1), nt * nslice - 1)

    def prev_slice(r, j):
        return jnp.where(r >= 2, (r - 2) * nslice + jnp.minimum(j, nslice - 1), 0)

    side_in, side_out, side_shapes = [], [], []
    for src, lead, rb in side:
        rows, cols = src.shape[-2:]
        nblk = rows // rb
        assert rows % rb == 0 and nblk <= nt * nf and len(lead) == src.ndim - 2

        def blk(r, j, nblk=nblk):
            return jnp.clip((r - 1) * nf + j, 0, nblk - 1)

        side_in.append(pl.BlockSpec((None,) * len(lead) + (rb, cols),
                                    lambda r, j, lead=lead, blk=blk: (*lead, blk(r, j), 0)))
        side_out.append(pl.BlockSpec((rb, cols), lambda r, j, blk=blk: (blk(r, j), 0)))
        side_shapes.append(jax.ShapeDtypeStruct((rows, cols), BF16))

    out = pl.pallas_call(
        functools.partial(_ffn_body, nt=nt, nslice=nslice, nside=len(side)),
        out_shape=[jax.ShapeDtypeStruct((t, d), F32)] + side_shapes,
        grid=(nt + 2, nf),
        in_specs=[
            pl.BlockSpec((rs, d), lambda r, j: (next_slice(r, j), 0)),
            pl.BlockSpec((rs, d), lambda r, j: (prev_slice(r, j), 0)),
            pl.BlockSpec((1, d), lambda r, j: (0, 0)),
            pl.BlockSpec((1, d), lambda r, j: (0, 0)),
            pl.BlockSpec((d, tf), lambda r, j: (0, w_step(r, j))),
            pl.BlockSpec((d, tf), lambda r, j: (0, w_step(r, j) + nf)),
            pl.BlockSpec((tf, d), lambda r, j: (w_step(r, j), 0)),
        ] + side_in,
        out_specs=[pl.BlockSpec((rs, d), lambda r, j: (prev_slice(r, j), 0))] + side_out,
        scratch_shapes=[pltpu.VMEM((tm, d), BF16), pltpu.VMEM((tm, d), BF16),
                        pltpu.VMEM((tm, d), F32), pltpu.VMEM((tm, d), F32)],
        compiler_params=_params(),
    )(x, x, g_pre, g_post, w_up, w_up, w_down, *[src for src, _, _ in side])
    return out[0], tuple(out[1:])


def _sg_mask(seg_mode):
    row = lax.broadcasted_iota(jnp.int32, (SG_CHUNK, SG_CHUNK), 0)
    col = lax.broadcasted_iota(jnp.int32, (SG_CHUNK, SG_CHUNK), 1)
    if seg_mode is None:
        return (row // STREAM_CHUNK) >= (col // STREAM_CHUNK)
    return (row // seg_mode) == (col // seg_mode)


def _sg_body(x_ref, g2_ref, g3_ref, wu_ref, wv_ref, lng_ref, lnb_ref, ws_ref, bias_ref, wo_ref,
             *rest, ng, gpt, nchunk, seg_mode, emit_v):
    if emit_v:
        o_ref, vout_ref, h_scr, u_scr, v_scr, y_scr, acc_scr, rs_scr = rest
    else:
        o_ref, h_scr, u_scr, v_scr, y_scr, acc_scr, rs_scr = rest
        vout_ref = None
    j = pl.program_id(1)
    tn = gpt * LANES
    d = ng * tn
    tm = x_ref.shape[0]

    @pl.when(j == 0)
    def _():
        _pre_norm(x_ref, g2_ref, h_scr, acc_scr, rs_scr)

    @pl.when(j < ng)
    def _():
        h = h_scr[...]
        u_scr[j] = _gelu(_dot(h, wu_ref[...]))
        v_scr[j] = _gelu(_dot(h, wv_ref[...]))

    @pl.when(j == ng)
    def _():
        mu = sum(jnp.sum(v_scr[jj], axis=-1, keepdims=True) for jj in range(ng)) * (1.0 / d)
        var = sum(jnp.sum(jnp.square(v_scr[jj] - mu), axis=-1, keepdims=True)
                  for jj in range(ng)) * (1.0 / d)
        rstd = lax.rsqrt(var + EPS)
        for jj in range(ng):
            cols = slice(jj * tn, (jj + 1) * tn)
            vn = (v_scr[jj] - mu) * rstd * lng_ref[:, cols] + lnb_ref[:, cols]
            v_scr[jj] = vn
            if emit_v:
                vout_ref[:, cols] = vn

    @pl.when(j >= ng)
    def _():
        jj = j - ng
        mask = _sg_mask(seg_mode)
        for g in range(gpt):
            cols = slice(g * LANES, (g + 1) * LANES)
            w = jnp.where(mask, ws_ref[g], 0.0).astype(BF16)
            for n in range(nchunk):
                rows = slice(n * SG_CHUNK, (n + 1) * SG_CHUNK)
                s = _dot(w, v_scr[jj, rows, cols].astype(BF16)) + bias_ref[:, cols]
                y_scr[rows, cols] = (u_scr[jj, rows, cols] * s).astype(BF16)
        acc_scr[...] += _dot(y_scr[...], wo_ref[...])

    @pl.when(j == 2 * ng - 1)
    def _():
        _post_norm(x_ref, g3_ref, acc_scr, rs_scr, o_ref, 1.0)


def _sg_mixer(x, g2, g3, w_in, ln_g, ln_b, w_s, bias_rows, w_out, *, tm, seg_mode, emit_v):
    t, d = x.shape
    tn = SG_TILE
    gpt = tn // LANES
    ng = d // tn
    assert t % tm == 0 and tm % SG_CHUNK == 0 and d % tn == 0 and tm % ROW_CHUNK == 0
    last = ng - 1
    body = functools.partial(_sg_body, ng=ng, gpt=gpt, nchunk=tm // SG_CHUNK,
                             seg_mode=seg_mode, emit_v=emit_v)
    row_spec = pl.BlockSpec((tm, d), lambda i, j: (i, 0))
    vec_spec = pl.BlockSpec((1, d), lambda i, j: (0, 0))
    out_shape = jax.ShapeDtypeStruct((t, d), F32)
    return pl.pallas_call(
        body,
        out_shape=(out_shape, out_shape) if emit_v else out_shape,
        grid=(t // tm, 2 * ng),
        in_specs=[
            row_spec, vec_spec, vec_spec,
            pl.BlockSpec((d, tn), lambda i, j: (0, jnp.minimum(j, last))),
            pl.BlockSpec((d, tn), lambda i, j: (0, ng + jnp.minimum(j, last))),
            vec_spec, vec_spec,
            pl.BlockSpec((gpt, SG_CHUNK, SG_CHUNK), lambda i, j: (jnp.maximum(j - ng, 0), 0, 0)),
            pl.BlockSpec((SG_CHUNK, tn), lambda i, j: (0, jnp.maximum(j - ng, 0))),
            pl.BlockSpec((tn, d), lambda i, j: (jnp.maximum(j - ng, 0), 0)),
        ],
        out_specs=(row_spec, row_spec) if emit_v else row_spec,
        scratch_shapes=[
            pltpu.VMEM((tm, d), BF16),
            pltpu.VMEM((ng, tm, tn), F32),
            pltpu.VMEM((ng, tm, tn), F32),
            pltpu.VMEM((tm, tn), BF16),
            pltpu.VMEM((tm, d), F32),
            pltpu.VMEM((tm, LANES), F32),
        ],
        compiler_params=_params(),
    )(x, g2, g3, w_in, w_in, ln_g, ln_b, w_s, bias_rows, w_out)


def _scan_tables(chunk, seg):
    levels = []
    m = seg // 2
    while m >= 1:
        levels.append(m)
        m //= 2
    nb = 2 + len(levels)
    t = np.arange(chunk)[:, None]
    j = np.arange(chunk)[None, :]
    same_seg = (t // seg) == (j // seg)
    sums = np.zeros((nb, chunk, chunk), np.float32)
    sums[0] = same_seg & (j <= t)
    sums[1] = same_seg & (j > t)
    level_id = np.where(t == j, 0, -1).astype(np.int32)
    for li, m in enumerate(levels):
        g = 2 * m
        ref = (t // g) * g + m - 1
        odd = (t % g) >= m
        sums[2 + li] = np.where(odd, (j > ref) & (j <= t), (j > t) & (j <= ref))
        pair = ((t // g) == (j // g)) & odd & ((j % g) < m)
        level_id = np.where(pair, li + 1, level_id).astype(np.int32)
    sums = sums.reshape(nb * chunk, chunk)
    return np.concatenate([sums, sums, sums], axis=1), level_id, nb


def _split3(x):
    hi = x.astype(BF16)
    r1 = x - hi.astype(F32)
    mid = r1.astype(BF16)
    lo = (r1 - mid.astype(F32)).astype(BF16)
    return jnp.concatenate([hi, mid, lo], axis=0)


def _hgrn_body(x_ref, g2_ref, g3_ref, wq_ref, wf_ref, wi_ref, wg_ref, hl_ref, ng_ref, wo_ref,
               sums_ref, lid_ref, s0_ref, o_ref, sout_ref, *scratch,
               layer, chunk, seg, nh, nb, nchunk, nhg):
    sets = (scratch[0:6], scratch[6:12])
    h_scr, e_scr, st_scr, on_scr, acc_scr, rs_scr, worst_smem = scratch[12:]
    i = pl.program_id(0)
    j = pl.program_id(1)
    nseg = chunk // seg
    nlev = nb - 2
    heads = nhg * nh
    lid = lid_ref[...]

    @pl.when(j == 0)
    def _():
        _pre_norm(x_ref, g2_ref, h_scr, acc_scr, rs_scr)

    @pl.when(jnp.logical_and(i == 0, j == 0))
    def _():
        for s in range(nseg):
            for hd in range(heads):
                st_scr[s, hd] = s0_ref[s, hd].T
        worst_smem[0] = jnp.float32(0.0)
        worst_smem[1] = jnp.float32(0.0)

    def project(p):
        q_scr, lf_scr, k_scr, v_scr, gate_scr, b_scr = sets[p]
        hl = hl_ref[...]
        e = jnp.exp(hl - jnp.max(hl, axis=0, keepdims=True))
        pr = e / jnp.sum(e, axis=0, keepdims=True)
        lb = jnp.zeros_like(pr[0:1])
        for r in range(1, layer + 1):
            lb = lb + pr[r:r + 1]
        h = h_scr[...]
        q_scr[...] = _silu(_dot(h, wq_ref[...]))
        f = lb + (1.0 - lb) * jax.nn.sigmoid(_dot(h, wf_ref[...]))
        lf = jnp.log(f)
        lf_scr[...] = lf
        k_scr[...] = 1.0 - f
        v_scr[...] = _dot(h, wi_ref[...]).astype(BF16)
        gate_scr[...] = _silu(_dot(h, wg_ref[...]))
        if nseg == 1:
            worst = jnp.zeros((1, lf.shape[1]), F32)
            for c in range(nchunk):
                rows = slice(c * chunk, (c + 1) * chunk)
                b = _dot(sums_ref[0:chunk, :], _split3(lf[rows, :]))
                b_scr[rows, :] = b
                worst = jnp.maximum(worst, -b[chunk - 1:chunk, :])
            worst_smem[p] = jnp.max(worst)

    def scan(p, split):
        q_scr, lf_scr, k_scr, v_scr, gate_scr, b_scr = sets[p]
        head0 = (j - 1) * nh

        def carry_state(rows, cols, head, o_intra, q_pre, k_end, vh, e_end):
            inter = []
            for s in range(nseg):
                srows = slice(s * seg, (s + 1) * seg)
                st = st_scr[s, head]
                inter.append(_dot_nt(q_pre[srows], st.astype(BF16)))
                st_scr[s, head] = st * e_end[s] + _dot_tn(vh[srows], k_end[srows])
            o = o_intra + (inter[0] if nseg == 1 else jnp.concatenate(inter, axis=0))
            on_scr[rows, cols] = (_rms(o, ng_ref[...]) * gate_scr[rows, cols]).astype(BF16)

        for c in range(nchunk):
            rows = slice(c * chunk, (c + 1) * chunk)
            if split:
                b = b_scr[rows, :]
                half = 0.5 * b[chunk - 1:chunk, :]
                x = b - half
                e_q = jnp.exp(x)
                e_k = jnp.exp(-x)
                e_half = jnp.exp(half)
            else:
                e_scr[...] = jnp.exp(_dot(sums_ref[...], _split3(lf_scr[rows, :])))
            for hh in range(nh):
                cols = slice(hh * HEAD_DIM, (hh + 1) * HEAD_DIM)
                vh = v_scr[rows, cols]
                if split:
                    qt = q_scr[rows, cols] * e_q[:, cols]
                    kt = k_scr[rows, cols] * e_k[:, cols]
                    eh = e_half[:, cols]
                    sc = jnp.where(lid >= 0, _dot_nt(qt.astype(BF16), kt.astype(BF16)), 0.0)
                    carry_state(rows, cols, head0 + hh, _dot(sc.astype(BF16), vh),
                                (qt * eh).astype(BF16), (kt * eh).astype(BF16), vh, [eh * eh])
                else:
                    qh = q_scr[rows, cols]
                    kh = k_scr[rows, cols]
                    e_pre = e_scr[0:chunk, cols]
                    k_end = (kh * e_scr[chunk:2 * chunk, cols]).astype(BF16)
                    sc = jnp.where(lid == 0, _dot_nt(qh.astype(BF16), kh.astype(BF16)), 0.0)
                    for l in range(nlev):
                        el = e_scr[(2 + l) * chunk:(3 + l) * chunk, cols]
                        sc_l = _dot_nt((qh * el).astype(BF16), (kh * el).astype(BF16))
                        sc = jnp.where(lid == l + 1, sc_l, sc)
                    e_end = [e_pre[(s + 1) * seg - 1:(s + 1) * seg, :] for s in range(nseg)]
                    carry_state(rows, cols, head0 + hh, _dot(sc.astype(BF16), vh),
                                (qh * e_pre).astype(BF16), k_end, vh, e_end)
        acc_scr[...] += _dot(on_scr[...], wo_ref[...])

    def both(p, split):
        project(p)
        scan(1 - p, split)

    for p in (0, 1):
        mine = (j % 2) == p
        inner = jnp.logical_and(mine, jnp.logical_and(j >= 1, j < nhg))
        last = jnp.logical_and(mine, j == nhg)
        if p == 0:
            pl.when(j == 0)(functools.partial(project, 0))
        if nseg == 1:
            benign = worst_smem[1 - p] <= MAX_SPLIT_DECAY
            pl.when(jnp.logical_and(inner, benign))(functools.partial(both, p, True))
            pl.when(jnp.logical_and(inner, jnp.logical_not(benign)))(functools.partial(both, p, False))
            if nhg % 2 == p:
                pl.when(jnp.logical_and(last, benign))(functools.partial(scan, 1 - p, True))
                pl.when(jnp.logical_and(last, jnp.logical_not(benign)))(functools.partial(scan, 1 - p, False))
        else:
            pl.when(inner)(functools.partial(both, p, False))
            if nhg % 2 == p:
                pl.when(last)(functools.partial(scan, 1 - p, False))

    @pl.when(j == nhg)
    def _():
        _post_norm(x_ref, g3_ref, acc_scr, rs_scr, o_ref, 1.0)

    @pl.when(jnp.logical_and(i == pl.num_programs(0) - 1, j == nhg))
    def _():
        for s in range(nseg):
            for hd in range(heads):
                sout_ref[s, hd] = st_scr[s, hd].T


def _hgrn_mixer(x, g2, g3, w_in, hg_lower, norm_g, w_out, s0, *, layer, tm, chunk, seg):
    t, d = x.shape
    heads = d // HEAD_DIM
    nseg = chunk // seg
    tn = HG_TILE
    nh = tn // HEAD_DIM
    nhg = heads // nh
    assert t % tm == 0 and tm % chunk == 0 and heads % nh == 0 and tm % ROW_CHUNK == 0
    assert s0.shape == (nseg, heads, HEAD_DIM, HEAD_DIM)
    assert nseg == 1 or t == chunk
    sums, level_id, nb = _scan_tables(chunk, seg)
    body = functools.partial(_hgrn_body, layer=layer, chunk=chunk, seg=seg, nh=nh, nb=nb,
                             nchunk=tm // chunk, nhg=nhg)
    row_spec = pl.BlockSpec((tm, d), lambda i, j: (i, 0))
    vec_spec = pl.BlockSpec((1, d), lambda i, j: (0, 0))
    nsec = d // tn
    state_spec = pl.BlockSpec(s0.shape, lambda i, j: (0, 0, 0, 0))

    def proj(j):
        return jnp.minimum(j, nhg - 1)

    def outp(j):
        return jnp.maximum(j - 1, 0)

    scratch_set = [
        pltpu.VMEM((tm, tn), F32),
        pltpu.VMEM((tm, tn), F32),
        pltpu.VMEM((tm, tn), F32),
        pltpu.VMEM((tm, tn), BF16),
        pltpu.VMEM((tm, tn), F32),
        pltpu.VMEM((tm, tn), F32),
    ]
    return pl.pallas_call(
        body,
        out_shape=(jax.ShapeDtypeStruct((t, d), F32), jax.ShapeDtypeStruct(s0.shape, F32)),
        grid=(t // tm, nhg + 1),
        in_specs=[
            row_spec, vec_spec, vec_spec,
            pl.BlockSpec((d, tn), lambda i, j: (0, proj(j))),
            pl.BlockSpec((d, tn), lambda i, j: (0, nsec + proj(j))),
            pl.BlockSpec((d, tn), lambda i, j: (0, 2 * nsec + proj(j))),
            pl.BlockSpec((d, tn), lambda i, j: (0, 3 * nsec + proj(j))),
            pl.BlockSpec((hg_lower.shape[0], tn), lambda i, j: (0, proj(j))),
            pl.BlockSpec((1, HEAD_DIM), lambda i, j: (0, 0)),
            pl.BlockSpec((tn, d), lambda i, j: (outp(j), 0)),
            pl.BlockSpec(sums.shape, lambda i, j: (0, 0)),
            pl.BlockSpec(level_id.shape, lambda i, j: (0, 0)),
            state_spec,
        ],
        out_specs=(row_spec, state_spec),
        scratch_shapes=scratch_set + scratch_set + [
            pltpu.VMEM((tm, d), BF16),
            pltpu.VMEM((nb * chunk, tn), F32),
            pltpu.VMEM((nseg, heads, HEAD_DIM, HEAD_DIM), F32),
            pltpu.VMEM((tm, tn), BF16),
            pltpu.VMEM((tm, d), F32),
            pltpu.VMEM((tm, LANES), F32),
            pltpu.SMEM((2,), F32),
        ],
        compiler_params=_params(),
    )(x, g2, g3, w_in, w_in, w_in, w_in, hg_lower, norm_g, w_out,
      jnp.asarray(sums, BF16), jnp.asarray(level_id), s0)


def _trunk(x, hg_init, p, wts, *, tm, ffn_tm, sample):
    depth = p["norm_g"].shape[0]
    n_mix = 2
    hg_states, sg_vs = [], []

    def conversions(keys):
        jobs = []
        for key in keys:
            if key not in wts and key[0] == "ffn" and key[1] < depth:
                jobs.append((key, ((p["ffn_w_up"], key[1:], UP_CAST_ROWS), (p["ffn_w_down"], key[1:], DOWN_CAST_ROWS))))
            elif key not in wts and key[0] in ("sg", "hg"):
                jobs.append((key, ((p[key[0] + "_w_in"], key[1:], MIX_CAST_ROWS), (p[key[0] + "_w_out"], key[1:], MIX_CAST_ROWS))))
        return jobs

    def ffn(x, i, k, g_pre, g_post):
        nxt = ("ffn", i, 1) if k == 0 else ("ffn", i + 1, 0)
        mixer = [(("sg", "hg")[i % n_mix], i // n_mix)] if k == 0 else []
        jobs = conversions([nxt] + mixer)
        side = tuple(job for _, pair in jobs for job in pair)
        x, converted = _ffn(x, g_pre, g_post, *wts[("ffn", i, k)], tm=ffn_tm, side=side)
        for n, (key, _) in enumerate(jobs):
            wts[key] = converted[2 * n:2 * n + 2]
        return x

    for i in range(depth):
        g = p["norm_g"][i]
        row = lambda r: g[r:r + 1]
        jm = i // n_mix
        x = ffn(x, i, 0, row(0), row(1))
        if i % n_mix == 0:
            w_s, b_s = p["sg_w_s"][jm], p["sg_b_s"][jm]
            if sample is not None:
                n_streams, seq = sample
                w_s = jnp.tile(w_s[:, :seq, :seq], (1, n_streams, n_streams))
                b_s = jnp.tile(b_s[:, :seq], (1, n_streams))
            bias_rows = jnp.repeat(b_s.T, LANES, axis=1)
            w_in, w_out = wts[("sg", jm)]
            res = _sg_mixer(x, row(2), row(3), w_in, p["sg_ln_g"][jm:jm + 1],
                            p["sg_ln_b"][jm:jm + 1], w_s, bias_rows, w_out,
                            tm=tm, seg_mode=None if sample is None else sample[1],
                            emit_v=sample is not None)
            if sample is not None:
                x, v_rows = res
                sg_vs.append(v_rows)
            else:
                x = res
        else:
            w_in, w_out = wts[("hg", jm)]
            x, st = _hgrn_mixer(x, row(2), row(3), w_in, p["hg_lower"],
                                p["hg_norm_g"][jm:jm + 1], w_out, hg_init[jm],
                                layer=i, tm=tm, chunk=SG_CHUNK if sample is None else x.shape[0],
                                seg=SG_CHUNK if sample is None else sample[1])
            hg_states.append(st)
        x = ffn(x, i, 1, row(4), row(5))
    return x, hg_states, sg_vs


def kernel(x_prompt, x_sample, state_hgrn, norm_g, ffn_w_up, ffn_w_down, sg_w_in, sg_ln_g, sg_ln_b,
           sg_w_s, sg_b_s, sg_w_out, hg_w_in, hg_lower, hg_norm_g, hg_w_out):
    batch, seq, d = x_prompt.shape
    dec_batch, dec_seq, _ = x_sample.shape
    n_hg = hg_w_in.shape[0]
    heads = d // HEAD_DIM
    p = dict(norm_g=norm_g, ffn_w_up=ffn_w_up, ffn_w_down=ffn_w_down, sg_w_in=sg_w_in, sg_ln_g=sg_ln_g,
             sg_ln_b=sg_ln_b, sg_w_s=sg_w_s, sg_b_s=sg_b_s, sg_w_out=sg_w_out, hg_w_in=hg_w_in,
             hg_lower=hg_lower, hg_norm_g=hg_norm_g, hg_w_out=hg_w_out)
    wts = {("ffn", 0, 0): (ffn_w_up[0, 0].astype(BF16), ffn_w_down[0, 0].astype(BF16))}
    ys, sts = [], []
    for b in range(batch):
        init = [jnp.zeros((1, heads, HEAD_DIM, HEAD_DIM), F32) for _ in range(n_hg)]
        y, hg_p, _ = _trunk(x_prompt[b], init, p, wts, tm=512, ffn_tm=1024, sample=None)
        ys.append(y)
        sts.append(jnp.stack([s[0] for s in hg_p], axis=0))
    y_prompt = jnp.stack(ys, axis=0)
    state_hgrn_prompt = jnp.stack(sts, axis=1)

    init = [state_hgrn[jm].astype(F32) for jm in range(n_hg)]
    y, hg_s, sg_v = _trunk(x_sample.reshape(dec_batch * dec_seq, d), init, p, wts,
                           tm=dec_batch * dec_seq, ffn_tm=dec_batch * dec_seq, sample=(dec_batch, dec_seq))
    y_sample = y.reshape(dec_batch, dec_seq, d)
    state_hgrn_sample = jnp.stack(hg_s, axis=0)
    state_sg_v_sample = jnp.stack([v.reshape(dec_batch, dec_seq, d) for v in sg_v], axis=0)
    return (y_prompt, y_sample, state_hgrn_prompt, state_hgrn_sample, state_sg_v_sample)
```

```python
import functools

import numpy as np
import jax
import jax.numpy as jnp
from jax import lax
from jax.experimental import pallas as pl
from jax.experimental.pallas import tpu as pltpu

EPS = 1e-6
F32 = jnp.float32
BF16 = jnp.bfloat16

LANES = 128
ROW_CHUNK = 16
HEAD_DIM = 128
SG_CHUNK = 128
FFN_TILE = 512
SG_TILE = 512
UP_CAST_ROWS = 16
MIX_CAST_ROWS = 16
DOWN_CAST_ROWS = 64
HG_TILE = 256
STREAM_CHUNK = 64
V7X_VMEM_BYTES = 64 * 1024 * 1024
VMEM_LIMIT_BYTES = V7X_VMEM_BYTES - 8 * 1024 * 1024
MAX_SPLIT_DECAY = 140.0


def _dot(a, b):
    return jnp.dot(a, b, preferred_element_type=F32)


def _dot_nt(a, b):
    return lax.dot_general(a, b, (((1,), (1,)), ((), ())), preferred_element_type=F32)


def _dot_tn(a, b):
    return lax.dot_general(a, b, (((0,), (0,)), ((), ())), preferred_element_type=F32)


def _rms(x, g):
    return x * lax.rsqrt(jnp.mean(x * x, axis=-1, keepdims=True) + EPS) * g


def _silu(x):
    return x * (0.5 + 0.5 * jnp.tanh(0.5 * x))


def _gelu(x):
    return 0.5 * x * (1.0 + lax.erf(x * np.float32(np.sqrt(0.5))))


def _zero_after(v):
    u = pltpu.bitcast(v, jnp.uint32)
    u = lax.shift_right_logical(lax.shift_right_logical(u, jnp.uint32(16)), jnp.uint32(16))
    return u.astype(F32)


def _for_tiles(n_rows, n_cols, fn):
    def step(c, carry):
        rows = pl.ds(pl.multiple_of(c * ROW_CHUNK, ROW_CHUNK), ROW_CHUNK)
        for ct in range(n_cols // LANES):
            fn(rows, slice(ct * LANES, (ct + 1) * LANES))
        return carry
    lax.fori_loop(0, n_rows // ROW_CHUNK, step, 0, unroll=max(2, 32 * LANES // n_cols))


def _row_rsqrt(x):
    r = lax.rsqrt(jnp.mean(x * x, axis=-1, keepdims=True) + EPS)
    return jnp.broadcast_to(r, (x.shape[0], LANES))


def _pre_norm(x_ref, g_ref, h_scr, acc_scr, rs_scr):
    rs_scr[...] = _row_rsqrt(x_ref[...])

    def tile(rows, cols):
        h_scr[rows, cols] = (x_ref[rows, cols] * rs_scr[rows, :] * g_ref[:, cols]).astype(BF16)
        acc_scr[rows, cols] = jnp.zeros((ROW_CHUNK, LANES), F32)
    _for_tiles(x_ref.shape[0], x_ref.shape[1], tile)


def _post_norm(x_ref, g_ref, acc_scr, rs_scr, o_ref, scale):
    rs_scr[...] = _row_rsqrt(acc_scr[...])

    def tile(rows, cols):
        o_ref[rows, cols] = x_ref[rows, cols] + (acc_scr[rows, cols] * rs_scr[rows, :]) * (scale * g_ref[:, cols])
    _for_tiles(x_ref.shape[0], x_ref.shape[1], tile)


def _params(grid_rank=2):
    return pltpu.CompilerParams(dimension_semantics=("arbitrary",) * grid_rank,
                                vmem_limit_bytes=VMEM_LIMIT_BYTES)


def _ffn_body(*refs, nt, nslice, nside):
    xn_ref, xp_ref, gpre_ref, gpost_ref, wa_ref, wb_ref, wd_ref = refs[:7]
    side_src = refs[7:7 + nside]
    o_ref = refs[7 + nside]
    side_dst = refs[8 + nside:8 + 2 * nside]
    h0, h1, acc0, acc1 = refs[8 + 2 * nside:]
    r = pl.program_id(0)
    j = pl.program_id(1)
    rs = xn_ref.shape[0]
    base = jnp.minimum(j, nslice - 1) * rs

    def chunk_rows(c):
        return pl.ds(pl.multiple_of(base + c * ROW_CHUNK, ROW_CHUNK), ROW_CHUNK)

    def convert_side():
        for src, dst in zip(side_src, side_dst):
            dst[...] = src[...].astype(BF16)

    def stages(h_pre, h_main, acc_main, acc_post):
        def pre():
            for c in range(rs // ROW_CHUNK):
                xc = xn_ref[c * ROW_CHUNK:(c + 1) * ROW_CHUNK, :]
                h_pre[chunk_rows(c), :] = _rms(xc, gpre_ref[...]).astype(BF16)

        def main(zero=None):
            h = h_main[...]
            a = _dot(h, wa_ref[...])
            if zero is not None:
                a = a + jnp.tile(zero[0:1, :], (1, a.shape[1] // LANES))
            b = _dot(h, wb_ref[...])
            contrib = _dot((_silu(a) * b).astype(BF16), wd_ref[...])
            acc_main[...] = jnp.where(j == 0, contrib, acc_main[...] + contrib)

        def post():
            for c in range(rs // ROW_CHUNK):
                rows = slice(c * ROW_CHUNK, (c + 1) * ROW_CHUNK)
                o_ref[rows, :] = xp_ref[rows, :] + 0.5 * _rms(acc_post[chunk_rows(c), :], gpost_ref[...])

        steady = jnp.logical_and(r >= 2, r < nt)

        @pl.when(steady)
        def _():
            pre()
            post()
            convert_side()
            zero = _zero_after(h_pre[chunk_rows(0), 0:LANES]) + _zero_after(o_ref[0:8, 0:LANES])
            for dst in side_dst:
                zero = zero + _zero_after(dst[0:ROW_CHUNK, 0:LANES])
            main(zero)

        @pl.when(jnp.logical_not(steady))
        def _():
            @pl.when(jnp.logical_and(r == 0, j == 0))
            def _():
                acc0[...] = jnp.zeros_like(acc0)
                acc1[...] = jnp.zeros_like(acc1)
            pl.when(r < nt)(pre)
            pl.when(jnp.logical_and(r >= 1, r <= nt))(main)
            pl.when(r >= 2)(post)
            convert_side()

    pl.when(r % 2 == 0)(lambda: stages(h0, h1, acc1, acc0))
    pl.when(r % 2 == 1)(lambda: stages(h1, h0, acc0, acc1))


def _ffn_single_body(x_ref, gpre_ref, gpost_ref, wa_ref, wb_ref, wd_ref, o_ref, h_scr, acc_scr):
    j = pl.program_id(0)

    @pl.when(j == 0)
    def _():
        h_scr[...] = _rms(x_ref[...], gpre_ref[...]).astype(BF16)
        acc_scr[...] = jnp.zeros_like(acc_scr)

    h = h_scr[...]
    a = _dot(h, wa_ref[...])
    b = _dot(h, wb_ref[...])
    acc_scr[...] += _dot((_silu(a) * b).astype(BF16), wd_ref[...])

    @pl.when(j == pl.num_programs(0) - 1)
    def _():
        o_ref[...] = x_ref[...] + 0.5 * _rms(acc_scr[...], gpost_ref[...])


def _ffn(x, g_pre, g_post, w_up, w_down, *, tm, side=()):
    t, d = x.shape
    f = w_down.shape[0]
    tf = FFN_TILE
    nf = f // tf
    nt = t // tm
    nslice = min(nf, tm // ROW_CHUNK)
    while tm % (nslice * ROW_CHUNK):
        nslice -= 1
    rs = tm // nslice
    assert t % tm == 0 and f % tf == 0
    if nt == 1 and not side:
        out = pl.pallas_call(
            _ffn_single_body,
            out_shape=jax.ShapeDtypeStruct((t, d), F32),
            grid=(nf,),
            in_specs=[
                pl.BlockSpec((t, d), lambda j: (0, 0)),
                pl.BlockSpec((1, d), lambda j: (0, 0)),
                pl.BlockSpec((1, d), lambda j: (0, 0)),
                pl.BlockSpec((d, tf), lambda j: (0, j)),
                pl.BlockSpec((d, tf), lambda j: (0, j + nf)),
                pl.BlockSpec((tf, d), lambda j: (j, 0)),
            ],
            out_specs=pl.BlockSpec((t, d), lambda j: (0, 0)),
            scratch_shapes=[pltpu.VMEM((t, d), BF16), pltpu.VMEM((t, d), F32)],
            compiler_params=_params(1),
        )(x, g_pre, g_post, w_up, w_up, w_down)
        return out, ()

    def w_step(r, j):
        return jnp.where(r == 0, 0, jnp.where(r > nt, nf - 1, j))

    def next_slice(r, j):
        return jnp.where(r < nt, r * nslice + jnp.minimum(j, nslice - 1), nt * nslice - 1)

    def prev_slice(r, j):
        return jnp.where(r >= 2, (r - 2) * nslice + jnp.minimum(j, nslice - 1), 0)

    side_in, side_out, side_shapes = [], [], []
    for src, lead, rb in side:
        rows, cols = src.shape[-2:]
        nblk = rows // rb
        assert rows % rb == 0 and nblk <= nt * nf and len(lead) == src.ndim - 2

        def blk(r, j, nblk=nblk):
            return jnp.clip((r - 1) * nf + j, 0, nblk - 1)

        side_in.append(pl.BlockSpec((None,) * len(lead) + (rb, cols),
                                    lambda r, j, lead=lead, blk=blk: (*lead, blk(r, j), 0)))
        side_out.append(pl.BlockSpec((rb, cols), lambda r, j, blk=blk: (blk(r, j), 0)))
        side_shapes.append(jax.ShapeDtypeStruct((rows, cols), BF16))

    out = pl.pallas_call(
        functools.partial(_ffn_body, nt=nt, nslice=nslice, nside=len(side)),
        out_shape=[jax.ShapeDtypeStruct((t, d), F32)] + side_shapes,
        grid=(nt + 2, nf),
        in_specs=[
            pl.BlockSpec((rs, d), lambda r, j: (next_slice(r, j), 0)),
            pl.BlockSpec((rs, d), lambda r, j: (prev_slice(r, j), 0)),
            pl.BlockSpec((1, d), lambda r, j: (0, 0)),
            pl.BlockSpec((1, d), lambda r, j: (0, 0)),
            pl.BlockSpec((d, tf), lambda r, j: (0, w_step(r, j))),
            pl.BlockSpec((d, tf), lambda r, j: (0, w_step(r, j) + nf)),
            pl.BlockSpec((tf, d), lambda r, j: (w_step(r, j), 0)),
        ] + side_in,
        out_specs=[pl.BlockSpec((rs, d), lambda r, j: (prev_slice(r, j), 0))] + side_out,
        scratch_shapes=[pltpu.VMEM((tm, d), BF16), pltpu.VMEM((tm, d), BF16),
                        pltpu.VMEM((tm, d), F32), pltpu.VMEM((tm, d), F32)],
        compiler_params=_params(),
    )(x, x, g_pre, g_post, w_up, w_up, w_down, *[src for src, _, _ in side])
    return out[0], tuple(out[1:])


def _sg_mask(seg_mode):
    row = lax.broadcasted_iota(jnp.int32, (SG_CHUNK, SG_CHUNK), 0)
    col = lax.broadcasted_iota(jnp.int32, (SG_CHUNK, SG_CHUNK), 1)
    if seg_mode is None:
        return (row // STREAM_CHUNK) >= (col // STREAM_CHUNK)
    return (row // seg_mode) == (col // seg_mode)


def _sg_body(x_ref, g2_ref, g3_ref, wu_ref, wv_ref, lng_ref, lnb_ref, ws_ref, bias_ref, wo_ref,
             *rest, ng, gpt, nchunk, seg_mode, emit_v):
    if emit_v:
        o_ref, vout_ref, h_scr, u_scr, v_scr, y_scr, acc_scr, rs_scr = rest
    else:
        o_ref, h_scr, u_scr, v_scr, y_scr, acc_scr, rs_scr = rest
        vout_ref = None
    j = pl.program_id(1)
    tn = gpt * LANES
    d = ng * tn
    tm = x_ref.shape[0]

    @pl.when(j == 0)
    def _():
        _pre_norm(x_ref, g2_ref, h_scr, acc_scr, rs_scr)

    @pl.when(j < ng)
    def _():
        h = h_scr[...]
        u_scr[j] = _gelu(_dot(h, wu_ref[...]))
        v_scr[j] = _gelu(_dot(h, wv_ref[...]))

    @pl.when(j == ng)
    def _():
        mu = sum(jnp.sum(v_scr[jj], axis=-1, keepdims=True) for jj in range(ng)) * (1.0 / d)
        var = sum(jnp.sum(jnp.square(v_scr[jj] - mu), axis=-1, keepdims=True)
                  for jj in range(ng)) * (1.0 / d)
        rstd = lax.rsqrt(var + EPS)
        for jj in range(ng):
            cols = slice(jj * tn, (jj + 1) * tn)
            vn = (v_scr[jj] - mu) * rstd * lng_ref[:, cols] + lnb_ref[:, cols]
            v_scr[jj] = vn
            if emit_v:
                vout_ref[:, cols] = vn

    @pl.when(j >= ng)
    def _():
        jj = j - ng
        mask = _sg_mask(seg_mode)
        for g in range(gpt):
            cols = slice(g * LANES, (g + 1) * LANES)
            w = jnp.where(mask, ws_ref[g], 0.0).astype(BF16)
            for n in range(nchunk):
                rows = slice(n * SG_CHUNK, (n + 1) * SG_CHUNK)
                s = _dot(w, v_scr[jj, rows, cols].astype(BF16)) + bias_ref[:, cols]
                y_scr[rows, cols] = (u_scr[jj, rows, cols] * s).astype(BF16)
        acc_scr[...] += _dot(y_scr[...], wo_ref[...])

    @pl.when(j == 2 * ng - 1)
    def _():
        _post_norm(x_ref, g3_ref, acc_scr, rs_scr, o_ref, 1.0)


def _sg_mixer(x, g2, g3, w_in, ln_g, ln_b, w_s, bias_rows, w_out, *, tm, seg_mode, emit_v):
    t, d = x.shape
    tn = SG_TILE
    gpt = tn // LANES
    ng = d // tn
    assert t % tm == 0 and tm % SG_CHUNK == 0 and d % tn == 0 and tm % ROW_CHUNK == 0
    last = ng - 1
    body = functools.partial(_sg_body, ng=ng, gpt=gpt, nchunk=tm // SG_CHUNK,
                             seg_mode=seg_mode, emit_v=emit_v)
    row_spec = pl.BlockSpec((tm, d), lambda i, j: (i, 0))
    vec_spec = pl.BlockSpec((1, d), lambda i, j: (0, 0))
    out_shape = jax.ShapeDtypeStruct((t, d), F32)
    return pl.pallas_call(
        body,
        out_shape=(out_shape, out_shape) if emit_v else out_shape,
        grid=(t // tm, 2 * ng),
        in_specs=[
            row_spec, vec_spec, vec_spec,
            pl.BlockSpec((d, tn), lambda i, j: (0, jnp.minimum(j, last))),
            pl.BlockSpec((d, tn), lambda i, j: (0, ng + jnp.minimum(j, last))),
            vec_spec, vec_spec,
            pl.BlockSpec((gpt, SG_CHUNK, SG_CHUNK), lambda i, j: (jnp.maximum(j - ng, 0), 0, 0)),
            pl.BlockSpec((SG_CHUNK, tn), lambda i, j: (0, jnp.maximum(j - ng, 0))),
            pl.BlockSpec((tn, d), lambda i, j: (jnp.maximum(j - ng, 0), 0)),
        ],
        out_specs=(row_spec, row_spec) if emit_v else row_spec,
        scratch_shapes=[
            pltpu.VMEM((tm, d), BF16),
            pltpu.VMEM((ng, tm, tn), F32),
            pltpu.VMEM((ng, tm, tn), F32),
            pltpu.VMEM((tm, tn), BF16),
            pltpu.VMEM((tm, d), F32),
            pltpu.VMEM((tm, LANES), F32),
        ],
        compiler_params=_params(),
    )(x, g2, g3, w_in, w_in, ln_g, ln_b, w_s, bias_rows, w_out)


def _scan_tables(chunk, seg):
    levels = []
    m = seg // 2
    while m >= 1:
        levels.append(m)
        m //= 2
    nb = 2 + len(levels)
    t = np.arange(chunk)[:, None]
    j = np.arange(chunk)[None, :]
    same_seg = (t // seg) == (j // seg)
    sums = np.zeros((nb, chunk, chunk), np.float32)
    sums[0] = same_seg & (j <= t)
    sums[1] = same_seg & (j > t)
    level_id = np.where(t == j, 0, -1).astype(np.int32)
    for li, m in enumerate(levels):
        g = 2 * m
        ref = (t // g) * g + m - 1
        odd = (t % g) >= m
        sums[2 + li] = np.where(odd, (j > ref) & (j <= t), (j > t) & (j <= ref))
        pair = ((t // g) == (j // g)) & odd & ((j % g) < m)
        level_id = np.where(pair, li + 1, level_id).astype(np.int32)
    sums = sums.reshape(nb * chunk, chunk)
    return np.concatenate([sums, sums, sums], axis=1), level_id, nb


def _split3(x):
    hi = x.astype(BF16)
    r1 = x - hi.astype(F32)
    mid = r1.astype(BF16)
    lo = (r1 - mid.astype(F32)).astype(BF16)
    return jnp.concatenate([hi, mid, lo], axis=0)


def _hgrn_body(x_ref, g2_ref, g3_ref, wq_ref, wf_ref, wi_ref, wg_ref, hl_ref, ng_ref, wo_ref,
               sums_ref, lid_ref, s0_ref, o_ref, sout_ref, *scratch,
               layer, chunk, seg, nh, nb, nchunk, nhg):
    sets = (scratch[0:6], scratch[6:12])
    h_scr, e_scr, st_scr, on_scr, acc_scr, rs_scr, worst_smem = scratch[12:]
    i = pl.program_id(0)
    j = pl.program_id(1)
    nseg = chunk // seg
    nlev = nb - 2
    heads = nhg * nh
    lid = lid_ref[...]

    @pl.when(j == 0)
    def _():
        _pre_norm(x_ref, g2_ref, h_scr, acc_scr, rs_scr)

    @pl.when(jnp.logical_and(i == 0, j == 0))
    def _():
        for s in range(nseg):
            for hd in range(heads):
                st_scr[s, hd] = s0_ref[s, hd].T
        worst_smem[0] = jnp.float32(0.0)
        worst_smem[1] = jnp.float32(0.0)

    def project(p):
        q_scr, lf_scr, k_scr, v_scr, gate_scr, b_scr = sets[p]
        hl = hl_ref[...]
        e = jnp.exp(hl - jnp.max(hl, axis=0, keepdims=True))
        pr = e / jnp.sum(e, axis=0, keepdims=True)
        lb = jnp.zeros_like(pr[0:1])
        for r in range(1, layer + 1):
            lb = lb + pr[r:r + 1]
        h = h_scr[...]
        q_scr[...] = _silu(_dot(h, wq_ref[...]))
        f = lb + (1.0 - lb) * jax.nn.sigmoid(_dot(h, wf_ref[...]))
        lf = jnp.log(f)
        lf_scr[...] = lf
        k_scr[...] = 1.0 - f
        v_scr[...] = _dot(h, wi_ref[...]).astype(BF16)
        gate_scr[...] = _silu(_dot(h, wg_ref[...]))
        if nseg == 1:
            worst = jnp.zeros((1, lf.shape[1]), F32)
            for c in range(nchunk):
                rows = slice(c * chunk, (c + 1) * chunk)
                b = _dot(sums_ref[0:chunk, :], _split3(lf[rows, :]))
                b_scr[rows, :] = b
                worst = jnp.maximum(worst, -b[chunk - 1:chunk, :])
            worst_smem[p] = jnp.max(worst)

    def scan(p, split):
        q_scr, lf_scr, k_scr, v_scr, gate_scr, b_scr = sets[p]
        head0 = (j - 1) * nh

        def carry_state(rows, cols, head, o_intra, q_pre, k_end, vh, e_end):
            inter = []
            for s in range(nseg):
                srows = slice(s * seg, (s + 1) * seg)
                st = st_scr[s, head]
                inter.append(_dot_nt(q_pre[srows], st.astype(BF16)))
                st_scr[s, head] = st * e_end[s] + _dot_tn(vh[srows], k_end[srows])
            o = o_intra + (inter[0] if nseg == 1 else jnp.concatenate(inter, axis=0))
            on_scr[rows, cols] = (_rms(o, ng_ref[...]) * gate_scr[rows, cols]).astype(BF16)

        for c in range(nchunk):
            rows = slice(c * chunk, (c + 1) * chunk)
            if split:
                b = b_scr[rows, :]
                half = 0.5 * b[chunk - 1:chunk, :]
                x = b - half
                e_q = jnp.exp(x)
                e_k = jnp.exp(-x)
                e_half = jnp.exp(half)
            else:
                e_scr[...] = jnp.exp(_dot(sums_ref[...], _split3(lf_scr[rows, :])))
            for hh in range(nh):
                cols = slice(hh * HEAD_DIM, (hh + 1) * HEAD_DIM)
                vh = v_scr[rows, cols]
                if split:
                    qt = q_scr[rows, cols] * e_q[:, cols]
                    kt = k_scr[rows, cols] * e_k[:, cols]
                    eh = e_half[:, cols]
                    sc = jnp.where(lid >= 0, _dot_nt(qt.astype(BF16), kt.astype(BF16)), 0.0)
                    carry_state(rows, cols, head0 + hh, _dot(sc.astype(BF16), vh),
                                (qt * eh).astype(BF16), (kt * eh).astype(BF16), vh, [eh * eh])
                else:
                    qh = q_scr[rows, cols]
                    kh = k_scr[rows, cols]
                    e_pre = e_scr[0:chunk, cols]
                    k_end = (kh * e_scr[chunk:2 * chunk, cols]).astype(BF16)
                    sc = jnp.where(lid == 0, _dot_nt(qh.astype(BF16), kh.astype(BF16)), 0.0)
                    for l in range(nlev):
                        el = e_scr[(2 + l) * chunk:(3 + l) * chunk, cols]
                        sc_l = _dot_nt((qh * el).astype(BF16), (kh * el).astype(BF16))
                        sc = jnp.where(lid == l + 1, sc_l, sc)
                    e_end = [e_pre[(s + 1) * seg - 1:(s + 1) * seg, :] for s in range(nseg)]
                    carry_state(rows, cols, head0 + hh, _dot(sc.astype(BF16), vh),
                                (qh * e_pre).astype(BF16), k_end, vh, e_end)
        acc_scr[...] += _dot(on_scr[...], wo_ref[...])

    def both(p, split):
        project(p)
        scan(1 - p, split)

    for p in (0, 1):
        mine = (j % 2) == p
        inner = jnp.logical_and(mine, jnp.logical_and(j >= 1, j < nhg))
        last = jnp.logical_and(mine, j == nhg)
        if p == 0:
            pl.when(j == 0)(functools.partial(project, 0))
        if nseg == 1:
            benign = worst_smem[1 - p] <= MAX_SPLIT_DECAY
            pl.when(jnp.logical_and(inner, benign))(functools.partial(both, p, True))
            pl.when(jnp.logical_and(inner, jnp.logical_not(benign)))(functools.partial(both, p, False))
            if nhg % 2 == p:
                pl.when(jnp.logical_and(last, benign))(functools.partial(scan, 1 - p, True))
                pl.when(jnp.logical_and(last, jnp.logical_not(benign)))(functools.partial(scan, 1 - p, False))
        else:
            pl.when(inner)(functools.partial(both, p, False))
            if nhg % 2 == p:
                pl.when(last)(functools.partial(scan, 1 - p, False))

    @pl.when(j == nhg)
    def _():
        _post_norm(x_ref, g3_ref, acc_scr, rs_scr, o_ref, 1.0)

    @pl.when(jnp.logical_and(i == pl.num_programs(0) - 1, j == nhg))
    def _():
        for s in range(nseg):
            for hd in range(heads):
                sout_ref[s, hd] = st_scr[s, hd].T


def _hgrn_mixer(x, g2, g3, w_in, hg_lower, norm_g, w_out, s0, *, layer, tm, chunk, seg):
    t, d = x.shape
    heads = d // HEAD_DIM
    nseg = chunk // seg
    tn = HG_TILE
    nh = tn // HEAD_DIM
    nhg = heads // nh
    assert t % tm == 0 and tm % chunk == 0 and heads % nh == 0 and tm % ROW_CHUNK == 0
    assert s0.shape == (nseg, heads, HEAD_DIM, HEAD_DIM)
    assert nseg == 1 or t == chunk
    sums, level_id, nb = _scan_tables(chunk, seg)
    body = functools.partial(_hgrn_body, layer=layer, chunk=chunk, seg=seg, nh=nh, nb=nb,
                             nchunk=tm // chunk, nhg=nhg)
    row_spec = pl.BlockSpec((tm, d), lambda i, j: (i, 0))
    vec_spec = pl.BlockSpec((1, d), lambda i, j: (0, 0))
    nsec = d // tn
    state_spec = pl.BlockSpec(s0.shape, lambda i, j: (0, 0, 0, 0))

    def proj(j):
        return jnp.minimum(j, nhg - 1)

    def outp(j):
        return jnp.maximum(j - 1, 0)

    scratch_set = [
        pltpu.VMEM((tm, tn), F32),
        pltpu.VMEM((tm, tn), F32),
        pltpu.VMEM((tm, tn), F32),
        pltpu.VMEM((tm, tn), BF16),
        pltpu.VMEM((tm, tn), F32),
        pltpu.VMEM((tm, tn), F32),
    ]
    return pl.pallas_call(
        body,
        out_shape=(jax.ShapeDtypeStruct((t, d), F32), jax.ShapeDtypeStruct(s0.shape, F32)),
        grid=(t // tm, nhg + 1),
        in_specs=[
            row_spec, vec_spec, vec_spec,
            pl.BlockSpec((d, tn), lambda i, j: (0, proj(j))),
            pl.BlockSpec((d, tn), lambda i, j: (0, nsec + proj(j))),
            pl.BlockSpec((d, tn), lambda i, j: (0, 2 * nsec + proj(j))),
            pl.BlockSpec((d, tn), lambda i, j: (0, 3 * nsec + proj(j))),
            pl.BlockSpec((hg_lower.shape[0], tn), lambda i, j: (0, proj(j))),
            pl.BlockSpec((1, HEAD_DIM), lambda i, j: (0, 0)),
            pl.BlockSpec((tn, d), lambda i, j: (outp(j), 0)),
            pl.BlockSpec(sums.shape, lambda i, j: (0, 0)),
            pl.BlockSpec(level_id.shape, lambda i, j: (0, 0)),
            state_spec,
        ],
        out_specs=(row_spec, state_spec),
        scratch_shapes=scratch_set + scratch_set + [
            pltpu.VMEM((tm, d), BF16),
            pltpu.VMEM((nb * chunk, tn), F32),
            pltpu.VMEM((nseg, heads, HEAD_DIM, HEAD_DIM), F32),
            pltpu.VMEM((tm, tn), BF16),
            pltpu.VMEM((tm, d), F32),
            pltpu.VMEM((tm, LANES), F32),
            pltpu.SMEM((2,), F32),
        ],
        compiler_params=_params(),
    )(x, g2, g3, w_in, w_in, w_in, w_in, hg_lower, norm_g, w_out,
      jnp.asarray(sums, BF16), jnp.asarray(level_id), s0)


def _trunk(x, hg_init, p, wts, *, tm, ffn_tm, sample):
    depth = p["norm_g"].shape[0]
    n_mix = 2
    hg_states, sg_vs = [], []

    def conversions(keys):
        jobs = []
        for key in keys:
            if key not in wts and key[0] == "ffn" and key[1] < depth:
                jobs.append((key, ((p["ffn_w_up"], key[1:], UP_CAST_ROWS), (p["ffn_w_down"], key[1:], DOWN_CAST_ROWS))))
            elif key not in wts and key[0] in ("sg", "hg"):
                jobs.append((key, ((p[key[0] + "_w_in"], key[1:], MIX_CAST_ROWS), (p[key[0] + "_w_out"], key[1:], MIX_CAST_ROWS))))
        return jobs

    def ffn(x, i, k, g_pre, g_post):
        nxt = ("ffn", i, 1) if k == 0 else ("ffn", i + 1, 0)
        mixer = [(("sg", "hg")[i % n_mix], i // n_mix)] if k == 0 else []
        jobs = conversions([nxt] + mixer)
        side = tuple(job for _, pair in jobs for job in pair)
        x, converted = _ffn(x, g_pre, g_post, *wts[("ffn", i, k)], tm=ffn_tm, side=side)
        for n, (key, _) in enumerate(jobs):
            wts[key] = converted[2 * n:2 * n + 2]
        return x

    for i in range(depth):
        g = p["norm_g"][i]
        row = lambda r: g[r:r + 1]
        jm = i // n_mix
        x = ffn(x, i, 0, row(0), row(1))
        if i % n_mix == 0:
            w_s, b_s = p["sg_w_s"][jm], p["sg_b_s"][jm]
            if sample is not None:
                n_streams, seq = sample
                w_s = jnp.tile(w_s[:, :seq, :seq], (1, n_streams, n_streams))
                b_s = jnp.tile(b_s[:, :seq], (1, n_streams))
            bias_rows = jnp.repeat(b_s.T, LANES, axis=1)
            w_in, w_out = wts[("sg", jm)]
            res = _sg_mixer(x, row(2), row(3), w_in, p["sg_ln_g"][jm:jm + 1],
                            p["sg_ln_b"][jm:jm + 1], w_s, bias_rows, w_out,
                            tm=tm, seg_mode=None if sample is None else sample[1],
                            emit_v=sample is not None)
            if sample is not None:
                x, v_rows = res
                sg_vs.append(v_rows)
            else:
                x = res
        else:
            w_in, w_out = wts[("hg", jm)]
            x, st = _hgrn_mixer(x, row(2), row(3), w_in, p["hg_lower"],
                                p["hg_norm_g"][jm:jm + 1], w_out, hg_init[jm],
                                layer=i, tm=tm, chunk=SG_CHUNK if sample is None else x.shape[0],
                                seg=SG_CHUNK if sample is None else sample[1])
            hg_states.append(st)
        x = ffn(x, i, 1, row(4), row(5))
    return x, hg_states, sg_vs


def kernel(x_prompt, x_sample, state_hgrn, norm_g, ffn_w_up, ffn_w_down, sg_w_in, sg_ln_g, sg_ln_b,
           sg_w_s, sg_b_s, sg_w_out, hg_w_in, hg_lower, hg_norm_g, hg_w_out):
    batch, seq, d = x_prompt.shape
    dec_batch, dec_seq, _ = x_sample.shape
    n_hg = hg_w_in.shape[0]
    heads = d // HEAD_DIM
    p = dict(norm_g=norm_g, ffn_w_up=ffn_w_up, ffn_w_down=ffn_w_down, sg_w_in=sg_w_in, sg_ln_g=sg_ln_g,
             sg_ln_b=sg_ln_b, sg_w_s=sg_w_s, sg_b_s=sg_b_s, sg_w_out=sg_w_out, hg_w_in=hg_w_in,
             hg_lower=hg_lower, hg_norm_g=hg_norm_g, hg_w_out=hg_w_out)
    wts = {("ffn", 0, 0): (ffn_w_up[0, 0].astype(BF16), ffn_w_down[0, 0].astype(BF16))}
    ys, sts = [], []
    for b in range(batch):
        init = [jnp.zeros((1, heads, HEAD_DIM, HEAD_DIM), F32) for _ in range(n_hg)]
        y, hg_p, _ = _trunk(x_prompt[b], init, p, wts, tm=512, ffn_tm=1024, sample=None)
        ys.append(y)
        sts.append(jnp.stack([s[0] for s in hg_p], axis=0))
    y_prompt = jnp.stack(ys, axis=0)
    state_hgrn_prompt = jnp.stack(sts, axis=1)

    init = [state_hgrn[jm].astype(F32) for jm in range(n_hg)]
    y, hg_s, sg_v = _trunk(x_sample.reshape(dec_batch * dec_seq, d), init, p, wts,
                           tm=dec_batch * dec_seq, ffn_tm=dec_batch * dec_seq, sample=(dec_batch, dec_seq))
    y_sample = y.reshape(dec_batch, dec_seq, d)
    state_hgrn_sample = jnp.stack(hg_s, axis=0)
    state_sg_v_sample = jnp.stack([v.reshape(dec_batch, dec_seq, d) for v in sg_v], axis=0)
    return (y_prompt, y_sample, state_hgrn_prompt, state_hgrn_sample, state_sg_v_sample)
```
